```python
import math
import jax
import jax.numpy as jnp
from jax import lax
import numpy as np

D_MODEL = 1024
BATCH = 1
SEQ = 16384
DEPTH = 1
DEC_BATCH = 32
DEC_SEQ = 64
PAST_LEN = 2048

CHUNK = 64
MIX_WIDTH = D_MODEL
A_WIDTH = MIX_WIDTH // 2
D_HEAD_A = 64
H_A = A_WIDTH // (2 * D_HEAD_A)
ROPE_DIM = D_HEAD_A // 4
ROPE_THETA = 500000.0
Q_BLOCK = 128
B_WIDTH = MIX_WIDTH - A_WIDTH
D_HEAD_B = 64
H_B = B_WIDTH // D_HEAD_B
BAND_CHUNKS = 8
BAND_PAST = BAND_CHUNKS * CHUNK
REL_CLIP = 128
A_QK_COLS = 2 * H_A * D_HEAD_A
A_V_COLS = H_A * 2 * D_HEAD_A
B_COLS = H_B * D_HEAD_B
IN_COLS = 2 * A_QK_COLS + A_V_COLS + 3 * B_COLS
N_EXPERTS = 32
TOP_K = 4
D_FF = D_MODEL
SWIGLU_LIMIT = 7.0
SWIGLU_ALPHA = 1.702
MOE_BLOCK = 128
LN_EPS = 1e-5
SUBLN_EPS = 1e-5
DEEPNORM_ALPHA = (2.0 * DEPTH) ** 0.25
DEEPNORM_BETA = (8.0 * DEPTH) ** -0.25
NEG_INF = -1e30

kernel_name = "hybrid_diffattn_chunkband_moe_stream_step"


def lambda_init_for(layer_idx):
    return 0.8 - 0.6 * math.exp(-0.3 * layer_idx)


def layer_norm(x, g, b):
    xf = x.astype(jnp.float32)
    mu = jnp.mean(xf, axis=-1, keepdims=True)
    var = jnp.mean(jnp.square(xf - mu), axis=-1, keepdims=True)
    return ((xf - mu) * lax.rsqrt(var + LN_EPS) * g + b).astype(x.dtype)


def rope(x, pos):
    half = ROPE_DIM // 2
    inv = ROPE_THETA ** (-jnp.arange(half, dtype=jnp.float32) * 2.0 / ROPE_DIM)
    ang = pos.astype(jnp.float32)[:, None] * inv[None, :]
    cos = jnp.cos(ang)[None, :, None, :]
    sin = jnp.sin(ang)[None, :, None, :]
    xf = x.astype(jnp.float32)
    x1 = xf[..., :half]
    x2 = xf[..., half:ROPE_DIM]
    out = jnp.concatenate([x1 * cos - x2 * sin, x2 * cos + x1 * sin, xf[..., ROPE_DIM:]], axis=-1)
    return out.astype(x.dtype)


def project(x, w):
    b, s, _ = x.shape
    h = x @ w
    o1 = A_QK_COLS
    o2 = o1 + A_QK_COLS
    o3 = o2 + A_V_COLS
    o4 = o3 + B_COLS
    o5 = o4 + B_COLS
    qa = h[..., :o1].reshape(b, s, 2 * H_A, D_HEAD_A)
    ka = h[..., o1:o2].reshape(b, s, 2 * H_A, D_HEAD_A)
    va = h[..., o2:o3].reshape(b, s, H_A, 2 * D_HEAD_A)
    qb = h[..., o3:o4].reshape(b, s, H_B, D_HEAD_B)
    kb = h[..., o4:o5].reshape(b, s, H_B, D_HEAD_B)
    vb = h[..., o5:].reshape(b, s, H_B, D_HEAD_B)
    return qa, ka, va, qb, kb, vb


def diff_attn(q, k, v, qpos, kpos, lam):
    s = jnp.einsum('bqhd,bkhd->bhqk', q, k, preferred_element_type=jnp.float32) * (D_HEAD_A ** -0.5)
    mask = (kpos // CHUNK)[None, :] <= (qpos // CHUNK)[:, None]
    p = jax.nn.softmax(jnp.where(mask, s, NEG_INF), axis=-1)
    b, hh, sq, sk = p.shape
    p = p.reshape(b, hh // 2, 2, sq, sk)
    a = p[:, :, 0] - lam * p[:, :, 1]
    return jnp.einsum('bhqk,bkhe->bqhe', a, v.astype(jnp.float32))


def diff_attn_blocked(q, k, v, pos, lam):
    b, s, hh, d = q.shape
    nb = s // Q_BLOCK
    qb = q.reshape(b, nb, Q_BLOCK, hh, d).transpose(1, 0, 2, 3, 4)
    pb = pos.reshape(nb, Q_BLOCK)
    out = lax.map(lambda a: diff_attn(a[0], k, v, a[1], pos, lam), (qb, pb))
    return out.transpose(1, 0, 2, 3, 4).reshape(b, s, H_A, 2 * D_HEAD_A)


def head_rmsnorm(o, g, lam_init):
    of = o.astype(jnp.float32)
    of = of * lax.rsqrt(jnp.mean(jnp.square(of), axis=-1, keepdims=True) + SUBLN_EPS)
    return of * g * (1.0 - lam_init)


def band_keys(k):
    b, s, h, d = k.shape
    n = s // CHUNK
    kp = jnp.pad(k, ((0, 0), (BAND_PAST, 0), (0, 0), (0, 0))).reshape(b, n + BAND_CHUNKS, CHUNK, h, d)
    return jnp.concatenate([kp[:, j:j + n] for j in range(BAND_CHUNKS + 1)], axis=2)


def band_attn(q, k, v, rel, valid, bias_table):
    bias = bias_table[:, jnp.clip(rel, -REL_CLIP, REL_CLIP) + REL_CLIP].astype(jnp.float32)
    s = jnp.einsum('bnqhd,bnkhd->bnhqk', q, k, preferred_element_type=jnp.float32) * (D_HEAD_B ** -0.5)
    s = s + bias[None, None]
    p = jax.nn.softmax(jnp.where(valid[None, :, None], s, NEG_INF), axis=-1)
    return jnp.einsum('bnhqk,bnkhd->bnqhd', p, v.astype(jnp.float32))


def moe(x2d, w_router, b_router, w_gate_up, b_gate_up, w_down, b_down):
    t, d = x2d.shape
    logits = (x2d @ w_router + b_router).astype(jnp.float32)
    top_v, top_i = lax.top_k(logits, TOP_K)
    gates = jax.nn.softmax(top_v, axis=-1)
    n_assign = t * TOP_K
    flat_e = top_i.reshape(-1)
    flat_t = jnp.repeat(jnp.arange(t, dtype=jnp.int32), TOP_K)
    flat_g = gates.reshape(-1)
    order = jnp.argsort(flat_e)
    se = flat_e[order]
    counts = jnp.bincount(flat_e, length=N_EXPERTS)
    padded = ((counts + MOE_BLOCK - 1) // MOE_BLOCK) * MOE_BLOCK
    pad_end = jnp.cumsum(padded)
    pad_start = pad_end - padded
    start = jnp.cumsum(counts) - counts
    dest = pad_start[se] + jnp.arange(n_assign) - start[se]
    nblk = -(-n_assign // MOE_BLOCK) + N_EXPERTS
    p_rows = nblk * MOE_BLOCK
    buf_t = jnp.full((p_rows,), t, jnp.int32).at[dest].set(flat_t[order])
    buf_g = jnp.zeros((p_rows,), jnp.float32).at[dest].set(flat_g[order])
    blk_e = jnp.minimum(jnp.searchsorted(pad_end, jnp.arange(nblk) * MOE_BLOCK, side='right'), N_EXPERTS - 1)
    xpad = jnp.concatenate([x2d, jnp.zeros((1, d), x2d.dtype)], axis=0)
    xb = xpad[buf_t].reshape(nblk, MOE_BLOCK, d)

    def expert_block(args):
        xe, e = args
        h = (xe @ w_gate_up[e] + b_gate_up[e]).astype(jnp.float32)
        gate = jnp.minimum(h[..., ::2], SWIGLU_LIMIT)
        up = jnp.clip(h[..., 1::2], -SWIGLU_LIMIT, SWIGLU_LIMIT)
        glu = gate * jax.nn.sigmoid(gate * SWIGLU_ALPHA)
        return ((up + 1.0) * glu).astype(xe.dtype) @ w_down[e] + b_down[e]

    yb = lax.map(expert_block, (xb, blk_e)).reshape(p_rows, d)
    y = jnp.zeros((t + 1, d), jnp.float32).at[buf_t].add(yb.astype(jnp.float32) * buf_g[:, None])
    return y[:t].astype(x2d.dtype)


def merge_and_channel_mix(x, oa, ob, lam_init, subln_g, w_out, ln1_g, ln1_b, w_router, b_router,
                          w_gate_up, b_gate_up, w_down, b_down, ln2_g, ln2_b):
    b, s, d = x.shape
    oa = head_rmsnorm(oa, subln_g, lam_init).reshape(b, s, A_WIDTH)
    ob = ob.astype(jnp.float32).reshape(b, s, B_WIDTH)
    mix = jnp.concatenate([oa, ob], axis=-1).astype(x.dtype) @ w_out
    x = layer_norm(DEEPNORM_ALPHA * x + mix, ln1_g, ln1_b)
    y = moe(x.reshape(b * s, d), w_router, b_router, w_gate_up, b_gate_up, w_down, b_down).reshape(b, s, d)
    return layer_norm(DEEPNORM_ALPHA * x + y, ln2_g, ln2_b)


def setup_inputs(seed: int = 0) -> dict:
    key = jax.random.key(seed)
    ks = jax.random.split(key, 24)
    nrm = jax.random.normal
    lb = min(BAND_PAST, PAST_LEN)
    beta = DEEPNORM_BETA
    col_scale = jnp.concatenate([
        jnp.ones((2 * A_QK_COLS,), jnp.float32), jnp.full((A_V_COLS,), beta, jnp.float32),
        jnp.ones((2 * B_COLS,), jnp.float32), jnp.full((B_COLS,), beta, jnp.float32)])
    return {
        "x_prompt": nrm(ks[0], (BATCH, SEQ, D_MODEL), jnp.float32),
        "x_sample": nrm(ks[1], (DEC_BATCH, DEC_SEQ, D_MODEL), jnp.float32),
        "cache_a_k": nrm(ks[2], (DEPTH, DEC_BATCH, PAST_LEN, 2 * H_A, D_HEAD_A), jnp.float32),
        "cache_a_v": nrm(ks[3], (DEPTH, DEC_BATCH, PAST_LEN, H_A, 2 * D_HEAD_A), jnp.float32) * beta,
        "cache_b_k": nrm(ks[4], (DEPTH, DEC_BATCH, lb, H_B, D_HEAD_B), jnp.float32),
        "cache_b_v": nrm(ks[5], (DEPTH, DEC_BATCH, lb, H_B, D_HEAD_B), jnp.float32) * beta,
        "w_in": nrm(ks[6], (DEPTH, D_MODEL, IN_COLS), jnp.float32) * (D_MODEL ** -0.5) * col_scale,
        "lambda_qk": nrm(ks[7], (DEPTH, 4, D_HEAD_A), jnp.float32) * 0.1,
        "subln_g": 1.0 + 0.05 * nrm(ks[8], (DEPTH, 2 * D_HEAD_A), jnp.float32),
        "rel_bias": nrm(ks[9], (DEPTH, H_B, 2 * REL_CLIP + 1), jnp.float32) * 0.5,
        "w_out": nrm(ks[10], (DEPTH, MIX_WIDTH, D_MODEL), jnp.float32) * (MIX_WIDTH ** -0.5) * beta,
        "ln1_g": 1.0 + 0.05 * nrm(ks[11], (DEPTH, D_MODEL), jnp.float32),
        "ln1_b": 0.02 * nrm(ks[12], (DEPTH, D_MODEL), jnp.float32),
        "w_router": nrm(ks[13], (DEPTH, D_MODEL, N_EXPERTS), jnp.float32) * (D_MODEL ** -0.5),
        "b_router": 0.01 * nrm(ks[14], (DEPTH, N_EXPERTS), jnp.float32),
        "w_gate_up": nrm(ks[15], (DEPTH, N_EXPERTS, D_MODEL, 2 * D_FF), jnp.float32) * (D_MODEL ** -0.5) * beta,
        "b_gate_up": 0.02 * nrm(ks[16], (DEPTH, N_EXPERTS, 2 * D_FF), jnp.float32),
        "w_down": nrm(ks[17], (DEPTH, N_EXPERTS, D_FF, D_MODEL), jnp.float32) * (D_FF ** -0.5) * beta,
        "b_down": 0.02 * nrm(ks[18], (DEPTH, N_EXPERTS, D_MODEL), jnp.float32),
        "ln2_g": 1.0 + 0.05 * nrm(ks[19], (DEPTH, D_MODEL), jnp.float32),
        "ln2_b": 0.02 * nrm(ks[20], (DEPTH, D_MODEL), jnp.float32),
    }


def reference(x_prompt, x_sample, cache_a_k, cache_a_v, cache_b_k, cache_b_v, w_in, lambda_qk, subln_g,
              rel_bias, w_out, ln1_g, ln1_b, w_router, b_router, w_gate_up, b_gate_up, w_down, b_down,
              ln2_g, ln2_b):
    b, s, _ = x_prompt.shape
    bd, sd, _ = x_sample.shape
    past = cache_a_k.shape[2]
    lb = cache_b_k.shape[2]
    nch = s // CHUNK
    band_len = (BAND_CHUNKS + 1) * CHUNK
    n_keep_p = min(BAND_PAST, s)
    n_keep_s = min(BAND_PAST, lb + sd)

    pos_p = jnp.arange(s, dtype=jnp.int32)
    pos_s = past + jnp.arange(sd, dtype=jnp.int32)
    kpos_a = jnp.arange(past + sd, dtype=jnp.int32)
    kpos_b = past - lb + jnp.arange(lb + sd, dtype=jnp.int32)
    band_off = jnp.arange(band_len, dtype=jnp.int32) - BAND_PAST
    rel_p = band_off[None, :] - jnp.arange(CHUNK, dtype=jnp.int32)[:, None]
    valid_p = ((jnp.arange(nch, dtype=jnp.int32)[:, None] * CHUNK + band_off[None, :]) >= 0)[:, None, :]
    rel_s = kpos_b[None, :] - pos_s[:, None]
    dch = (pos_s // CHUNK)[:, None] - (kpos_b // CHUNK)[None, :]
    valid_s = ((dch >= 0) & (dch <= BAND_CHUNKS) & (kpos_b[None, :] >= 0))[None]

    xp = x_prompt
    xs = x_sample
    ak_p, av_p, bk_p, bv_p, ak_s, av_s, bk_s, bv_s = [], [], [], [], [], [], [], []
    for l in range(DEPTH):
        lam_init = lambda_init_for(l)
        lq = lambda_qk[l].astype(jnp.float32)
        lam = jnp.exp(jnp.sum(lq[0] * lq[1])) - jnp.exp(jnp.sum(lq[2] * lq[3])) + lam_init
        shared = (lam_init, subln_g[l], w_out[l], ln1_g[l], ln1_b[l], w_router[l], b_router[l],
                  w_gate_up[l], b_gate_up[l], w_down[l], b_down[l], ln2_g[l], ln2_b[l])

        qa, ka, va, qb, kb, vb = project(xp, w_in[l])
        qa = rope(qa, pos_p)
        ka = rope(ka, pos_p)
        oa = diff_attn_blocked(qa, ka, va, pos_p, lam)
        ob = band_attn(qb.reshape(b, nch, CHUNK, H_B, D_HEAD_B), band_keys(kb), band_keys(vb),
                       rel_p, valid_p, rel_bias[l]).reshape(b, s, H_B, D_HEAD_B)
        xp = merge_and_channel_mix(xp, oa, ob, *shared)
        ak_p.append(ka)
        av_p.append(va)
        bk_p.append(kb[:, s - n_keep_p:])
        bv_p.append(vb[:, s - n_keep_p:])

        qa, ka, va, qb, kb, vb = project(xs, w_in[l])
        qa = rope(qa, pos_s)
        ka = rope(ka, pos_s)
        ka_all = jnp.concatenate([cache_a_k[l], ka], axis=1)
        va_all = jnp.concatenate([cache_a_v[l], va], axis=1)
        oa = diff_attn(qa, ka_all, va_all, pos_s, kpos_a, lam)
        kb_all = jnp.concatenate([cache_b_k[l], kb], axis=1)
        vb_all = jnp.concatenate([cache_b_v[l], vb], axis=1)
        ob = band_attn(qb[:, None], kb_all[:, None], vb_all[:, None], rel_s, valid_s,
                       rel_bias[l])[:, 0]
        xs = merge_and_channel_mix(xs, oa, ob, *shared)
        ak_s.append(ka)
        av_s.append(va)
        bk_s.append(kb_all[:, lb + sd - n_keep_s:])
        bv_s.append(vb_all[:, lb + sd - n_keep_s:])

    new_a_k_prompt = jnp.stack(ak_p, axis=0)
    new_a_v_prompt = jnp.stack(av_p, axis=0)
    new_b_k_prompt = jnp.stack(bk_p, axis=0)
    new_b_v_prompt = jnp.stack(bv_p, axis=0)
    new_a_k_sample = jnp.stack(ak_s, axis=0)
    new_a_v_sample = jnp.stack(av_s, axis=0)
    new_b_k_sample = jnp.stack(bk_s, axis=0)
    new_b_v_sample = jnp.stack(bv_s, axis=0)
    return (xp, xs, new_a_k_prompt, new_a_v_prompt, new_b_k_prompt, new_b_v_prompt,
            new_a_k_sample, new_a_v_sample, new_b_k_sample, new_b_v_sample)
```

```python
import functools
import math

import jax
import jax.numpy as jnp
from jax import lax
from jax.experimental import pallas as pl
from jax.experimental.pallas import tpu as pltpu

F32 = jnp.float32
BF16 = jnp.bfloat16
I32 = jnp.int32

D_MODEL = 1024
CHUNK = 64
D_HEAD_A = 64
H_A = 4
N_COMP = 2 * H_A
ROPE_DIM = D_HEAD_A // 4
ROPE_THETA = 500000.0
D_HEAD_B = 64
H_B = 8
BAND_CHUNKS = 8
BAND_PAST = BAND_CHUNKS * CHUNK
REL_CLIP = 128
GROUP = 512
N_EXPERTS = 32
TOP_K = 4
TOP_SHIFT = 2
SWIGLU_LIMIT = 7.0
SWIGLU_ALPHA = 1.702
LN_EPS = 1e-5
SUBLN_EPS = 1e-5
DEPTH = 1
DEEPNORM_ALPHA = (2.0 * DEPTH) ** 0.25
NEG_INF = -1e30
LANES = 128
PAIR = 2 * D_HEAD_A

VMEM_LIMIT = 56 * 1024 * 1024

PROJ_ROWS = 256
ATT_TQ = 512
ATT_TK = 512
BAND_TQ = 256
MERGE_ROWS = 256
MOE_ROWS = 256
FINAL_ROWS = 256


def _lambda_init(layer_idx):
    return 0.8 - 0.6 * math.exp(-0.3 * layer_idx)


def _cparams(sem):
    return pltpu.CompilerParams(dimension_semantics=sem, vmem_limit_bytes=VMEM_LIMIT)


def _half_mask(shape, upper):
    lane = lax.broadcasted_iota(I32, shape, len(shape) - 1)
    return (lane >= D_HEAD_A) if upper else (lane < D_HEAD_A)


def _proj_kernel(x_ref, w_ref, cos_ref, s1_ref, s2_ref,
                 qa_ref, ka32_ref, ka16_ref, va32_ref, va16_ref, vat_ref,
                 qb_ref, kb32_ref, kb16_ref, vb32_ref, vb16_ref):
    xb = x_ref[...].astype(BF16)

    def group(g):
        return jnp.dot(xb, w_ref[:, g * GROUP:(g + 1) * GROUP], preferred_element_type=F32)

    cos = cos_ref[...]
    s1 = s1_ref[...]
    s2 = s2_ref[...]

    def rope(h):
        parts = []
        for j in range(GROUP // LANES):
            hj = h[:, j * LANES:(j + 1) * LANES]
            nxt = pltpu.roll(hj, LANES - ROPE_DIM // 2, 1)
            prv = pltpu.roll(hj, ROPE_DIM // 2, 1)
            parts.append(hj * cos + nxt * s1 + prv * s2)
        return jnp.concatenate(parts, axis=1)

    qa = rope(group(0)) * (D_HEAD_A ** -0.5)
    qa_ref[...] = qa.astype(BF16)
    ka = rope(group(1))
    ka32_ref[...] = ka
    ka16_ref[...] = ka.astype(BF16)
    va = group(2)
    va32_ref[...] = va
    va16_ref[...] = va.astype(BF16)
    vat_ref[...] = va.T.astype(BF16)
    qb_ref[...] = (group(3) * (D_HEAD_B ** -0.5)).astype(BF16)
    kb = group(4)
    kb32_ref[...] = kb
    kb16_ref[...] = kb.astype(BF16)
    vb = group(5)
    vb32_ref[...] = vb
    vb16_ref[...] = vb.astype(BF16)


def _rope_tables(pos):
    half = ROPE_DIM // 2
    inv = ROPE_THETA ** (-jnp.arange(half, dtype=F32) * 2.0 / ROPE_DIM)
    ang = pos.astype(F32)[:, None] * inv[None, :]
    cos = jnp.cos(ang)
    sin = jnp.sin(ang)
    t = pos.shape[0]
    ones = jnp.ones((t, D_HEAD_A - ROPE_DIM), F32)
    zeros = jnp.zeros((t, D_HEAD_A - ROPE_DIM), F32)
    zh = jnp.zeros((t, half), F32)
    cos_h = jnp.concatenate([cos, cos, ones], axis=1)
    s1_h = jnp.concatenate([-sin, zh, zeros], axis=1)
    s2_h = jnp.concatenate([zh, sin, zeros], axis=1)
    tile = lambda a: jnp.concatenate([a, a], axis=1)
    return tile(cos_h), tile(s1_h), tile(s2_h)


def _project(x, w16, pos):
    t = x.shape[0]
    cos, s1, s2 = _rope_tables(pos)
    row = lambda i: (i, 0)
    blk = lambda cols: pl.BlockSpec((PROJ_ROWS, cols), row)
    f32o = jax.ShapeDtypeStruct((t, GROUP), F32)
    b16o = jax.ShapeDtypeStruct((t, GROUP), BF16)
    return pl.pallas_call(
        _proj_kernel,
        grid=(t // PROJ_ROWS,),
        in_specs=[blk(D_MODEL), pl.BlockSpec(w16.shape, lambda i: (0, 0)),
                  blk(LANES), blk(LANES), blk(LANES)],
        out_specs=[blk(GROUP), blk(GROUP), blk(GROUP), blk(GROUP), blk(GROUP),
                   pl.BlockSpec((GROUP, PROJ_ROWS), lambda i: (0, i)),
                   blk(GROUP), blk(GROUP), blk(GROUP), blk(GROUP), blk(GROUP)],
        out_shape=[b16o, f32o, b16o, f32o, b16o, jax.ShapeDtypeStruct((GROUP, t), BF16),
                   b16o, f32o, b16o, f32o, b16o],
        compiler_params=_cparams(("parallel",)),
        name="proj",
    )(x, w16, cos, s1, s2)


def _subln(o, g, lam_init, axis):
    o = o * lax.rsqrt(jnp.mean(jnp.square(o), axis=axis, keepdims=True) + SUBLN_EPS)
    return o * g * (1.0 - lam_init)


def _attn_a_prompt_kernel(qi_ref, kj_ref, lam_ref, q_ref, k_ref, vt_ref, g_ref, o_ref,
                          m_ref, l_ref, acc_ref, *, lam_init):
    s = pl.program_id(0)
    qi = qi_ref[s]
    kj = kj_ref[s]

    @pl.when(kj == 0)
    def _():
        m_ref[...] = jnp.full(m_ref.shape, -jnp.inf, F32)
        l_ref[...] = jnp.zeros(l_ref.shape, F32)
        acc_ref[...] = jnp.zeros(acc_ref.shape, F32)

    kchunk = lax.broadcasted_iota(I32, (ATT_TK, ATT_TQ), 0) // CHUNK + kj * (ATT_TK // CHUNK)
    qchunk = lax.broadcasted_iota(I32, (ATT_TK, ATT_TQ), 1) // CHUNK + qi * (ATT_TQ // CHUNK)
    allowed = kchunk <= qchunk

    for c in range(N_COMP):
        h = c // 2
        kp = k_ref[:, h * PAIR:(h + 1) * PAIR]
        qp = q_ref[:, h * PAIR:(h + 1) * PAIR]
        qz = jnp.where(_half_mask(qp.shape, c % 2 == 1), qp, jnp.zeros_like(qp))
        st = lax.dot_general(kp, qz, (((1,), (1,)), ((), ())), preferred_element_type=F32)
        st = jnp.where(allowed, st, NEG_INF)
        m_old = m_ref[c:c + 1, :]
        m_new = jnp.maximum(m_old, jnp.max(st, axis=0, keepdims=True))
        alpha = jnp.exp(m_old - m_new)
        p = jnp.exp(st - m_new)
        l_ref[c:c + 1, :] = alpha * l_ref[c:c + 1, :] + jnp.sum(p, axis=0, keepdims=True)
        pv = jnp.dot(vt_ref[h * PAIR:(h + 1) * PAIR, :], p.astype(BF16), preferred_element_type=F32)
        acc_ref[c] = alpha * acc_ref[c] + pv
        m_ref[c:c + 1, :] = m_new

    @pl.when(kj == qi)
    def _():
        lam = lam_ref[0]
        for h in range(H_A):
            o0 = acc_ref[2 * h] / l_ref[2 * h:2 * h + 1, :]
            o1 = acc_ref[2 * h + 1] / l_ref[2 * h + 1:2 * h + 2, :]
            ot = _subln(o0 - lam * o1, g_ref[...], lam_init, 0)
            o_ref[:, h * PAIR:(h + 1) * PAIR] = ot.T.astype(BF16)


def _attn_a_prompt(qa16, ka16, vat16, lam, subln_g, s_len, lam_init):
    nq = s_len // ATT_TQ
    qi_tab, kj_tab = [], []
    for i in range(nq):
        for j in range(i * ATT_TQ // ATT_TK + 1):
            qi_tab.append(i)
            kj_tab.append(j)
    qi_tab = jnp.asarray(qi_tab, I32)
    kj_tab = jnp.asarray(kj_tab, I32)
    grid_spec = pltpu.PrefetchScalarGridSpec(
        num_scalar_prefetch=2,
        grid=(int(qi_tab.shape[0]),),
        in_specs=[
            pl.BlockSpec(memory_space=pltpu.SMEM),
            pl.BlockSpec((ATT_TQ, GROUP), lambda s, qi, kj: (qi[s], 0)),
            pl.BlockSpec((ATT_TK, GROUP), lambda s, qi, kj: (kj[s], 0)),
            pl.BlockSpec((GROUP, ATT_TK), lambda s, qi, kj: (0, kj[s])),
            pl.BlockSpec((PAIR, 1), lambda s, qi, kj: (0, 0)),
        ],
        out_specs=pl.BlockSpec((ATT_TQ, GROUP), lambda s, qi, kj: (qi[s], 0)),
        scratch_shapes=[pltpu.VMEM((N_COMP, ATT_TQ), F32), pltpu.VMEM((N_COMP, ATT_TQ), F32),
                        pltpu.VMEM((N_COMP, PAIR, ATT_TQ), F32)],
    )
    return pl.pallas_call(
        functools.partial(_attn_a_prompt_kernel, lam_init=lam_init),
        grid_spec=grid_spec,
        out_shape=jax.ShapeDtypeStruct((s_len, GROUP), BF16),
        compiler_params=_cparams(("arbitrary",)),
        name="attn_a_prompt",
    )(qi_tab, kj_tab, lam, qa16, ka16, vat16, subln_g.reshape(PAIR, 1))


def _attend(qz, segs):
    scores = []
    for k, _, bias, mask in segs:
        sc = lax.dot_general(qz, k, (((1,), (1,)), ((), ())), preferred_element_type=F32)
        if bias is not None:
            sc = sc + bias
        if mask is not None:
            sc = jnp.where(mask, sc, NEG_INF)
        scores.append(sc)
    m = functools.reduce(jnp.maximum, [jnp.max(sc, axis=1, keepdims=True) for sc in scores])
    l = None
    o = None
    for sc, (_, v, _, _) in zip(scores, segs):
        p = jnp.exp(sc - m)
        ls = jnp.sum(p, axis=1, keepdims=True)
        os_ = jnp.dot(p.astype(BF16), v, preferred_element_type=F32)
        l = ls if l is None else l + ls
        o = os_ if o is None else o + os_
    return o / l


def _half_only(qp, upper):
    return jnp.where(_half_mask(qp.shape, upper), qp, jnp.zeros_like(qp))


def _attn_a_sample_kernel(lam_ref, q_ref, kn_ref, vn_ref, kc_ref, vc_ref, g_ref, o_ref, *, lam_init):
    lam = lam_ref[0]
    for h in range(H_A):
        cols = slice(h * PAIR, (h + 1) * PAIR)
        segs = [(kc_ref[:, cols].astype(BF16), vc_ref[:, cols].astype(BF16), None, None),
                (kn_ref[:, cols], vn_ref[:, cols], None, None)]
        qp = q_ref[:, cols]
        o0 = _attend(_half_only(qp, False), segs)
        o1 = _attend(_half_only(qp, True), segs)
        o = _subln(o0 - lam * o1, g_ref[...], lam_init, 1)
        o_ref[:, cols] = o.astype(BF16)


def _attn_a_sample(qa16, ka16, va16, cache_k, cache_v, lam, subln_g, row0, lam_init):
    nb, past, _ = cache_k.shape
    sd = CHUNK
    new = lambda b: (row0 // sd + b, 0)
    return pl.pallas_call(
        functools.partial(_attn_a_sample_kernel, lam_init=lam_init),
        grid=(nb,),
        in_specs=[
            pl.BlockSpec(memory_space=pltpu.SMEM),
            pl.BlockSpec((sd, GROUP), new), pl.BlockSpec((sd, GROUP), new), pl.BlockSpec((sd, GROUP), new),
            pl.BlockSpec((None, past, GROUP), lambda b: (b, 0, 0)),
            pl.BlockSpec((None, past, GROUP), lambda b: (b, 0, 0)),
            pl.BlockSpec((1, PAIR), lambda b: (0, 0)),
        ],
        out_specs=pl.BlockSpec((sd, GROUP), lambda b: (b, 0)),
        out_shape=jax.ShapeDtypeStruct((nb * sd, GROUP), BF16),
        compiler_params=_cparams(("parallel",)),
        name="attn_a_sample",
    )(lam, qa16, ka16, va16, cache_k, cache_v, subln_g.reshape(1, PAIR))


def _band_bias(rel_bias, n_q_chunks):
    nq = n_q_chunks * CHUNK
    nk = nq + BAND_PAST
    qpos = jnp.arange(nq, dtype=I32)[:, None]
    kpos = jnp.arange(nk, dtype=I32)[None, :] - BAND_PAST
    dch = qpos // CHUNK - jnp.floor_divide(kpos, CHUNK)
    ok = (dch >= 0) & (dch <= BAND_CHUNKS)
    rel = jnp.clip(kpos - qpos, -REL_CLIP, REL_CLIP) + REL_CLIP
    bias = rel_bias[:, rel].astype(F32)
    return jnp.where(ok[None], bias, NEG_INF)


def _band_heads(q_ref, seg_refs, bias_ref, o_ref):
    for j in range(H_B // 2):
        cols = slice(j * PAIR, (j + 1) * PAIR)
        qp = q_ref[:, cols]
        res = []
        for half in range(2):
            segs = []
            for k_ref, v_ref, col0, mask in seg_refs:
                n = k_ref.shape[0]
                segs.append((k_ref[:, cols].astype(BF16), v_ref[:, cols].astype(BF16),
                             bias_ref[2 * j + half, :, col0:col0 + n], mask))
            res.append(_attend(_half_only(qp, half == 1), segs))
        o = jnp.where(_half_mask(res[0].shape, False), res[0], res[1])
        o_ref[:, cols] = o.astype(BF16)


def _attn_b_prompt_kernel(q_ref, k0_ref, k1_ref, k2_ref, v0_ref, v1_ref, v2_ref, bias_ref, o_ref):
    i = pl.program_id(0)
    kpos = lax.broadcasted_iota(I32, (BAND_TQ, BAND_TQ), 1) + (i * BAND_TQ - BAND_PAST)
    seg_refs = [(k0_ref, v0_ref, 0, kpos >= 0),
                (k1_ref, v1_ref, BAND_TQ, kpos + BAND_TQ >= 0),
                (k2_ref, v2_ref, 2 * BAND_TQ, None)]
    _band_heads(q_ref, seg_refs, bias_ref, o_ref)


def _attn_b_prompt(qb16, kb16, vb16, bias, s_len):
    assert BAND_PAST == 2 * BAND_TQ
    cur = lambda i: (i, 0)
    p1 = lambda i: (jnp.maximum(i - 1, 0), 0)
    p2 = lambda i: (jnp.maximum(i - 2, 0), 0)
    blk = lambda f: pl.BlockSpec((BAND_TQ, GROUP), f)
    return pl.pallas_call(
        _attn_b_prompt_kernel,
        grid=(s_len // BAND_TQ,),
        in_specs=[blk(cur), blk(p2), blk(p1), blk(cur), blk(p2), blk(p1), blk(cur),
                  pl.BlockSpec(bias.shape, lambda i: (0, 0, 0))],
        out_specs=blk(cur),
        out_shape=jax.ShapeDtypeStruct((s_len, GROUP), BF16),
        compiler_params=_cparams(("parallel",)),
        name="attn_b_prompt",
    )(qb16, kb16, kb16, kb16, vb16, vb16, vb16, bias)


def _attn_b_sample_kernel(q_ref, kn_ref, vn_ref, kc_ref, vc_ref, bias_ref, o_ref):
    seg_refs = [(kc_ref, vc_ref, 0, None), (kn_ref, vn_ref, kc_ref.shape[0], None)]
    _band_heads(q_ref, seg_refs, bias_ref, o_ref)


def _attn_b_sample(qb16, kb16, vb16, cache_k, cache_v, bias, row0):
    nb, lb, _ = cache_k.shape
    sd = CHUNK
    new = lambda b: (row0 // sd + b, 0)
    return pl.pallas_call(
        _attn_b_sample_kernel,
        grid=(nb,),
        in_specs=[
            pl.BlockSpec((sd, GROUP), new), pl.BlockSpec((sd, GROUP), new), pl.BlockSpec((sd, GROUP), new),
            pl.BlockSpec((None, lb, GROUP), lambda b: (b, 0, 0)),
            pl.BlockSpec((None, lb, GROUP), lambda b: (b, 0, 0)),
            pl.BlockSpec(bias.shape, lambda b: (0, 0, 0)),
        ],
        out_specs=pl.BlockSpec((sd, GROUP), lambda b: (b, 0)),
        out_shape=jax.ShapeDtypeStruct((nb * sd, GROUP), BF16),
        compiler_params=_cparams(("parallel",)),
        name="attn_b_sample",
    )(qb16, kb16, vb16, cache_k, cache_v, bias)


def _layer_norm(z, g, b):
    mu = jnp.mean(z, axis=-1, keepdims=True)
    var = jnp.mean(jnp.square(z - mu), axis=-1, keepdims=True)
    return (z - mu) * lax.rsqrt(var + LN_EPS) * g + b


def _merge_kernel(oap_ref, obp_ref, oas_ref, obs_ref, x_ref, wo_ref, g_ref, b_ref, wr_ref, br_ref,
                  x1_ref, ti_ref, gate_ref, *, n_prompt_blocks):
    is_prompt = pl.program_id(0) < n_prompt_blocks
    oa = jnp.where(is_prompt, oap_ref[...], oas_ref[...])
    ob = jnp.where(is_prompt, obp_ref[...], obs_ref[...])
    mix = jnp.dot(oa, wo_ref[:GROUP, :], preferred_element_type=F32)
    mix = mix + jnp.dot(ob, wo_ref[GROUP:, :], preferred_element_type=F32)
    x1 = _layer_norm(DEEPNORM_ALPHA * x_ref[...] + mix, g_ref[...], b_ref[...])
    x1_ref[...] = x1
    logits = jnp.dot(x1.astype(BF16), wr_ref[...], preferred_element_type=F32) + br_ref[...]
    lane = lax.broadcasted_iota(I32, logits.shape, 1)
    logits = jnp.where(lane < N_EXPERTS, logits, -jnp.inf)
    vals, idxs = [], []
    for _ in range(TOP_K):
        mx = jnp.max(logits, axis=1, keepdims=True)
        ix = jnp.min(jnp.where(logits == mx, lane, LANES), axis=1, keepdims=True)
        vals.append(mx)
        idxs.append(ix)
        logits = jnp.where(lane == ix, -jnp.inf, logits)
    col = lax.broadcasted_iota(I32, ti_ref.shape, 1)
    top_v = jnp.broadcast_to(vals[-1], ti_ref.shape)
    top_i = jnp.broadcast_to(idxs[-1], ti_ref.shape)
    for k in range(TOP_K - 1):
        top_v = jnp.where(col == k, vals[k], top_v)
        top_i = jnp.where(col == k, idxs[k], top_i)
    ti_ref[...] = top_i
    e = jnp.exp(top_v - vals[0])
    gate_ref[...] = e / jnp.sum(e, axis=1, keepdims=True)


def _merge(oa_p, ob_p, oa_s, ob_s, x, wo16, ln_g, ln_b, wr16, br):
    t = x.shape[0]
    npb = oa_p.shape[0] // MERGE_ROWS
    row = lambda i: (i, 0)
    const = lambda i: (0, 0)
    blk = lambda cols: pl.BlockSpec((MERGE_ROWS, cols), row)
    prompt = pl.BlockSpec((MERGE_ROWS, GROUP), lambda i: (jnp.minimum(i, npb - 1), 0))
    sample = pl.BlockSpec((MERGE_ROWS, GROUP), lambda i: (jnp.maximum(i - npb, 0), 0))
    return pl.pallas_call(
        functools.partial(_merge_kernel, n_prompt_blocks=npb),
        grid=(t // MERGE_ROWS,),
        in_specs=[prompt, prompt, sample, sample, blk(D_MODEL),
                  pl.BlockSpec(wo16.shape, const), pl.BlockSpec((1, D_MODEL), const),
                  pl.BlockSpec((1, D_MODEL), const), pl.BlockSpec(wr16.shape, const),
                  pl.BlockSpec((1, LANES), const)],
        out_specs=[blk(D_MODEL), blk(TOP_K), blk(TOP_K)],
        out_shape=[jax.ShapeDtypeStruct((t, D_MODEL), F32), jax.ShapeDtypeStruct((t, TOP_K), I32),
                   jax.ShapeDtypeStruct((t, TOP_K), F32)],
        compiler_params=_cparams(("parallel",)),
        name="merge",
    )(oa_p, ob_p, oa_s, ob_s, x, wo16, ln_g.reshape(1, D_MODEL), ln_b.reshape(1, D_MODEL), wr16, br)


def _moe_kernel(blk_e_ref, nused_ref, idx_hbm, x_hbm, wg_ref, wu_ref, wd_ref, bg_ref, bu_ref, bd_ref,
                out_hbm, idx_smem, xbuf, ybuf, isem, gsem, ssem, *, n_tok):
    del blk_e_ref
    i = pl.program_id(0)
    nused = nused_ref[0]
    n_assign = n_tok * TOP_K

    def idx_copy(b):
        return pltpu.make_async_copy(idx_hbm.at[b], idx_smem.at[b % 3], isem.at[b % 3])

    def gather_copy(b, r, tok):
        return pltpu.make_async_copy(x_hbm.at[pl.ds(tok, 1)], xbuf.at[b % 2, pl.ds(r, 1)], gsem.at[b % 2])

    def scatter_copy(b, r, dst):
        return pltpu.make_async_copy(ybuf.at[b % 2, pl.ds(r, 1)], out_hbm.at[pl.ds(dst, 1)], ssem.at[b % 2])

    def start_gather(b):
        def body(r, carry):
            a = idx_smem[b % 3, r]
            tok = jnp.where(a >= 0, lax.shift_right_logical(a, TOP_SHIFT), 0)
            gather_copy(b, r, tok).start()
            return carry
        lax.fori_loop(0, MOE_ROWS, body, 0, unroll=8)

    def wait_gather(b):
        def body(r, carry):
            gather_copy(b, r, 0).wait()
            return carry
        lax.fori_loop(0, MOE_ROWS, body, 0, unroll=8)

    def start_scatter(b):
        def body(r, carry):
            a = idx_smem[b % 3, r]
            real = (a & (TOP_K - 1)) * n_tok + lax.shift_right_logical(a, TOP_SHIFT)
            dump = n_assign + (b % 2) * MOE_ROWS + r
            scatter_copy(b, r, jnp.where(a >= 0, real, dump)).start()
            return carry
        lax.fori_loop(0, MOE_ROWS, body, 0, unroll=8)

    def wait_scatter(b):
        def body(r, carry):
            scatter_copy(b, r, 0).wait()
            return carry
        lax.fori_loop(0, MOE_ROWS, body, 0, unroll=8)

    @pl.when(i < nused)
    def _():
        @pl.when(i == 0)
        def _():
            ybuf[...] = jnp.zeros(ybuf.shape, F32)
            clear = [pltpu.make_async_copy(ybuf.at[p], out_hbm.at[pl.ds(n_assign + p * MOE_ROWS, MOE_ROWS)],
                                           ssem.at[p]) for p in range(2)]
            for c in clear:
                c.start()
            for c in clear:
                c.wait()
            idx_copy(0).start()
            idx_copy(0).wait()
            start_gather(0)

            @pl.when(nused > 1)
            def _():
                idx_copy(1).start()

        @pl.when(i + 1 < nused)
        def _():
            idx_copy(i + 1).wait()
            start_gather(i + 1)

        @pl.when(i + 2 < nused)
        def _():
            idx_copy(i + 2).start()

        wait_gather(i)
        x = xbuf[i % 2].astype(BF16)
        hg = jnp.dot(x, wg_ref[...], preferred_element_type=F32) + bg_ref[...]
        hu = jnp.dot(x, wu_ref[...], preferred_element_type=F32) + bu_ref[...]
        gate = jnp.minimum(hg, SWIGLU_LIMIT)
        up = jnp.clip(hu, -SWIGLU_LIMIT, SWIGLU_LIMIT)
        glu = gate * (1.0 / (1.0 + jnp.exp(-(gate * SWIGLU_ALPHA))))
        act = ((up + 1.0) * glu).astype(BF16)
        y = jnp.dot(act, wd_ref[...], preferred_element_type=F32) + bd_ref[...]

        @pl.when(i >= 2)
        def _():
            wait_scatter(i - 2)

        ybuf[i % 2] = y
        start_scatter(i)

        @pl.when(i == nused - 1)
        def _():
            wait_scatter(i)

            @pl.when(i >= 1)
            def _():
                wait_scatter(i - 1)


def _moe(x1, idx, blk_e, nused, wg16, wu16, wd16, bg, bu, bd):
    t = x1.shape[0]
    nblk = idx.shape[0]
    wmap = lambda i, be, nu: (be[i], 0, 0)
    grid_spec = pltpu.PrefetchScalarGridSpec(
        num_scalar_prefetch=2,
        grid=(nblk,),
        in_specs=[
            pl.BlockSpec(memory_space=pl.ANY),
            pl.BlockSpec(memory_space=pl.ANY),
            pl.BlockSpec((None, D_MODEL, D_MODEL), wmap),
            pl.BlockSpec((None, D_MODEL, D_MODEL), wmap),
            pl.BlockSpec((None, D_MODEL, D_MODEL), wmap),
            pl.BlockSpec((None, 1, D_MODEL), wmap),
            pl.BlockSpec((None, 1, D_MODEL), wmap),
            pl.BlockSpec((None, 1, D_MODEL), wmap),
        ],
        out_specs=pl.BlockSpec(memory_space=pl.ANY),
        scratch_shapes=[
            pltpu.SMEM((3, MOE_ROWS), I32),
            pltpu.VMEM((2, MOE_ROWS, D_MODEL), F32),
            pltpu.VMEM((2, MOE_ROWS, D_MODEL), F32),
            pltpu.SemaphoreType.DMA((3,)),
            pltpu.SemaphoreType.DMA((2,)),
            pltpu.SemaphoreType.DMA((2,)),
        ],
    )
    return pl.pallas_call(
        functools.partial(_moe_kernel, n_tok=t),
        grid_spec=grid_spec,
        out_shape=jax.ShapeDtypeStruct((t * TOP_K + 2 * MOE_ROWS, D_MODEL), F32),
        compiler_params=_cparams(("arbitrary",)),
        name="moe",
    )(blk_e, nused, idx, x1, wg16, wu16, wd16, bg, bu, bd)


def _route(top_i):
    t = top_i.shape[0]
    n = t * TOP_K
    nblk = n // MOE_ROWS + N_EXPERTS
    flat_e = top_i.reshape(-1)
    order = jnp.argsort(flat_e, stable=True).astype(I32)
    counts = jnp.sum((flat_e[:, None] == jnp.arange(N_EXPERTS, dtype=I32)[None, :]).astype(I32), axis=0)
    padded = ((counts + MOE_ROWS - 1) // MOE_ROWS) * MOE_ROWS
    pad_end = jnp.cumsum(padded)
    pad_start = pad_end - padded
    start = jnp.cumsum(counts) - counts
    nused = (pad_end[-1] // MOE_ROWS).astype(I32).reshape(1)
    blk_first = jnp.arange(nblk, dtype=I32) * MOE_ROWS
    blk_e = jnp.minimum(jnp.searchsorted(pad_end, blk_first, side="right"), N_EXPERTS - 1).astype(I32)
    r = jnp.arange(nblk * MOE_ROWS, dtype=I32)
    e_r = jnp.repeat(blk_e, MOE_ROWS)
    j = r - pad_start[e_r]
    valid = (j < counts[e_r]) & (r < pad_end[-1])
    a = order[jnp.clip(start[e_r] + j, 0, n - 1)]
    idx = jnp.where(valid, a, -1 - (r % MOE_ROWS)).astype(I32).reshape(nblk, MOE_ROWS)
    return idx, blk_e, nused


def _final_kernel(x1_ref, gate_ref, y0_ref, y1_ref, y2_ref, y3_ref, g_ref, b_ref, o_ref):
    gates = gate_ref[...]
    y = gates[:, 0:1] * y0_ref[...]
    for k, ref in enumerate((y1_ref, y2_ref, y3_ref), start=1):
        y = y + gates[:, k:k + 1] * ref[...]
    o_ref[...] = _layer_norm(DEEPNORM_ALPHA * x1_ref[...] + y, g_ref[...], b_ref[...])


def _final(x1, gates, planes, ln_g, ln_b):
    t = x1.shape[0]
    nb = t // FINAL_ROWS
    row = lambda i: (i, 0)
    const = lambda i: (0, 0)
    plane = lambda k: pl.BlockSpec((FINAL_ROWS, D_MODEL), lambda i: (k * nb + i, 0))
    return pl.pallas_call(
        _final_kernel,
        grid=(nb,),
        in_specs=[pl.BlockSpec((FINAL_ROWS, D_MODEL), row), pl.BlockSpec((FINAL_ROWS, TOP_K), row),
                  plane(0), plane(1), plane(2), plane(3),
                  pl.BlockSpec((1, D_MODEL), const), pl.BlockSpec((1, D_MODEL), const)],
        out_specs=pl.BlockSpec((FINAL_ROWS, D_MODEL), row),
        out_shape=jax.ShapeDtypeStruct((t, D_MODEL), F32),
        compiler_params=_cparams(("parallel",)),
        name="final",
    )(x1, gates, planes, planes, planes, planes, ln_g.reshape(1, D_MODEL), ln_b.reshape(1, D_MODEL))


def kernel(x_prompt, x_sample, cache_a_k, cache_a_v, cache_b_k, cache_b_v, w_in, lambda_qk, subln_g,
           rel_bias, w_out, ln1_g, ln1_b, w_router, b_router, w_gate_up, b_gate_up, w_down, b_down,
           ln2_g, ln2_b):
    b, s, d = x_prompt.shape
    bd, sd, _ = x_sample.shape
    depth, _, past, _, _ = cache_a_k.shape
    lb = cache_b_k.shape[2]
    assert depth == 1 and b == 1 and d == D_MODEL
    assert sd == CHUNK and past % CHUNK == 0 and lb == BAND_PAST and s >= BAND_PAST
    assert s % ATT_TQ == 0 and s % BAND_TQ == 0
    ts = bd * sd
    t = s + ts
    lam_init = _lambda_init(0)

    x = jnp.concatenate([x_prompt.reshape(s, d), x_sample.reshape(ts, d)], axis=0)
    pos = jnp.concatenate([jnp.arange(s, dtype=I32),
                           jnp.tile(past + jnp.arange(sd, dtype=I32), bd)])
    lq = lambda_qk[0].astype(F32)
    lam = (jnp.exp(jnp.sum(lq[0] * lq[1])) - jnp.exp(jnp.sum(lq[2] * lq[3])) + lam_init).reshape(1)

    (qa16, ka32, ka16, va32, va16, vat16,
     qb16, kb32, kb16, vb32, vb16) = _project(x, w_in[0].astype(BF16), pos)

    oa_p = _attn_a_prompt(qa16, ka16, vat16, lam, subln_g[0], s, lam_init)
    oa_s = _attn_a_sample(qa16, ka16, va16, cache_a_k[0].reshape(bd, past, GROUP),
                          cache_a_v[0].reshape(bd, past, GROUP), lam, subln_g[0], s, lam_init)

    ob_p = _attn_b_prompt(qb16, kb16, vb16, _band_bias(rel_bias[0], BAND_TQ // CHUNK), s)
    ob_s = _attn_b_sample(qb16, kb16, vb16, cache_b_k[0].reshape(bd, lb, GROUP),
                          cache_b_v[0].reshape(bd, lb, GROUP), _band_bias(rel_bias[0], 1), s)

    wr16 = jnp.pad(w_router[0], ((0, 0), (0, LANES - N_EXPERTS))).astype(BF16)
    br = jnp.pad(b_router[0], (0, LANES - N_EXPERTS)).reshape(1, LANES)
    x1, top_i, gates = _merge(oa_p, ob_p, oa_s, ob_s, x, w_out[0].astype(BF16), ln1_g[0], ln1_b[0],
                              wr16, br)

    idx, blk_e, nused = _route(top_i)
    wgu = w_gate_up[0].reshape(N_EXPERTS, D_MODEL, D_MODEL, 2)
    bgu = b_gate_up[0].reshape(N_EXPERTS, 1, D_MODEL, 2)
    planes = _moe(x1, idx, blk_e, nused, wgu[..., 0].astype(BF16), wgu[..., 1].astype(BF16),
                  w_down[0].astype(BF16), bgu[..., 0], bgu[..., 1], b_down[0].reshape(N_EXPERTS, 1, D_MODEL))
    y = _final(x1, gates, planes, ln2_g[0], ln2_b[0])

    heads_a = lambda a, n: a.reshape(1, n, -1, N_COMP, D_HEAD_A)
    vals_a = lambda a, n: a.reshape(1, n, -1, H_A, 2 * D_HEAD_A)
    heads_b = lambda a, n: a.reshape(1, n, -1, H_B, D_HEAD_B)
    keep_s = min(BAND_PAST, lb + sd)
    kb_s = jnp.concatenate([cache_b_k[0], kb32[s:].reshape(bd, sd, H_B, D_HEAD_B)], axis=1)[:, lb + sd - keep_s:]
    vb_s = jnp.concatenate([cache_b_v[0], vb32[s:].reshape(bd, sd, H_B, D_HEAD_B)], axis=1)[:, lb + sd - keep_s:]
    return (y[:s].reshape(b, s, d), y[s:].reshape(bd, sd, d),
            heads_a(ka32[:s], b), vals_a(va32[:s], b),
            heads_b(kb32[s - BAND_PAST:s], b), heads_b(vb32[s - BAND_PAST:s], b),
            heads_a(ka32[s:], bd), vals_a(va32[s:], bd),
            kb_s[None], vb_s[None])
```

```python
import functools
import math

import jax
import jax.numpy as jnp
from jax import lax
from jax.experimental import pallas as pl
from jax.experimental.pallas import tpu as pltpu

F32 = jnp.float32
BF16 = jnp.bfloat16
I32 = jnp.int32

D_MODEL = 1024
CHUNK = 64
D_HEAD_A = 64
H_A = 4
N_COMP = 2 * H_A
ROPE_DIM = D_HEAD_A // 4
ROPE_THETA = 500000.0
D_HEAD_B = 64
H_B = 8
BAND_CHUNKS = 8
BAND_PAST = BAND_CHUNKS * CHUNK
REL_CLIP = 128
GROUP = 512
N_EXPERTS = 32
TOP_K = 4
TOP_SHIFT = 2
SWIGLU_LIMIT = 7.0
SWIGLU_ALPHA = 1.702
LN_EPS = 1e-5
SUBLN_EPS = 1e-5
DEPTH = 1
DEEPNORM_ALPHA = (2.0 * DEPTH) ** 0.25
NEG_INF = -1e30
LANES = 128
PAIR = 2 * D_HEAD_A
VT_ROWS = PAIR + 16
LOG2_E = math.log2(math.e)

VMEM_LIMIT = 56 * 1024 * 1024

PROJ_ROWS = 256
ATT_TQ = 512
ATT_TK = 512
BAND_TQ = 256
MERGE_ROWS = 256
MOE_ROWS = 256
FINAL_ROWS = 256


def _lambda_init(layer_idx):
    return 0.8 - 0.6 * math.exp(-0.3 * layer_idx)


def _cparams(sem):
    return pltpu.CompilerParams(dimension_semantics=sem, vmem_limit_bytes=VMEM_LIMIT)


def _half_mask(shape, upper):
    lane = lax.broadcasted_iota(I32, shape, len(shape) - 1)
    return (lane >= D_HEAD_A) if upper else (lane < D_HEAD_A)


def _proj_kernel(x_ref, w_ref, cos_ref, s1_ref, s2_ref,
                 qa_ref, ka32_ref, ka16_ref, va32_ref, va16_ref, vat_ref,
                 qb_ref, kb32_ref, kb16_ref, vb32_ref, vb16_ref):
    xb = x_ref[...].astype(BF16)

    def group(g):
        return jnp.dot(xb, w_ref[:, g * GROUP:(g + 1) * GROUP], preferred_element_type=F32)

    cos = cos_ref[...]
    s1 = s1_ref[...]
    s2 = s2_ref[...]

    def rope(h):
        parts = []
        for j in range(GROUP // LANES):
            hj = h[:, j * LANES:(j + 1) * LANES]
            nxt = pltpu.roll(hj, LANES - ROPE_DIM // 2, 1)
            prv = pltpu.roll(hj, ROPE_DIM // 2, 1)
            parts.append(hj * cos + nxt * s1 + prv * s2)
        return jnp.concatenate(parts, axis=1)

    qa = rope(group(0)) * (D_HEAD_A ** -0.5 * LOG2_E)
    qa_ref[...] = qa.astype(BF16)
    ka = rope(group(1))
    ka32_ref[...] = ka
    ka16_ref[...] = ka.astype(BF16)
    va = group(2)
    va32_ref[...] = va
    va16_ref[...] = va.astype(BF16)
    vt = va.T
    rows = vt.shape[1]
    tail = jnp.concatenate([jnp.ones((1, rows), F32), jnp.zeros((VT_ROWS - PAIR - 1, rows), F32)], axis=0)
    parts = []
    for h in range(H_A):
        parts += [vt[h * PAIR:(h + 1) * PAIR], tail]
    vat_ref[...] = jnp.concatenate(parts, axis=0).astype(BF16)
    qb_ref[...] = (group(3) * (D_HEAD_B ** -0.5)).astype(BF16)
    kb = group(4)
    kb32_ref[...] = kb
    kb16_ref[...] = kb.astype(BF16)
    vb = group(5)
    vb32_ref[...] = vb
    vb16_ref[...] = vb.astype(BF16)


def _rope_tables(pos):
    half = ROPE_DIM // 2
    inv = ROPE_THETA ** (-jnp.arange(half, dtype=F32) * 2.0 / ROPE_DIM)
    ang = pos.astype(F32)[:, None] * inv[None, :]
    cos = jnp.cos(ang)
    sin = jnp.sin(ang)
    t = pos.shape[0]
    ones = jnp.ones((t, D_HEAD_A - ROPE_DIM), F32)
    zeros = jnp.zeros((t, D_HEAD_A - ROPE_DIM), F32)
    zh = jnp.zeros((t, half), F32)
    cos_h = jnp.concatenate([cos, cos, ones], axis=1)
    s1_h = jnp.concatenate([-sin, zh, zeros], axis=1)
    s2_h = jnp.concatenate([zh, sin, zeros], axis=1)
    tile = lambda a: jnp.concatenate([a, a], axis=1)
    return tile(cos_h), tile(s1_h), tile(s2_h)


def _project(x, w16, pos):
    t = x.shape[0]
    cos, s1, s2 = _rope_tables(pos)
    row = lambda i: (i, 0)
    blk = lambda cols: pl.BlockSpec((PROJ_ROWS, cols), row)
    f32o = jax.ShapeDtypeStruct((t, GROUP), F32)
    b16o = jax.ShapeDtypeStruct((t, GROUP), BF16)
    return pl.pallas_call(
        _proj_kernel,
        grid=(t // PROJ_ROWS,),
        in_specs=[blk(D_MODEL), pl.BlockSpec(w16.shape, lambda i: (0, 0)),
                  blk(LANES), blk(LANES), blk(LANES)],
        out_specs=[blk(GROUP), blk(GROUP), blk(GROUP), blk(GROUP), blk(GROUP),
                   pl.BlockSpec((H_A * VT_ROWS, PROJ_ROWS), lambda i: (0, i)),
                   blk(GROUP), blk(GROUP), blk(GROUP), blk(GROUP), blk(GROUP)],
        out_shape=[b16o, f32o, b16o, f32o, b16o, jax.ShapeDtypeStruct((H_A * VT_ROWS, t), BF16),
                   b16o, f32o, b16o, f32o, b16o],
        compiler_params=_cparams(("parallel",)),
        name="proj",
    )(x, w16, cos, s1, s2)


def _subln(o, g, lam_init, axis):
    o = o * lax.rsqrt(jnp.mean(jnp.square(o), axis=axis, keepdims=True) + SUBLN_EPS)
    return o * g * (1.0 - lam_init)


def _half_only(qp, upper):
    return jnp.where(_half_mask(qp.shape, upper), qp, jnp.zeros_like(qp))


def _attn_a_prompt_kernel(qi_ref, kj_ref, lam_ref, q_ref, k_ref, vt_ref, dmask_ref, g_ref, o_ref,
                          m_ref, acc_ref, *, lam_init):
    s = pl.program_id(0)
    qi = qi_ref[s]
    kj = kj_ref[s]

    @pl.when(kj == 0)
    def _():
        m_ref[...] = jnp.full(m_ref.shape, -jnp.inf, F32)
        acc_ref[...] = jnp.zeros(acc_ref.shape, F32)

    def step(diagonal):
        for c in range(N_COMP):
            h = c // 2
            kp = k_ref[:, h * PAIR:(h + 1) * PAIR]
            qz = _half_only(q_ref[:, h * PAIR:(h + 1) * PAIR], c % 2 == 1)
            st = lax.dot_general(kp, qz, (((1,), (1,)), ((), ())), preferred_element_type=F32)
            if diagonal:
                st = st + dmask_ref[...]
            m_old = m_ref[c:c + 1, :]
            m_new = jnp.maximum(m_old, jnp.max(st, axis=0, keepdims=True))
            alpha = jnp.exp2(m_old - m_new)
            p = jnp.exp2(st - m_new).astype(BF16)
            pv = jnp.dot(vt_ref[h * VT_ROWS:(h + 1) * VT_ROWS, :], p, preferred_element_type=F32)
            acc_ref[c] = alpha * acc_ref[c] + pv
            m_ref[c:c + 1, :] = m_new

    @pl.when(kj < qi)
    def _():
        step(False)

    @pl.when(kj == qi)
    def _():
        step(True)
        lam = lam_ref[0]
        for h in range(H_A):
            a0 = acc_ref[2 * h]
            a1 = acc_ref[2 * h + 1]
            o0 = a0[:PAIR] / a0[PAIR:PAIR + 1]
            o1 = a1[:PAIR] / a1[PAIR:PAIR + 1]
            ot = _subln(o0 - lam * o1, g_ref[...], lam_init, 0)
            o_ref[:, h * PAIR:(h + 1) * PAIR] = ot.T.astype(BF16)


def _attn_a_prompt(qa16, ka16, vat16, lam, subln_g, s_len, lam_init):
    assert ATT_TQ == ATT_TK
    kchunk = jnp.arange(ATT_TK, dtype=I32)[:, None] // CHUNK
    qchunk = jnp.arange(ATT_TQ, dtype=I32)[None, :] // CHUNK
    dmask = jnp.where(kchunk <= qchunk, 0.0, NEG_INF).astype(F32)
    nq = s_len // ATT_TQ
    qi_tab, kj_tab = [], []
    for i in range(nq):
        for j in range(i * ATT_TQ // ATT_TK + 1):
            qi_tab.append(i)
            kj_tab.append(j)
    qi_tab = jnp.asarray(qi_tab, I32)
    kj_tab = jnp.asarray(kj_tab, I32)
    grid_spec = pltpu.PrefetchScalarGridSpec(
        num_scalar_prefetch=2,
        grid=(int(qi_tab.shape[0]),),
        in_specs=[
            pl.BlockSpec(memory_space=pltpu.SMEM),
            pl.BlockSpec((ATT_TQ, GROUP), lambda s, qi, kj: (qi[s], 0)),
            pl.BlockSpec((ATT_TK, GROUP), lambda s, qi, kj: (kj[s], 0)),
            pl.BlockSpec((H_A * VT_ROWS, ATT_TK), lambda s, qi, kj: (0, kj[s])),
            pl.BlockSpec((ATT_TK, ATT_TQ), lambda s, qi, kj: (0, 0)),
            pl.BlockSpec((PAIR, 1), lambda s, qi, kj: (0, 0)),
        ],
        out_specs=pl.BlockSpec((ATT_TQ, GROUP), lambda s, qi, kj: (qi[s], 0)),
        scratch_shapes=[pltpu.VMEM((N_COMP, ATT_TQ), F32),
                        pltpu.VMEM((N_COMP, VT_ROWS, ATT_TQ), F32)],
    )
    return pl.pallas_call(
        functools.partial(_attn_a_prompt_kernel, lam_init=lam_init),
        grid_spec=grid_spec,
        out_shape=jax.ShapeDtypeStruct((s_len, GROUP), BF16),
        compiler_params=_cparams(("arbitrary",)),
        name="attn_a_prompt",
    )(qi_tab, kj_tab, lam, qa16, ka16, vat16, dmask, subln_g.reshape(PAIR, 1))


def _attend(qz, segs, exp=jnp.exp):
    scores = []
    for k, _, bias, mask in segs:
        sc = lax.dot_general(qz, k, (((1,), (1,)), ((), ())), preferred_element_type=F32)
        if bias is not None:
            sc = sc + bias
        if mask is not None:
            sc = jnp.where(mask, sc, NEG_INF)
        scores.append(sc)
    m = functools.reduce(jnp.maximum, [jnp.max(sc, axis=1, keepdims=True) for sc in scores])
    l = None
    o = None
    for sc, (_, v, _, _) in zip(scores, segs):
        p = exp(sc - m)
        ls = jnp.sum(p, axis=1, keepdims=True)
        os_ = jnp.dot(p.astype(BF16), v, preferred_element_type=F32)
        l = ls if l is None else l + ls
        o = os_ if o is None else o + os_
    return o / l


def _attn_a_sample_kernel(lam_ref, q_ref, kn_ref, vn_ref, kc_ref, vc_ref, g_ref, o_ref, *, lam_init):
    lam = lam_ref[0]
    for h in range(H_A):
        cols = slice(h * PAIR, (h + 1) * PAIR)
        segs = [(kc_ref[:, cols].astype(BF16), vc_ref[:, cols].astype(BF16), None, None),
                (kn_ref[:, cols], vn_ref[:, cols], None, None)]
        qp = q_ref[:, cols]
        o0 = _attend(_half_only(qp, False), segs, jnp.exp2)
        o1 = _attend(_half_only(qp, True), segs, jnp.exp2)
        o = _subln(o0 - lam * o1, g_ref[...], lam_init, 1)
        o_ref[:, cols] = o.astype(BF16)


def _attn_a_sample(qa16, ka16, va16, cache_k, cache_v, lam, subln_g, row0, lam_init):
    nb, past, _ = cache_k.shape
    sd = CHUNK
    new = lambda b: (row0 // sd + b, 0)
    return pl.pallas_call(
        functools.partial(_attn_a_sample_kernel, lam_init=lam_init),
        grid=(nb,),
        in_specs=[
            pl.BlockSpec(memory_space=pltpu.SMEM),
            pl.BlockSpec((sd, GROUP), new), pl.BlockSpec((sd, GROUP), new), pl.BlockSpec((sd, GROUP), new),
            pl.BlockSpec((None, past, GROUP), lambda b: (b, 0, 0)),
            pl.BlockSpec((None, past, GROUP), lambda b: (b, 0, 0)),
            pl.BlockSpec((1, PAIR), lambda b: (0, 0)),
        ],
        out_specs=pl.BlockSpec((sd, GROUP), lambda b: (b, 0)),
        out_shape=jax.ShapeDtypeStruct((nb * sd, GROUP), BF16),
        compiler_params=_cparams(("parallel",)),
        name="attn_a_sample",
    )(lam, qa16, ka16, va16, cache_k, cache_v, subln_g.reshape(1, PAIR))


def _band_bias(rel_bias, n_q_chunks):
    nq = n_q_chunks * CHUNK
    nk = nq + BAND_PAST
    heads = rel_bias.shape[0]
    n_lo = BAND_PAST - REL_CLIP
    period = max(nq + nk, n_lo + (2 * REL_CLIP + 1) + nq - 1)
    n_hi = period - (nq - 1) - n_lo - (2 * REL_CLIP + 1)
    assert n_lo >= 0
    lo = jnp.broadcast_to(rel_bias[:, :1], (heads, n_lo))
    hi = jnp.broadcast_to(rel_bias[:, -1:], (heads, n_hi))
    wrap = jnp.broadcast_to(rel_bias[:, :1], (heads, nq - 1))
    e = jnp.concatenate([lo, rel_bias, hi, wrap], axis=1).astype(F32)
    flat = jnp.broadcast_to(e[:, None, :], (heads, nq, period)).reshape(heads, nq * period)
    bias = flat[:, :nq * (period - 1)].reshape(heads, nq, period - 1)[:, :, :nk]
    qpos = jnp.arange(nq, dtype=I32)[:, None]
    kpos = jnp.arange(nk, dtype=I32)[None, :] - BAND_PAST
    dch = qpos // CHUNK - jnp.floor_divide(kpos, CHUNK)
    ok = (dch >= 0) & (dch <= BAND_CHUNKS)
    return jnp.where(ok[None], bias, NEG_INF)


def _band_heads(q_ref, seg_refs, bias_ref, o_ref):
    for j in range(H_B // 2):
        cols = slice(j * PAIR, (j + 1) * PAIR)
        qp = q_ref[:, cols]
        res = []
        for half in range(2):
            segs = []
            for k_ref, v_ref, col0, mask in seg_refs:
                n = k_ref.shape[0]
                segs.append((k_ref[:, cols].astype(BF16), v_ref[:, cols].astype(BF16),
                             bias_ref[2 * j + half, :, col0:col0 + n], mask))
            res.append(_attend(_half_only(qp, half == 1), segs))
        o = jnp.where(_half_mask(res[0].shape, False), res[0], res[1])
        o_ref[:, cols] = o.astype(BF16)


def _attn_b_prompt_kernel(q_ref, k0_ref, k1_ref, k2_ref, v0_ref, v1_ref, v2_ref, bias_ref, o_ref):
    i = pl.program_id(0)
    kpos = lax.broadcasted_iota(I32, (BAND_TQ, BAND_TQ), 1) + (i * BAND_TQ - BAND_PAST)
    seg_refs = [(k0_ref, v0_ref, 0, kpos >= 0),
                (k1_ref, v1_ref, BAND_TQ, kpos + BAND_TQ >= 0),
                (k2_ref, v2_ref, 2 * BAND_TQ, None)]
    _band_heads(q_ref, seg_refs, bias_ref, o_ref)


def _attn_b_prompt(qb16, kb16, vb16, bias, s_len):
    assert BAND_PAST == 2 * BAND_TQ
    cur = lambda i: (i, 0)
    p1 = lambda i: (jnp.maximum(i - 1, 0), 0)
    p2 = lambda i: (jnp.maximum(i - 2, 0), 0)
    blk = lambda f: pl.BlockSpec((BAND_TQ, GROUP), f)
    return pl.pallas_call(
        _attn_b_prompt_kernel,
        grid=(s_len // BAND_TQ,),
        in_specs=[blk(cur), blk(p2), blk(p1), blk(cur), blk(p2), blk(p1), blk(cur),
                  pl.BlockSpec(bias.shape, lambda i: (0, 0, 0))],
        out_specs=blk(cur),
        out_shape=jax.ShapeDtypeStruct((s_len, GROUP), BF16),
        compiler_params=_cparams(("parallel",)),
        name="attn_b_prompt",
    )(qb16, kb16, kb16, kb16, vb16, vb16, vb16, bias)


def _attn_b_sample_kernel(q_ref, kn_ref, vn_ref, kc_ref, vc_ref, bias_ref, o_ref):
    seg_refs = [(kc_ref, vc_ref, 0, None), (kn_ref, vn_ref, kc_ref.shape[0], None)]
    _band_heads(q_ref, seg_refs, bias_ref, o_ref)


def _attn_b_sample(qb16, kb16, vb16, cache_k, cache_v, bias, row0):
    nb, lb, _ = cache_k.shape
    sd = CHUNK
    new = lambda b: (row0 // sd + b, 0)
    return pl.pallas_call(
        _attn_b_sample_kernel,
        grid=(nb,),
        in_specs=[
            pl.BlockSpec((sd, GROUP), new), pl.BlockSpec((sd, GROUP), new), pl.BlockSpec((sd, GROUP), new),
            pl.BlockSpec((None, lb, GROUP), lambda b: (b, 0, 0)),
            pl.BlockSpec((None, lb, GROUP), lambda b: (b, 0, 0)),
            pl.BlockSpec(bias.shape, lambda b: (0, 0, 0)),
        ],
        out_specs=pl.BlockSpec((sd, GROUP), lambda b: (b, 0)),
        out_shape=jax.ShapeDtypeStruct((nb * sd, GROUP), BF16),
        compiler_params=_cparams(("parallel",)),
        name="attn_b_sample",
    )(qb16, kb16, vb16, cache_k, cache_v, bias)


def _layer_norm(z, g, b):
    mu = jnp.mean(z, axis=-1, keepdims=True)
    var = jnp.mean(jnp.square(z - mu), axis=-1, keepdims=True)
    return (z - mu) * lax.rsqrt(var + LN_EPS) * g + b


def _merge_kernel(oap_ref, obp_ref, oas_ref, obs_ref, x_ref, wo_ref, g_ref, b_ref, wr_ref, br_ref,
                  x1_ref, ti_ref, gate_ref, *, n_prompt_blocks):
    is_prompt = pl.program_id(0) < n_prompt_blocks
    oa = jnp.where(is_prompt, oap_ref[...], oas_ref[...])
    ob = jnp.where(is_prompt, obp_ref[...], obs_ref[...])
    mix = jnp.dot(oa, wo_ref[:GROUP, :], preferred_element_type=F32)
    mix = mix + jnp.dot(ob, wo_ref[GROUP:, :], preferred_element_type=F32)
    x1 = _layer_norm(DEEPNORM_ALPHA * x_ref[...] + mix, g_ref[...], b_ref[...])
    x1_ref[...] = x1
    logits = jnp.dot(x1.astype(BF16), wr_ref[...], preferred_element_type=F32) + br_ref[...]
    lane = lax.broadcasted_iota(I32, logits.shape, 1)
    logits = jnp.where(lane < N_EXPERTS, logits, -jnp.inf)
    vals, idxs = [], []
    for _ in range(TOP_K):
        mx = jnp.max(logits, axis=1, keepdims=True)
        ix = jnp.min(jnp.where(logits == mx, lane, LANES), axis=1, keepdims=True)
        vals.append(mx)
        idxs.append(ix)
        logits = jnp.where(lane == ix, -jnp.inf, logits)
    col = lax.broadcasted_iota(I32, ti_ref.shape, 1)
    top_v = jnp.broadcast_to(vals[-1], ti_ref.shape)
    top_i = jnp.broadcast_to(idxs[-1], ti_ref.shape)
    for k in range(TOP_K - 1):
        top_v = jnp.where(col == k, vals[k], top_v)
        top_i = jnp.where(col == k, idxs[k], top_i)
    ti_ref[...] = top_i
    e = jnp.exp(top_v - vals[0])
    gate_ref[...] = e / jnp.sum(e, axis=1, keepdims=True)


def _merge(oa_p, ob_p, oa_s, ob_s, x, wo16, ln_g, ln_b, wr16, br):
    t = x.shape[0]
    npb = oa_p.shape[0] // MERGE_ROWS
    row = lambda i: (i, 0)
    const = lambda i: (0, 0)
    blk = lambda cols: pl.BlockSpec((MERGE_ROWS, cols), row)
    prompt = pl.BlockSpec((MERGE_ROWS, GROUP), lambda i: (jnp.minimum(i, npb - 1), 0))
    sample = pl.BlockSpec((MERGE_ROWS, GROUP), lambda i: (jnp.maximum(i - npb, 0), 0))
    return pl.pallas_call(
        functools.partial(_merge_kernel, n_prompt_blocks=npb),
        grid=(t // MERGE_ROWS,),
        in_specs=[prompt, prompt, sample, sample, blk(D_MODEL),
                  pl.BlockSpec(wo16.shape, const), pl.BlockSpec((1, D_MODEL), const),
                  pl.BlockSpec((1, D_MODEL), const), pl.BlockSpec(wr16.shape, const),
                  pl.BlockSpec((1, LANES), const)],
        out_specs=[blk(D_MODEL), blk(TOP_K), blk(TOP_K)],
        out_shape=[jax.ShapeDtypeStruct((t, D_MODEL), F32), jax.ShapeDtypeStruct((t, TOP_K), I32),
                   jax.ShapeDtypeStruct((t, TOP_K), F32)],
        compiler_params=_cparams(("parallel",)),
        name="merge",
    )(oa_p, ob_p, oa_s, ob_s, x, wo16, ln_g.reshape(1, D_MODEL), ln_b.reshape(1, D_MODEL), wr16, br)


def _moe_kernel(blk_e_ref, nused_ref, idx_hbm, x_hbm, wg_ref, wu_ref, wd_ref, bg_ref, bu_ref, bd_ref,
                out_hbm, idx_smem, xbuf, ybuf, isem, gsem, ssem, *, n_tok):
    del blk_e_ref
    i = pl.program_id(0)
    nused = nused_ref[0]
    n_assign = n_tok * TOP_K
    last = nused - 1
    par = lax.rem(i, 2)

    def idx_copy(block, slot):
        return pltpu.make_async_copy(idx_hbm.at[block], idx_smem.at[slot], isem.at[slot])

    def start_gather(islot, xslot):
        for r in range(MOE_ROWS):
            tok = idx_smem[islot, 0, r]
            pltpu.make_async_copy(x_hbm.at[pl.ds(tok, 1)], xbuf.at[xslot, pl.ds(r, 1)],
                                  gsem.at[xslot]).start()

    def wait_gather(xslot):
        pltpu.make_async_copy(x_hbm.at[pl.ds(0, MOE_ROWS)], xbuf.at[xslot], gsem.at[xslot]).wait()

    def start_scatter(islot, yslot):
        for r in range(MOE_ROWS):
            dst = idx_smem[islot, 1, r]
            pltpu.make_async_copy(ybuf.at[yslot, pl.ds(r, 1)], out_hbm.at[pl.ds(dst, 1)],
                                  ssem.at[yslot]).start()

    def slot_flush(yslot, row0):
        return pltpu.make_async_copy(ybuf.at[yslot], out_hbm.at[pl.ds(row0, MOE_ROWS)], ssem.at[yslot])

    @pl.when(i == 0)
    def _():
        ybuf[...] = jnp.zeros(ybuf.shape, F32)
        for p in range(2):
            slot_flush(p, n_assign + p * MOE_ROWS).start()
        idx_copy(0, 0).start()
        idx_copy(0, 0).wait()
        start_gather(0, 0)
        idx_copy(jnp.minimum(1, last), 1).start()

    @pl.when(i < nused)
    def _():
        s_cur = lax.rem(i, 3)
        s_nxt = lax.rem(i + 1, 3)
        s_nn = lax.rem(i + 2, 3)
        idx_copy(0, s_nxt).wait()
        start_gather(s_nxt, 1 - par)
        idx_copy(jnp.minimum(i + 2, last), s_nn).start()

        wait_gather(par)
        x = xbuf[par].astype(BF16)
        hg = jnp.dot(x, wg_ref[...], preferred_element_type=F32) + bg_ref[...]
        hu = jnp.dot(x, wu_ref[...], preferred_element_type=F32) + bu_ref[...]
        gate = jnp.minimum(hg, SWIGLU_LIMIT)
        up = jnp.clip(hu, -SWIGLU_LIMIT, SWIGLU_LIMIT)
        glu = gate * (1.0 / (1.0 + jnp.exp(-(gate * SWIGLU_ALPHA))))
        act = ((up + 1.0) * glu).astype(BF16)
        y = jnp.dot(act, wd_ref[...], preferred_element_type=F32) + bd_ref[...]

        slot_flush(par, 0).wait()
        ybuf[par] = y
        start_scatter(s_cur, par)

        @pl.when(i == last)
        def _():
            slot_flush(par, 0).wait()
            slot_flush(1 - par, 0).wait()
            wait_gather(1 - par)
            idx_copy(0, s_nn).wait()


def _moe(x1, idx, blk_e, nused, wg16, wu16, wd16, bg, bu, bd):
    t = x1.shape[0]
    nblk = idx.shape[0]
    wmap = lambda i, be, nu: (be[i], 0, 0)
    grid_spec = pltpu.PrefetchScalarGridSpec(
        num_scalar_prefetch=2,
        grid=(nblk,),
        in_specs=[
            pl.BlockSpec(memory_space=pl.ANY),
            pl.BlockSpec(memory_space=pl.ANY),
            pl.BlockSpec((None, D_MODEL, D_MODEL), wmap),
            pl.BlockSpec((None, D_MODEL, D_MODEL), wmap),
            pl.BlockSpec((None, D_MODEL, D_MODEL), wmap),
            pl.BlockSpec((None, 1, D_MODEL), wmap),
            pl.BlockSpec((None, 1, D_MODEL), wmap),
            pl.BlockSpec((None, 1, D_MODEL), wmap),
        ],
        out_specs=pl.BlockSpec(memory_space=pl.ANY),
        scratch_shapes=[
            pltpu.SMEM((3, 2, MOE_ROWS), I32),
            pltpu.VMEM((2, MOE_ROWS, D_MODEL), F32),
            pltpu.VMEM((2, MOE_ROWS, D_MODEL), F32),
            pltpu.SemaphoreType.DMA((3,)),
            pltpu.SemaphoreType.DMA((2,)),
            pltpu.SemaphoreType.DMA((2,)),
        ],
    )
    return pl.pallas_call(
        functools.partial(_moe_kernel, n_tok=t),
        grid_spec=grid_spec,
        out_shape=jax.ShapeDtypeStruct((t * TOP_K + 2 * MOE_ROWS, D_MODEL), F32),
        compiler_params=_cparams(("arbitrary",)),
        name="moe",
    )(blk_e, nused, idx, x1, wg16, wu16, wd16, bg, bu, bd)


def _route(top_i):
    t = top_i.shape[0]
    n = t * TOP_K
    nblk = n // MOE_ROWS + N_EXPERTS
    flat_e = top_i.reshape(-1)
    order = jnp.argsort(flat_e, stable=True).astype(I32)
    experts = jnp.arange(N_EXPERTS, dtype=I32)
    counts = jnp.sum((flat_e[:, None] == experts[None, :]).astype(I32), axis=0)
    padded = ((counts + MOE_ROWS - 1) // MOE_ROWS) * MOE_ROWS
    pad_end = jnp.cumsum(padded)
    pad_start = pad_end - padded
    start = jnp.cumsum(counts) - counts
    nused = (pad_end[-1] // MOE_ROWS).astype(I32).reshape(1)
    blk_first = jnp.arange(nblk, dtype=I32) * MOE_ROWS
    blk_e = jnp.minimum(jnp.sum((blk_first[:, None] >= pad_end[None, :]).astype(I32), axis=1),
                        N_EXPERTS - 1)
    row = jnp.arange(MOE_ROWS, dtype=I32)[None, :]
    j = blk_first[:, None] + row - pad_start[blk_e][:, None]
    valid = j < counts[blk_e][:, None]
    a = order[jnp.clip(start[blk_e][:, None] + j, 0, n - 1)]
    tok = lax.shift_right_logical(a, TOP_SHIFT)
    src = jnp.where(valid, tok, 0)
    dump = n + (jnp.arange(nblk, dtype=I32)[:, None] % 2) * MOE_ROWS + row
    dst = jnp.where(valid, (a & (TOP_K - 1)) * t + tok, dump)
    return jnp.stack([src, dst], axis=1).astype(I32), blk_e.astype(I32), nused


def _final_kernel(x1_ref, gate_ref, y0_ref, y1_ref, y2_ref, y3_ref, g_ref, b_ref, o_ref):
    gates = gate_ref[...]
    y = gates[:, 0:1] * y0_ref[...]
    for k, ref in enumerate((y1_ref, y2_ref, y3_ref), start=1):
        y = y + gates[:, k:k + 1] * ref[...]
    o_ref[...] = _layer_norm(DEEPNORM_ALPHA * x1_ref[...] + y, g_ref[...], b_ref[...])


def _final(x1, gates, planes, ln_g, ln_b):
    t = x1.shape[0]
    nb = t // FINAL_ROWS
    row = lambda i: (i, 0)
    const = lambda i: (0, 0)
    plane = lambda k: pl.BlockSpec((FINAL_ROWS, D_MODEL), lambda i: (k * nb + i, 0))
    return pl.pallas_call(
        _final_kernel,
        grid=(nb,),
        in_specs=[pl.BlockSpec((FINAL_ROWS, D_MODEL), row), pl.BlockSpec((FINAL_ROWS, TOP_K), row),
                  plane(0), plane(1), plane(2), plane(3),
                  pl.BlockSpec((1, D_MODEL), const), pl.BlockSpec((1, D_MODEL), const)],
        out_specs=pl.BlockSpec((FINAL_ROWS, D_MODEL), row),
        out_shape=jax.ShapeDtypeStruct((t, D_MODEL), F32),
        compiler_params=_cparams(("parallel",)),
        name="final",
    )(x1, gates, planes, planes, planes, planes, ln_g.reshape(1, D_MODEL), ln_b.reshape(1, D_MODEL))


def kernel(x_prompt, x_sample, cache_a_k, cache_a_v, cache_b_k, cache_b_v, w_in, lambda_qk, subln_g,
           rel_bias, w_out, ln1_g, ln1_b, w_router, b_router, w_gate_up, b_gate_up, w_down, b_down,
           ln2_g, ln2_b):
    b, s, d = x_prompt.shape
    bd, sd, _ = x_sample.shape
    depth, _, past, _, _ = cache_a_k.shape
    lb = cache_b_k.shape[2]
    assert depth == 1 and b == 1 and d == D_MODEL
    assert sd == CHUNK and past % CHUNK == 0 and lb == BAND_PAST and s >= BAND_PAST
    assert s % ATT_TQ == 0 and s % BAND_TQ == 0
    ts = bd * sd
    t = s + ts
    lam_init = _lambda_init(0)

    x = jnp.concatenate([x_prompt.reshape(s, d), x_sample.reshape(ts, d)], axis=0)
    pos = jnp.concatenate([jnp.arange(s, dtype=I32),
                           jnp.tile(past + jnp.arange(sd, dtype=I32), bd)])
    lq = lambda_qk[0].astype(F32)
    lam = (jnp.exp(jnp.sum(lq[0] * lq[1])) - jnp.exp(jnp.sum(lq[2] * lq[3])) + lam_init).reshape(1)

    (qa16, ka32, ka16, va32, va16, vat16,
     qb16, kb32, kb16, vb32, vb16) = _project(x, w_in[0].astype(BF16), pos)

    oa_p = _attn_a_prompt(qa16, ka16, vat16, lam, subln_g[0], s, lam_init)
    oa_s = _attn_a_sample(qa16, ka16, va16, cache_a_k[0].reshape(bd, past, GROUP),
                          cache_a_v[0].reshape(bd, past, GROUP), lam, subln_g[0], s, lam_init)

    ob_p = _attn_b_prompt(qb16, kb16, vb16, _band_bias(rel_bias[0], BAND_TQ // CHUNK), s)
    ob_s = _attn_b_sample(qb16, kb16, vb16, cache_b_k[0].reshape(bd, lb, GROUP),
                          cache_b_v[0].reshape(bd, lb, GROUP), _band_bias(rel_bias[0], 1), s)

    wr16 = jnp.pad(w_router[0], ((0, 0), (0, LANES - N_EXPERTS))).astype(BF16)
    br = jnp.pad(b_router[0], (0, LANES - N_EXPERTS)).reshape(1, LANES)
    x1, top_i, gates = _merge(oa_p, ob_p, oa_s, ob_s, x, w_out[0].astype(BF16), ln1_g[0], ln1_b[0],
                              wr16, br)

    idx, blk_e, nused = _route(top_i)
    wgu = w_gate_up[0].reshape(N_EXPERTS, D_MODEL, D_MODEL, 2)
    bgu = b_gate_up[0].reshape(N_EXPERTS, 1, D_MODEL, 2)
    planes = _moe(x1, idx, blk_e, nused, wgu[..., 0].astype(BF16), wgu[..., 1].astype(BF16),
                  w_down[0].astype(BF16), bgu[..., 0], bgu[..., 1], b_down[0].reshape(N_EXPERTS, 1, D_MODEL))
    y = _final(x1, gates, planes, ln2_g[0], ln2_b[0])

    heads_a = lambda a, n: a.reshape(1, n, -1, N_COMP, D_HEAD_A)
    vals_a = lambda a, n: a.reshape(1, n, -1, H_A, 2 * D_HEAD_A)
    heads_b = lambda a, n: a.reshape(1, n, -1, H_B, D_HEAD_B)
    keep_s = min(BAND_PAST, lb + sd)
    kb_s = jnp.concatenate([cache_b_k[0], kb32[s:].reshape(bd, sd, H_B, D_HEAD_B)], axis=1)[:, lb + sd - keep_s:]
    vb_s = jnp.concatenate([cache_b_v[0], vb32[s:].reshape(bd, sd, H_B, D_HEAD_B)], axis=1)[:, lb + sd - keep_s:]
    return (y[:s].reshape(b, s, d), y[s:].reshape(bd, sd, d),
            heads_a(ka32[:s], b), vals_a(va32[:s], b),
            heads_b(kb32[s - BAND_PAST:s], b), heads_b(vb32[s - BAND_PAST:s], b),
            heads_a(ka32[s:], bd), vals_a(va32[s:], bd),
            kb_s[None], vb_s[None])
```

```python
import functools
import math

import jax
import jax.numpy as jnp
from jax import lax
from jax.experimental import pallas as pl
from jax.experimental.pallas import tpu as pltpu

F32 = jnp.float32
BF16 = jnp.bfloat16
I32 = jnp.int32

D_MODEL = 1024
CHUNK = 64
D_HEAD_A = 64
H_A = 4
N_COMP = 2 * H_A
ROPE_DIM = D_HEAD_A // 4
ROPE_THETA = 500000.0
D_HEAD_B = 64
H_B = 8
BAND_CHUNKS = 8
BAND_PAST = BAND_CHUNKS * CHUNK
REL_CLIP = 128
GROUP = 512
N_EXPERTS = 32
TOP_K = 4
TOP_SHIFT = 2
SWIGLU_LIMIT = 7.0
SWIGLU_ALPHA = 1.702
LN_EPS = 1e-5
SUBLN_EPS = 1e-5
DEPTH = 1
DEEPNORM_ALPHA = (2.0 * DEPTH) ** 0.25
NEG_INF = -1e30
LANES = 128
PAIR = 2 * D_HEAD_A
VT_ROWS = PAIR + 16
LOG2_E = math.log2(math.e)

VMEM_LIMIT = 56 * 1024 * 1024

PROJ_ROWS = 256
ATT_TQ = 512
ATT_TK = 1024
BAND_TQ = 256
MERGE_ROWS = 256
MOE_ROWS = 256
FINAL_ROWS = 256


def _lambda_init(layer_idx):
    return 0.8 - 0.6 * math.exp(-0.3 * layer_idx)


def _cparams(sem):
    return pltpu.CompilerParams(dimension_semantics=sem, vmem_limit_bytes=VMEM_LIMIT)


def _half_mask(shape, upper):
    lane = lax.broadcasted_iota(I32, shape, len(shape) - 1)
    return (lane >= D_HEAD_A) if upper else (lane < D_HEAD_A)


def _proj_kernel(x_ref, w_ref, cos_ref, s1_ref, s2_ref,
                 qa_ref, ka32_ref, ka16_ref, va32_ref, va16_ref, vat_ref,
                 qb_ref, kb32_ref, kb16_ref, vb32_ref, vb16_ref):
    xb = x_ref[...].astype(BF16)

    def group(g):
        return jnp.dot(xb, w_ref[:, g * GROUP:(g + 1) * GROUP], preferred_element_type=F32)

    cos = cos_ref[...]
    s1 = s1_ref[...]
    s2 = s2_ref[...]

    def rope(h):
        parts = []
        for j in range(GROUP // LANES):
            hj = h[:, j * LANES:(j + 1) * LANES]
            nxt = pltpu.roll(hj, LANES - ROPE_DIM // 2, 1)
            prv = pltpu.roll(hj, ROPE_DIM // 2, 1)
            parts.append(hj * cos + nxt * s1 + prv * s2)
        return jnp.concatenate(parts, axis=1)

    qa = rope(group(0)) * (D_HEAD_A ** -0.5 * LOG2_E)
    qa_ref[...] = qa.astype(BF16)
    ka = rope(group(1))
    ka32_ref[...] = ka
    ka16_ref[...] = ka.astype(BF16)
    va = group(2)
    va32_ref[...] = va
    va16_ref[...] = va.astype(BF16)
    vt = va.T
    rows = vt.shape[1]
    tail = jnp.concatenate([jnp.ones((1, rows), F32), jnp.zeros((VT_ROWS - PAIR - 1, rows), F32)], axis=0)
    parts = []
    for h in range(H_A):
        parts += [vt[h * PAIR:(h + 1) * PAIR], tail]
    vat_ref[...] = jnp.concatenate(parts, axis=0).astype(BF16)
    qb_ref[...] = (group(3) * (D_HEAD_B ** -0.5)).astype(BF16)
    kb = group(4)
    kb32_ref[...] = kb
    kb16_ref[...] = kb.astype(BF16)
    vb = group(5)
    vb32_ref[...] = vb
    vb16_ref[...] = vb.astype(BF16)


def _rope_tables(pos):
    half = ROPE_DIM // 2
    inv = ROPE_THETA ** (-jnp.arange(half, dtype=F32) * 2.0 / ROPE_DIM)
    ang = pos.astype(F32)[:, None] * jnp.tile(inv, LANES // half)[None, :]
    in_head = (jnp.arange(LANES, dtype=I32) % D_HEAD_A)[None, :]
    cos = jnp.where(in_head < ROPE_DIM, jnp.cos(ang), 1.0)
    sin = jnp.sin(ang)
    s1 = jnp.where(in_head < half, -sin, 0.0)
    s2 = jnp.where((in_head >= half) & (in_head < ROPE_DIM), sin, 0.0)
    return cos, s1, s2


def _project(x, w16, pos):
    t = x.shape[0]
    cos, s1, s2 = _rope_tables(pos)
    row = lambda i: (i, 0)
    blk = lambda cols: pl.BlockSpec((PROJ_ROWS, cols), row)
    f32o = jax.ShapeDtypeStruct((t, GROUP), F32)
    b16o = jax.ShapeDtypeStruct((t, GROUP), BF16)
    return pl.pallas_call(
        _proj_kernel,
        grid=(t // PROJ_ROWS,),
        in_specs=[blk(D_MODEL), pl.BlockSpec(w16.shape, lambda i: (0, 0)),
                  blk(LANES), blk(LANES), blk(LANES)],
        out_specs=[blk(GROUP), blk(GROUP), blk(GROUP), blk(GROUP), blk(GROUP),
                   pl.BlockSpec((H_A * VT_ROWS, PROJ_ROWS), lambda i: (0, i)),
                   blk(GROUP), blk(GROUP), blk(GROUP), blk(GROUP), blk(GROUP)],
        out_shape=[b16o, f32o, b16o, f32o, b16o, jax.ShapeDtypeStruct((H_A * VT_ROWS, t), BF16),
                   b16o, f32o, b16o, f32o, b16o],
        compiler_params=_cparams(("parallel",)),
        name="proj",
    )(x, w16, cos, s1, s2)


def _subln(o, g, lam_init, axis):
    o = o * lax.rsqrt(jnp.mean(jnp.square(o), axis=axis, keepdims=True) + SUBLN_EPS)
    return o * g * (1.0 - lam_init)


def _half_only(qp, upper):
    return jnp.where(_half_mask(qp.shape, upper), qp, jnp.zeros_like(qp))


def _attn_a_prompt_kernel(qi_ref, kj_ref, lam_ref, q_ref, k_ref, vt_ref, dmask_ref, g_ref, o_ref,
                          m_ref, acc_ref, *, lam_init):
    s = pl.program_id(0)
    qi = qi_ref[s]
    kj = kj_ref[s]

    @pl.when(kj == 0)
    def _():
        m_ref[...] = jnp.full(m_ref.shape, -jnp.inf, F32)
        acc_ref[...] = jnp.zeros(acc_ref.shape, F32)

    def step(diagonal):
        def scores(c):
            h = c // 2
            kp = k_ref[:, h * PAIR:(h + 1) * PAIR]
            qz = _half_only(q_ref[:, h * PAIR:(h + 1) * PAIR], c % 2 == 1)
            st = lax.dot_general(kp, qz, (((1,), (1,)), ((), ())), preferred_element_type=F32)
            return st + dmask_ref[...] if diagonal else st

        st_next = scores(0)
        for c in range(N_COMP):
            st = st_next
            if c + 1 < N_COMP:
                st_next = scores(c + 1)
            m_old = m_ref[c:c + 1, :]
            m_new = jnp.maximum(m_old, jnp.max(st, axis=0, keepdims=True))
            alpha = jnp.exp2(m_old - m_new)
            p = jnp.exp2(st - m_new).astype(BF16)
            h = c // 2
            pv = jnp.dot(vt_ref[h * VT_ROWS:(h + 1) * VT_ROWS, :], p, preferred_element_type=F32)
            acc_ref[c] = alpha * acc_ref[c] + pv
            m_ref[c:c + 1, :] = m_new

    last_kj = qi // (ATT_TK // ATT_TQ)

    @pl.when(kj < last_kj)
    def _():
        step(False)

    @pl.when(kj == last_kj)
    def _():
        step(True)
        lam = lam_ref[0]
        for h in range(H_A):
            a0 = acc_ref[2 * h]
            a1 = acc_ref[2 * h + 1]
            o0 = a0[:PAIR] / a0[PAIR:PAIR + 1]
            o1 = a1[:PAIR] / a1[PAIR:PAIR + 1]
            ot = _subln(o0 - lam * o1, g_ref[...], lam_init, 0)
            o_ref[:, h * PAIR:(h + 1) * PAIR] = ot.T.astype(BF16)


def _attn_a_prompt(qa16, ka16, vat16, lam, subln_g, s_len, lam_init):
    ratio = ATT_TK // ATT_TQ
    assert ATT_TK == ratio * ATT_TQ and s_len % ATT_TK == 0
    kchunk = jnp.arange(ATT_TK, dtype=I32)[None, :, None] // CHUNK
    qchunk = (jnp.arange(ratio, dtype=I32)[:, None, None] * ATT_TQ
              + jnp.arange(ATT_TQ, dtype=I32)[None, None, :]) // CHUNK
    dmask = jnp.where(kchunk <= qchunk, 0.0, NEG_INF).astype(F32)
    nq = s_len // ATT_TQ
    qi_tab, kj_tab = [], []
    for i in range(nq):
        for j in range(i // ratio + 1):
            qi_tab.append(i)
            kj_tab.append(j)
    qi_tab = jnp.asarray(qi_tab, I32)
    kj_tab = jnp.asarray(kj_tab, I32)
    grid_spec = pltpu.PrefetchScalarGridSpec(
        num_scalar_prefetch=2,
        grid=(int(qi_tab.shape[0]),),
        in_specs=[
            pl.BlockSpec(memory_space=pltpu.SMEM),
            pl.BlockSpec((ATT_TQ, GROUP), lambda s, qi, kj: (qi[s], 0)),
            pl.BlockSpec((ATT_TK, GROUP), lambda s, qi, kj: (kj[s], 0)),
            pl.BlockSpec((H_A * VT_ROWS, ATT_TK), lambda s, qi, kj: (0, kj[s])),
            pl.BlockSpec((None, ATT_TK, ATT_TQ), lambda s, qi, kj: (qi[s] % ratio, 0, 0)),
            pl.BlockSpec((PAIR, 1), lambda s, qi, kj: (0, 0)),
        ],
        out_specs=pl.BlockSpec((ATT_TQ, GROUP), lambda s, qi, kj: (qi[s], 0)),
        scratch_shapes=[pltpu.VMEM((N_COMP, ATT_TQ), F32),
                        pltpu.VMEM((N_COMP, VT_ROWS, ATT_TQ), F32)],
    )
    return pl.pallas_call(
        functools.partial(_attn_a_prompt_kernel, lam_init=lam_init),
        grid_spec=grid_spec,
        out_shape=jax.ShapeDtypeStruct((s_len, GROUP), BF16),
        compiler_params=_cparams(("arbitrary",)),
        name="attn_a_prompt",
    )(qi_tab, kj_tab, lam, qa16, ka16, vat16, dmask, subln_g.reshape(PAIR, 1))


def _attend(qz, segs, exp=jnp.exp):
    scores = []
    for k, _, bias, mask in segs:
        sc = lax.dot_general(qz, k, (((1,), (1,)), ((), ())), preferred_element_type=F32)
        if bias is not None:
            sc = sc + bias
        if mask is not None:
            sc = jnp.where(mask, sc, NEG_INF)
        scores.append(sc)
    m = functools.reduce(jnp.maximum, [jnp.max(sc, axis=1, keepdims=True) for sc in scores])
    l = None
    o = None
    for sc, (_, v, _, _) in zip(scores, segs):
        p = exp(sc - m)
        ls = jnp.sum(p, axis=1, keepdims=True)
        os_ = jnp.dot(p.astype(BF16), v, preferred_element_type=F32)
        l = ls if l is None else l + ls
        o = os_ if o is None else o + os_
    return o / l


def _attn_a_sample_kernel(lam_ref, q_ref, kn_ref, vn_ref, kc_ref, vc_ref, g_ref, o_ref, *, lam_init):
    lam = lam_ref[0]
    for h in range(H_A):
        cols = slice(h * PAIR, (h + 1) * PAIR)
        segs = [(kc_ref[:, cols].astype(BF16), vc_ref[:, cols].astype(BF16), None, None),
                (kn_ref[:, cols], vn_ref[:, cols], None, None)]
        qp = q_ref[:, cols]
        o0 = _attend(_half_only(qp, False), segs, jnp.exp2)
        o1 = _attend(_half_only(qp, True), segs, jnp.exp2)
        o = _subln(o0 - lam * o1, g_ref[...], lam_init, 1)
        o_ref[:, cols] = o.astype(BF16)


def _attn_a_sample(qa16, ka16, va16, cache_k, cache_v, lam, subln_g, row0, lam_init):
    nb, past, _ = cache_k.shape
    sd = CHUNK
    new = lambda b: (row0 // sd + b, 0)
    return pl.pallas_call(
        functools.partial(_attn_a_sample_kernel, lam_init=lam_init),
        grid=(nb,),
        in_specs=[
            pl.BlockSpec(memory_space=pltpu.SMEM),
            pl.BlockSpec((sd, GROUP), new), pl.BlockSpec((sd, GROUP), new), pl.BlockSpec((sd, GROUP), new),
            pl.BlockSpec((None, past, GROUP), lambda b: (b, 0, 0)),
            pl.BlockSpec((None, past, GROUP), lambda b: (b, 0, 0)),
            pl.BlockSpec((1, PAIR), lambda b: (0, 0)),
        ],
        out_specs=pl.BlockSpec((sd, GROUP), lambda b: (b, 0)),
        out_shape=jax.ShapeDtypeStruct((nb * sd, GROUP), BF16),
        compiler_params=_cparams(("parallel",)),
        name="attn_a_sample",
    )(lam, qa16, ka16, va16, cache_k, cache_v, subln_g.reshape(1, PAIR))


def _band_bias(rel_bias, n_q_chunks):
    nq = n_q_chunks * CHUNK
    nk = nq + BAND_PAST
    heads = rel_bias.shape[0]
    n_lo = BAND_PAST - REL_CLIP
    period = max(nq + nk, n_lo + (2 * REL_CLIP + 1) + nq - 1)
    n_hi = period - (nq - 1) - n_lo - (2 * REL_CLIP + 1)
    assert n_lo >= 0
    lo = jnp.broadcast_to(rel_bias[:, :1], (heads, n_lo))
    hi = jnp.broadcast_to(rel_bias[:, -1:], (heads, n_hi))
    wrap = jnp.broadcast_to(rel_bias[:, :1], (heads, nq - 1))
    e = jnp.concatenate([lo, rel_bias, hi, wrap], axis=1).astype(F32)
    flat = jnp.broadcast_to(e[:, None, :], (heads, nq, period)).reshape(heads, nq * period)
    bias = flat[:, :nq * (period - 1)].reshape(heads, nq, period - 1)[:, :, :nk]
    qpos = jnp.arange(nq, dtype=I32)[:, None]
    kpos = jnp.arange(nk, dtype=I32)[None, :] - BAND_PAST
    dch = qpos // CHUNK - jnp.floor_divide(kpos, CHUNK)
    ok = (dch >= 0) & (dch <= BAND_CHUNKS)
    return jnp.where(ok[None], bias, NEG_INF)


def _band_heads(q_ref, seg_refs, bias_ref, o_ref):
    for j in range(H_B // 2):
        cols = slice(j * PAIR, (j + 1) * PAIR)
        qp = q_ref[:, cols]
        res = []
        for half in range(2):
            segs = []
            for k_ref, v_ref, col0, mask in seg_refs:
                n = k_ref.shape[0]
                segs.append((k_ref[:, cols].astype(BF16), v_ref[:, cols].astype(BF16),
                             bias_ref[2 * j + half, :, col0:col0 + n], mask))
            res.append(_attend(_half_only(qp, half == 1), segs))
        o = jnp.where(_half_mask(res[0].shape, False), res[0], res[1])
        o_ref[:, cols] = o.astype(BF16)


def _attn_b_prompt_kernel(q_ref, k0_ref, k1_ref, k2_ref, v0_ref, v1_ref, v2_ref, bias_ref, o_ref):
    i = pl.program_id(0)
    kpos = lax.broadcasted_iota(I32, (BAND_TQ, BAND_TQ), 1) + (i * BAND_TQ - BAND_PAST)
    seg_refs = [(k0_ref, v0_ref, 0, kpos >= 0),
                (k1_ref, v1_ref, BAND_TQ, kpos + BAND_TQ >= 0),
                (k2_ref, v2_ref, 2 * BAND_TQ, None)]
    _band_heads(q_ref, seg_refs, bias_ref, o_ref)


def _attn_b_prompt(qb16, kb16, vb16, bias, s_len):
    assert BAND_PAST == 2 * BAND_TQ
    cur = lambda i: (i, 0)
    p1 = lambda i: (jnp.maximum(i - 1, 0), 0)
    p2 = lambda i: (jnp.maximum(i - 2, 0), 0)
    blk = lambda f: pl.BlockSpec((BAND_TQ, GROUP), f)
    return pl.pallas_call(
        _attn_b_prompt_kernel,
        grid=(s_len // BAND_TQ,),
        in_specs=[blk(cur), blk(p2), blk(p1), blk(cur), blk(p2), blk(p1), blk(cur),
                  pl.BlockSpec(bias.shape, lambda i: (0, 0, 0))],
        out_specs=blk(cur),
        out_shape=jax.ShapeDtypeStruct((s_len, GROUP), BF16),
        compiler_params=_cparams(("parallel",)),
        name="attn_b_prompt",
    )(qb16, kb16, kb16, kb16, vb16, vb16, vb16, bias)


def _attn_b_sample_kernel(q_ref, kn_ref, vn_ref, kc_ref, vc_ref, bias_ref, o_ref):
    seg_refs = [(kc_ref, vc_ref, 0, None), (kn_ref, vn_ref, kc_ref.shape[0], None)]
    _band_heads(q_ref, seg_refs, bias_ref, o_ref)


def _attn_b_sample(qb16, kb16, vb16, cache_k, cache_v, bias, row0):
    nb, lb, _ = cache_k.shape
    sd = CHUNK
    new = lambda b: (row0 // sd + b, 0)
    return pl.pallas_call(
        _attn_b_sample_kernel,
        grid=(nb,),
        in_specs=[
            pl.BlockSpec((sd, GROUP), new), pl.BlockSpec((sd, GROUP), new), pl.BlockSpec((sd, GROUP), new),
            pl.BlockSpec((None, lb, GROUP), lambda b: (b, 0, 0)),
            pl.BlockSpec((None, lb, GROUP), lambda b: (b, 0, 0)),
            pl.BlockSpec(bias.shape, lambda b: (0, 0, 0)),
        ],
        out_specs=pl.BlockSpec((sd, GROUP), lambda b: (b, 0)),
        out_shape=jax.ShapeDtypeStruct((nb * sd, GROUP), BF16),
        compiler_params=_cparams(("parallel",)),
        name="attn_b_sample",
    )(qb16, kb16, vb16, cache_k, cache_v, bias)


def _layer_norm(z, g, b):
    mu = jnp.mean(z, axis=-1, keepdims=True)
    var = jnp.mean(jnp.square(z - mu), axis=-1, keepdims=True)
    return (z - mu) * lax.rsqrt(var + LN_EPS) * g + b


def _merge_kernel(oap_ref, obp_ref, oas_ref, obs_ref, x_ref, wo_ref, g_ref, b_ref, wr_ref, br_ref,
                  x1_ref, ti_ref, gate_ref, *, n_prompt_blocks):
    is_prompt = pl.program_id(0) < n_prompt_blocks
    oa = jnp.where(is_prompt, oap_ref[...], oas_ref[...])
    ob = jnp.where(is_prompt, obp_ref[...], obs_ref[...])
    mix = jnp.dot(oa, wo_ref[:GROUP, :], preferred_element_type=F32)
    mix = mix + jnp.dot(ob, wo_ref[GROUP:, :], preferred_element_type=F32)
    x1 = _layer_norm(DEEPNORM_ALPHA * x_ref[...] + mix, g_ref[...], b_ref[...])
    x1_ref[...] = x1
    logits = jnp.dot(x1.astype(BF16), wr_ref[...], preferred_element_type=F32) + br_ref[...]
    lane = lax.broadcasted_iota(I32, logits.shape, 1)
    logits = jnp.where(lane < N_EXPERTS, logits, -jnp.inf)
    vals, idxs = [], []
    for _ in range(TOP_K):
        mx = jnp.max(logits, axis=1, keepdims=True)
        ix = jnp.min(jnp.where(logits == mx, lane, LANES), axis=1, keepdims=True)
        vals.append(mx)
        idxs.append(ix)
        logits = jnp.where(lane == ix, -jnp.inf, logits)
    col = lax.broadcasted_iota(I32, ti_ref.shape, 1)
    top_v = jnp.broadcast_to(vals[-1], ti_ref.shape)
    top_i = jnp.broadcast_to(idxs[-1], ti_ref.shape)
    for k in range(TOP_K - 1):
        top_v = jnp.where(col == k, vals[k], top_v)
        top_i = jnp.where(col == k, idxs[k], top_i)
    ti_ref[...] = top_i
    e = jnp.exp(top_v - vals[0])
    gate_ref[...] = e / jnp.sum(e, axis=1, keepdims=True)


def _merge(oa_p, ob_p, oa_s, ob_s, x, wo16, ln_g, ln_b, wr16, br):
    t = x.shape[0]
    npb = oa_p.shape[0] // MERGE_ROWS
    row = lambda i: (i, 0)
    const = lambda i: (0, 0)
    blk = lambda cols: pl.BlockSpec((MERGE_ROWS, cols), row)
    prompt = pl.BlockSpec((MERGE_ROWS, GROUP), lambda i: (jnp.minimum(i, npb - 1), 0))
    sample = pl.BlockSpec((MERGE_ROWS, GROUP), lambda i: (jnp.maximum(i - npb, 0), 0))
    return pl.pallas_call(
        functools.partial(_merge_kernel, n_prompt_blocks=npb),
        grid=(t // MERGE_ROWS,),
        in_specs=[prompt, prompt, sample, sample, blk(D_MODEL),
                  pl.BlockSpec(wo16.shape, const), pl.BlockSpec((1, D_MODEL), const),
                  pl.BlockSpec((1, D_MODEL), const), pl.BlockSpec(wr16.shape, const),
                  pl.BlockSpec((1, LANES), const)],
        out_specs=[blk(D_MODEL), blk(TOP_K), blk(TOP_K)],
        out_shape=[jax.ShapeDtypeStruct((t, D_MODEL), F32), jax.ShapeDtypeStruct((t, TOP_K), I32),
                   jax.ShapeDtypeStruct((t, TOP_K), F32)],
        compiler_params=_cparams(("parallel",)),
        name="merge",
    )(oa_p, ob_p, oa_s, ob_s, x, wo16, ln_g.reshape(1, D_MODEL), ln_b.reshape(1, D_MODEL), wr16, br)


def _moe_kernel(blk_e_ref, nused_ref, idx_hbm, x_hbm, wg_ref, wu_ref, wd_ref, bg_ref, bu_ref, bd_ref,
                out_hbm, idx_smem, xbuf, ybuf, isem, gsem, ssem, *, n_tok):
    del blk_e_ref
    i = pl.program_id(0)
    nused = nused_ref[0]
    n_assign = n_tok * TOP_K
    last = nused - 1
    par = lax.rem(i, 2)

    def idx_copy(block, slot):
        return pltpu.make_async_copy(idx_hbm.at[block], idx_smem.at[slot], isem.at[slot])

    def start_gather(islot, xslot):
        for r in range(MOE_ROWS):
            tok = idx_smem[islot, 0, r]
            pltpu.make_async_copy(x_hbm.at[pl.ds(tok, 1)], xbuf.at[xslot, pl.ds(r, 1)],
                                  gsem.at[xslot]).start()

    def wait_gather(xslot):
        pltpu.make_async_copy(x_hbm.at[pl.ds(0, MOE_ROWS)], xbuf.at[xslot], gsem.at[xslot]).wait()

    def start_scatter(islot, yslot):
        for r in range(MOE_ROWS):
            dst = idx_smem[islot, 1, r]
            pltpu.make_async_copy(ybuf.at[yslot, pl.ds(r, 1)], out_hbm.at[pl.ds(dst, 1)],
                                  ssem.at[yslot]).start()

    def slot_flush(yslot, row0):
        return pltpu.make_async_copy(ybuf.at[yslot], out_hbm.at[pl.ds(row0, MOE_ROWS)], ssem.at[yslot])

    @pl.when(i == 0)
    def _():
        ybuf[...] = jnp.zeros(ybuf.shape, F32)
        for p in range(2):
            slot_flush(p, n_assign + p * MOE_ROWS).start()
        idx_copy(0, 0).start()
        idx_copy(0, 0).wait()
        start_gather(0, 0)
        idx_copy(jnp.minimum(1, last), 1).start()

    @pl.when(i < nused)
    def _():
        s_cur = lax.rem(i, 3)
        s_nxt = lax.rem(i + 1, 3)
        s_nn = lax.rem(i + 2, 3)
        idx_copy(0, s_nxt).wait()
        start_gather(s_nxt, 1 - par)
        idx_copy(jnp.minimum(i + 2, last), s_nn).start()

        wait_gather(par)
        x = xbuf[par].astype(BF16)
        hg = jnp.dot(x, wg_ref[...], preferred_element_type=F32) + bg_ref[...]
        hu = jnp.dot(x, wu_ref[...], preferred_element_type=F32) + bu_ref[...]
        gate = jnp.minimum(hg, SWIGLU_LIMIT)
        up = jnp.clip(hu, -SWIGLU_LIMIT, SWIGLU_LIMIT)
        glu = gate * (1.0 / (1.0 + jnp.exp(-(gate * SWIGLU_ALPHA))))
        act = ((up + 1.0) * glu).astype(BF16)
        y = jnp.dot(act, wd_ref[...], preferred_element_type=F32) + bd_ref[...]

        slot_flush(par, 0).wait()
        ybuf[par] = y
        start_scatter(s_cur, par)

        @pl.when(i == last)
        def _():
            slot_flush(par, 0).wait()
            slot_flush(1 - par, 0).wait()
            wait_gather(1 - par)
            idx_copy(0, s_nn).wait()


def _moe(x1, idx, blk_e, nused, wg16, wu16, wd16, bg, bu, bd):
    t = x1.shape[0]
    nblk = idx.shape[0]
    wmap = lambda i, be, nu: (be[i], 0, 0)
    grid_spec = pltpu.PrefetchScalarGridSpec(
        num_scalar_prefetch=2,
        grid=(nblk,),
        in_specs=[
            pl.BlockSpec(memory_space=pl.ANY),
            pl.BlockSpec(memory_space=pl.ANY),
            pl.BlockSpec((None, D_MODEL, D_MODEL), wmap),
            pl.BlockSpec((None, D_MODEL, D_MODEL), wmap),
            pl.BlockSpec((None, D_MODEL, D_MODEL), wmap),
            pl.BlockSpec((None, 1, D_MODEL), wmap),
            pl.BlockSpec((None, 1, D_MODEL), wmap),
            pl.BlockSpec((None, 1, D_MODEL), wmap),
        ],
        out_specs=pl.BlockSpec(memory_space=pl.ANY),
        scratch_shapes=[
            pltpu.SMEM((3, 2, MOE_ROWS), I32),
            pltpu.VMEM((2, MOE_ROWS, D_MODEL), F32),
            pltpu.VMEM((2, MOE_ROWS, D_MODEL), F32),
            pltpu.SemaphoreType.DMA((3,)),
            pltpu.SemaphoreType.DMA((2,)),
            pltpu.SemaphoreType.DMA((2,)),
        ],
    )
    return pl.pallas_call(
        functools.partial(_moe_kernel, n_tok=t),
        grid_spec=grid_spec,
        out_shape=jax.ShapeDtypeStruct((t * TOP_K + 2 * MOE_ROWS, D_MODEL), F32),
        compiler_params=_cparams(("arbitrary",)),
        name="moe",
    )(blk_e, nused, idx, x1, wg16, wu16, wd16, bg, bu, bd)


def _route(top_i):
    t = top_i.shape[0]
    n = t * TOP_K
    nblk = n // MOE_ROWS + N_EXPERTS
    flat_e = top_i.reshape(-1)
    order = jnp.argsort(flat_e, stable=True).astype(I32)
    experts = jnp.arange(N_EXPERTS, dtype=I32)
    counts = jnp.sum((flat_e[:, None] == experts[None, :]).astype(I32), axis=0)
    padded = ((counts + MOE_ROWS - 1) // MOE_ROWS) * MOE_ROWS
    pad_end = jnp.cumsum(padded)
    pad_start = pad_end - padded
    start = jnp.cumsum(counts) - counts
    nused = (pad_end[-1] // MOE_ROWS).astype(I32).reshape(1)
    blk_first = jnp.arange(nblk, dtype=I32) * MOE_ROWS
    blk_e = jnp.minimum(jnp.sum((blk_first[:, None] >= pad_end[None, :]).astype(I32), axis=1),
                        N_EXPERTS - 1)
    row = jnp.arange(MOE_ROWS, dtype=I32)[None, :]
    j = blk_first[:, None] + row - pad_start[blk_e][:, None]
    valid = j < counts[blk_e][:, None]
    a = order[jnp.clip(start[blk_e][:, None] + j, 0, n - 1)]
    tok = lax.shift_right_logical(a, TOP_SHIFT)
    src = jnp.where(valid, tok, 0)
    dump = n + (jnp.arange(nblk, dtype=I32)[:, None] % 2) * MOE_ROWS + row
    dst = jnp.where(valid, (a & (TOP_K - 1)) * t + tok, dump)
    return jnp.stack([src, dst], axis=1).astype(I32), blk_e.astype(I32), nused


def _final_kernel(x1_ref, gate_ref, y0_ref, y1_ref, y2_ref, y3_ref, g_ref, b_ref, o_ref):
    gates = gate_ref[...]
    y = gates[:, 0:1] * y0_ref[...]
    for k, ref in enumerate((y1_ref, y2_ref, y3_ref), start=1):
        y = y + gates[:, k:k + 1] * ref[...]
    o_ref[...] = _layer_norm(DEEPNORM_ALPHA * x1_ref[...] + y, g_ref[...], b_ref[...])


def _final(x1, gates, planes, ln_g, ln_b):
    t = x1.shape[0]
    nb = t // FINAL_ROWS
    row = lambda i: (i, 0)
    const = lambda i: (0, 0)
    plane = lambda k: pl.BlockSpec((FINAL_ROWS, D_MODEL), lambda i: (k * nb + i, 0))
    return pl.pallas_call(
        _final_kernel,
        grid=(nb,),
        in_specs=[pl.BlockSpec((FINAL_ROWS, D_MODEL), row), pl.BlockSpec((FINAL_ROWS, TOP_K), row),
                  plane(0), plane(1), plane(2), plane(3),
                  pl.BlockSpec((1, D_MODEL), const), pl.BlockSpec((1, D_MODEL), const)],
        out_specs=pl.BlockSpec((FINAL_ROWS, D_MODEL), row),
        out_shape=jax.ShapeDtypeStruct((t, D_MODEL), F32),
        compiler_params=_cparams(("parallel",)),
        name="final",
    )(x1, gates, planes, planes, planes, planes, ln_g.reshape(1, D_MODEL), ln_b.reshape(1, D_MODEL))


def kernel(x_prompt, x_sample, cache_a_k, cache_a_v, cache_b_k, cache_b_v, w_in, lambda_qk, subln_g,
           rel_bias, w_out, ln1_g, ln1_b, w_router, b_router, w_gate_up, b_gate_up, w_down, b_down,
           ln2_g, ln2_b):
    b, s, d = x_prompt.shape
    bd, sd, _ = x_sample.shape
    depth, _, past, _, _ = cache_a_k.shape
    lb = cache_b_k.shape[2]
    assert depth == 1 and b == 1 and d == D_MODEL
    assert sd == CHUNK and past % CHUNK == 0 and lb == BAND_PAST and s >= BAND_PAST
    assert s % ATT_TQ == 0 and s % BAND_TQ == 0
    ts = bd * sd
    t = s + ts
    lam_init = _lambda_init(0)

    x = jnp.concatenate([x_prompt.reshape(s, d), x_sample.reshape(ts, d)], axis=0)
    pos = jnp.concatenate([jnp.arange(s, dtype=I32),
                           jnp.tile(past + jnp.arange(sd, dtype=I32), bd)])
    lq = lambda_qk[0].astype(F32)
    lam = (jnp.exp(jnp.sum(lq[0] * lq[1])) - jnp.exp(jnp.sum(lq[2] * lq[3])) + lam_init).reshape(1)

    (qa16, ka32, ka16, va32, va16, vat16,
     qb16, kb32, kb16, vb32, vb16) = _project(x, w_in[0].astype(BF16), pos)

    oa_p = _attn_a_prompt(qa16, ka16, vat16, lam, subln_g[0], s, lam_init)
    oa_s = _attn_a_sample(qa16, ka16, va16, cache_a_k[0].reshape(bd, past, GROUP),
                          cache_a_v[0].reshape(bd, past, GROUP), lam, subln_g[0], s, lam_init)

    ob_p = _attn_b_prompt(qb16, kb16, vb16, _band_bias(rel_bias[0], BAND_TQ // CHUNK), s)
    ob_s = _attn_b_sample(qb16, kb16, vb16, cache_b_k[0].reshape(bd, lb, GROUP),
                          cache_b_v[0].reshape(bd, lb, GROUP), _band_bias(rel_bias[0], 1), s)

    wr16 = jnp.pad(w_router[0], ((0, 0), (0, LANES - N_EXPERTS))).astype(BF16)
    br = jnp.pad(b_router[0], (0, LANES - N_EXPERTS)).reshape(1, LANES)
    x1, top_i, gates = _merge(oa_p, ob_p, oa_s, ob_s, x, w_out[0].astype(BF16), ln1_g[0], ln1_b[0],
                              wr16, br)

    idx, blk_e, nused = _route(top_i)
    wgu = w_gate_up[0].reshape(N_EXPERTS, D_MODEL, D_MODEL, 2)
    bgu = b_gate_up[0].reshape(N_EXPERTS, 1, D_MODEL, 2)
    planes = _moe(x1, idx, blk_e, nused, wgu[..., 0].astype(BF16), wgu[..., 1].astype(BF16),
                  w_down[0].astype(BF16), bgu[..., 0], bgu[..., 1], b_down[0].reshape(N_EXPERTS, 1, D_MODEL))
    y = _final(x1, gates, planes, ln2_g[0], ln2_b[0])

    heads_a = lambda a, n: a.reshape(1, n, -1, N_COMP, D_HEAD_A)
    vals_a = lambda a, n: a.reshape(1, n, -1, H_A, 2 * D_HEAD_A)
    heads_b = lambda a, n: a.reshape(1, n, -1, H_B, D_HEAD_B)
    keep_s = min(BAND_PAST, lb + sd)
    kb_s = jnp.concatenate([cache_b_k[0], kb32[s:].reshape(bd, sd, H_B, D_HEAD_B)], axis=1)[:, lb + sd - keep_s:]
    vb_s = jnp.concatenate([cache_b_v[0], vb32[s:].reshape(bd, sd, H_B, D_HEAD_B)], axis=1)[:, lb + sd - keep_s:]
    return (y[:s].reshape(b, s, d), y[s:].reshape(bd, sd, d),
            heads_a(ka32[:s], b), vals_a(va32[:s], b),
            heads_b(kb32[s - BAND_PAST:s], b), heads_b(vb32[s - BAND_PAST:s], b),
            heads_a(ka32[s:], bd), vals_a(va32[s:], bd),
            kb_s[None], vb_s[None])
```

```python
import functools
import math

import jax
import jax.numpy as jnp
from jax import lax
from jax.experimental import pallas as pl
from jax.experimental.pallas import tpu as pltpu

F32 = jnp.float32
BF16 = jnp.bfloat16
I32 = jnp.int32

D_MODEL = 1024
CHUNK = 64
D_HEAD_A = 64
H_A = 4
N_COMP = 2 * H_A
ROPE_DIM = D_HEAD_A // 4
ROPE_THETA = 500000.0
D_HEAD_B = 64
H_B = 8
BAND_CHUNKS = 8
BAND_PAST = BAND_CHUNKS * CHUNK
REL_CLIP = 128
GROUP = 512
N_EXPERTS = 32
TOP_K = 4
TOP_SHIFT = 2
SWIGLU_LIMIT = 7.0
SWIGLU_ALPHA = 1.702
LN_EPS = 1e-5
SUBLN_EPS = 1e-5
DEPTH = 1
DEEPNORM_ALPHA = (2.0 * DEPTH) ** 0.25
NEG_INF = -1e30
LANES = 128
PAIR = 2 * D_HEAD_A
TT = D_MODEL // LANES
VT_ROWS = PAIR + 16
LOG2_E = math.log2(math.e)

VMEM_LIMIT = 56 * 1024 * 1024

PROJ_ROWS = 256
ATT_TQ = 512
ATT_TK = 1024
BAND_TQ = 256
MERGE_ROWS = 256
MOE_ROWS = 256
FINAL_ROWS = 256


def _lambda_init(layer_idx):
    return 0.8 - 0.6 * math.exp(-0.3 * layer_idx)


def _cparams(sem):
    return pltpu.CompilerParams(dimension_semantics=sem, vmem_limit_bytes=VMEM_LIMIT)


def _half_mask(shape, upper):
    lane = lax.broadcasted_iota(I32, shape, len(shape) - 1)
    return (lane >= D_HEAD_A) if upper else (lane < D_HEAD_A)


def _proj_kernel(x_ref, w_ref, cos_ref, s1_ref, s2_ref,
                 qa_ref, ka32_ref, ka16_ref, va32_ref, va16_ref, vat_ref,
                 qb_ref, kb32_ref, kb16_ref, vb32_ref, vb16_ref):
    xb = x_ref[...].astype(BF16)

    def group(g):
        return jnp.dot(xb, w_ref[:, g * GROUP:(g + 1) * GROUP], preferred_element_type=F32)

    cos = cos_ref[...]
    s1 = s1_ref[...]
    s2 = s2_ref[...]

    def rope(h):
        parts = []
        for j in range(GROUP // LANES):
            hj = h[:, j * LANES:(j + 1) * LANES]
            nxt = pltpu.roll(hj, LANES - ROPE_DIM // 2, 1)
            prv = pltpu.roll(hj, ROPE_DIM // 2, 1)
            parts.append(hj * cos + nxt * s1 + prv * s2)
        return jnp.concatenate(parts, axis=1)

    qa = rope(group(0)) * (D_HEAD_A ** -0.5 * LOG2_E)
    qa_ref[...] = qa.astype(BF16)
    ka = rope(group(1))
    ka32_ref[...] = ka
    ka16_ref[...] = ka.astype(BF16)
    va = group(2)
    va32_ref[...] = va
    va16_ref[...] = va.astype(BF16)
    vt = va.T
    rows = vt.shape[1]
    tail = jnp.concatenate([jnp.ones((1, rows), F32), jnp.zeros((VT_ROWS - PAIR - 1, rows), F32)], axis=0)
    parts = []
    for h in range(H_A):
        parts += [vt[h * PAIR:(h + 1) * PAIR], tail]
    vat_ref[...] = jnp.concatenate(parts, axis=0).astype(BF16)
    qb_ref[...] = (group(3) * (D_HEAD_B ** -0.5)).astype(BF16)
    kb = group(4)
    kb32_ref[...] = kb
    kb16_ref[...] = kb.astype(BF16)
    vb = group(5)
    vb32_ref[...] = vb
    vb16_ref[...] = vb.astype(BF16)


def _rope_tables(pos):
    half = ROPE_DIM // 2
    inv = ROPE_THETA ** (-jnp.arange(half, dtype=F32) * 2.0 / ROPE_DIM)
    ang = pos.astype(F32)[:, None] * jnp.tile(inv, LANES // half)[None, :]
    in_head = (jnp.arange(LANES, dtype=I32) % D_HEAD_A)[None, :]
    cos = jnp.where(in_head < ROPE_DIM, jnp.cos(ang), 1.0)
    sin = jnp.sin(ang)
    s1 = jnp.where(in_head < half, -sin, 0.0)
    s2 = jnp.where((in_head >= half) & (in_head < ROPE_DIM), sin, 0.0)
    return cos, s1, s2


def _project(x, w16, pos):
    t = x.shape[0]
    cos, s1, s2 = _rope_tables(pos)
    row = lambda i: (i, 0)
    blk = lambda cols: pl.BlockSpec((PROJ_ROWS, cols), row)
    f32o = jax.ShapeDtypeStruct((t, GROUP), F32)
    b16o = jax.ShapeDtypeStruct((t, GROUP), BF16)
    return pl.pallas_call(
        _proj_kernel,
        grid=(t // PROJ_ROWS,),
        in_specs=[blk(D_MODEL), pl.BlockSpec(w16.shape, lambda i: (0, 0)),
                  blk(LANES), blk(LANES), blk(LANES)],
        out_specs=[blk(GROUP), blk(GROUP), blk(GROUP), blk(GROUP), blk(GROUP),
                   pl.BlockSpec((H_A * VT_ROWS, PROJ_ROWS), lambda i: (0, i)),
                   blk(GROUP), blk(GROUP), blk(GROUP), blk(GROUP), blk(GROUP)],
        out_shape=[b16o, f32o, b16o, f32o, b16o, jax.ShapeDtypeStruct((H_A * VT_ROWS, t), BF16),
                   b16o, f32o, b16o, f32o, b16o],
        compiler_params=_cparams(("parallel",)),
        name="proj",
    )(x, w16, cos, s1, s2)


def _subln(o, g, lam_init, axis):
    o = o * lax.rsqrt(jnp.mean(jnp.square(o), axis=axis, keepdims=True) + SUBLN_EPS)
    return o * g * (1.0 - lam_init)


def _half_only(qp, upper):
    return jnp.where(_half_mask(qp.shape, upper), qp, jnp.zeros_like(qp))


def _attn_a_prompt_kernel(qi_ref, kj_ref, lam_ref, q_ref, k_ref, vt_ref, dmask_ref, g_ref, o_ref,
                          m_ref, acc_ref, *, lam_init):
    s = pl.program_id(0)
    qi = qi_ref[s]
    kj = kj_ref[s]

    @pl.when(kj == 0)
    def _():
        m_ref[...] = jnp.full(m_ref.shape, -jnp.inf, F32)
        acc_ref[...] = jnp.zeros(acc_ref.shape, F32)

    def step(diagonal):
        def scores(c):
            h = c // 2
            kp = k_ref[:, h * PAIR:(h + 1) * PAIR]
            qz = _half_only(q_ref[:, h * PAIR:(h + 1) * PAIR], c % 2 == 1)
            st = lax.dot_general(kp, qz, (((1,), (1,)), ((), ())), preferred_element_type=F32)
            return st + dmask_ref[...] if diagonal else st

        st_next = scores(0)
        for c in range(N_COMP):
            st = st_next
            if c + 1 < N_COMP:
                st_next = scores(c + 1)
            m_old = m_ref[c:c + 1, :]
            m_new = jnp.maximum(m_old, jnp.max(st, axis=0, keepdims=True))
            alpha = jnp.exp2(m_old - m_new)
            p = jnp.exp2(st - m_new).astype(BF16)
            h = c // 2
            pv = jnp.dot(vt_ref[h * VT_ROWS:(h + 1) * VT_ROWS, :], p, preferred_element_type=F32)
            acc_ref[c] = alpha * acc_ref[c] + pv
            m_ref[c:c + 1, :] = m_new

    last_kj = qi // (ATT_TK // ATT_TQ)

    @pl.when(kj < last_kj)
    def _():
        step(False)

    @pl.when(kj == last_kj)
    def _():
        step(True)
        lam = lam_ref[0]
        for h in range(H_A):
            a0 = acc_ref[2 * h]
            a1 = acc_ref[2 * h + 1]
            o0 = a0[:PAIR] / a0[PAIR:PAIR + 1]
            o1 = a1[:PAIR] / a1[PAIR:PAIR + 1]
            ot = _subln(o0 - lam * o1, g_ref[...], lam_init, 0)
            o_ref[:, h * PAIR:(h + 1) * PAIR] = ot.T.astype(BF16)


def _attn_a_prompt(qa16, ka16, vat16, lam, subln_g, s_len, lam_init):
    ratio = ATT_TK // ATT_TQ
    assert ATT_TK == ratio * ATT_TQ and s_len % ATT_TK == 0
    kchunk = jnp.arange(ATT_TK, dtype=I32)[None, :, None] // CHUNK
    qchunk = (jnp.arange(ratio, dtype=I32)[:, None, None] * ATT_TQ
              + jnp.arange(ATT_TQ, dtype=I32)[None, None, :]) // CHUNK
    dmask = jnp.where(kchunk <= qchunk, 0.0, NEG_INF).astype(F32)
    nq = s_len // ATT_TQ
    qi_tab, kj_tab = [], []
    for i in range(nq):
        for j in range(i // ratio + 1):
            qi_tab.append(i)
            kj_tab.append(j)
    qi_tab = jnp.asarray(qi_tab, I32)
    kj_tab = jnp.asarray(kj_tab, I32)
    grid_spec = pltpu.PrefetchScalarGridSpec(
        num_scalar_prefetch=2,
        grid=(int(qi_tab.shape[0]),),
        in_specs=[
            pl.BlockSpec(memory_space=pltpu.SMEM),
            pl.BlockSpec((ATT_TQ, GROUP), lambda s, qi, kj: (qi[s], 0)),
            pl.BlockSpec((ATT_TK, GROUP), lambda s, qi, kj: (kj[s], 0)),
            pl.BlockSpec((H_A * VT_ROWS, ATT_TK), lambda s, qi, kj: (0, kj[s])),
            pl.BlockSpec((None, ATT_TK, ATT_TQ), lambda s, qi, kj: (qi[s] % ratio, 0, 0)),
            pl.BlockSpec((PAIR, 1), lambda s, qi, kj: (0, 0)),
        ],
        out_specs=pl.BlockSpec((ATT_TQ, GROUP), lambda s, qi, kj: (qi[s], 0)),
        scratch_shapes=[pltpu.VMEM((N_COMP, ATT_TQ), F32),
                        pltpu.VMEM((N_COMP, VT_ROWS, ATT_TQ), F32)],
    )
    return pl.pallas_call(
        functools.partial(_attn_a_prompt_kernel, lam_init=lam_init),
        grid_spec=grid_spec,
        out_shape=jax.ShapeDtypeStruct((s_len, GROUP), BF16),
        compiler_params=_cparams(("arbitrary",)),
        name="attn_a_prompt",
    )(qi_tab, kj_tab, lam, qa16, ka16, vat16, dmask, subln_g.reshape(PAIR, 1))


def _attend(qz, segs, exp=jnp.exp):
    scores = []
    for k, _, bias, mask in segs:
        sc = lax.dot_general(qz, k, (((1,), (1,)), ((), ())), preferred_element_type=F32)
        if bias is not None:
            sc = sc + bias
        if mask is not None:
            sc = jnp.where(mask, sc, NEG_INF)
        scores.append(sc)
    m = functools.reduce(jnp.maximum, [jnp.max(sc, axis=1, keepdims=True) for sc in scores])
    l = None
    o = None
    for sc, (_, v, _, _) in zip(scores, segs):
        p = exp(sc - m)
        ls = jnp.sum(p, axis=1, keepdims=True)
        os_ = jnp.dot(p.astype(BF16), v, preferred_element_type=F32)
        l = ls if l is None else l + ls
        o = os_ if o is None else o + os_
    return o / l


def _attn_a_sample_kernel(lam_ref, q_ref, kn_ref, vn_ref, kc_ref, vc_ref, g_ref, o_ref, *, lam_init):
    lam = lam_ref[0]
    for h in range(H_A):
        cols = slice(h * PAIR, (h + 1) * PAIR)
        segs = [(kc_ref[:, cols].astype(BF16), vc_ref[:, cols].astype(BF16), None, None),
                (kn_ref[:, cols], vn_ref[:, cols], None, None)]
        qp = q_ref[:, cols]
        o0 = _attend(_half_only(qp, False), segs, jnp.exp2)
        o1 = _attend(_half_only(qp, True), segs, jnp.exp2)
        o = _subln(o0 - lam * o1, g_ref[...], lam_init, 1)
        o_ref[:, cols] = o.astype(BF16)


def _attn_a_sample(qa16, ka16, va16, cache_k, cache_v, lam, subln_g, row0, lam_init):
    nb, past, _ = cache_k.shape
    sd = CHUNK
    new = lambda b: (row0 // sd + b, 0)
    return pl.pallas_call(
        functools.partial(_attn_a_sample_kernel, lam_init=lam_init),
        grid=(nb,),
        in_specs=[
            pl.BlockSpec(memory_space=pltpu.SMEM),
            pl.BlockSpec((sd, GROUP), new), pl.BlockSpec((sd, GROUP), new), pl.BlockSpec((sd, GROUP), new),
            pl.BlockSpec((None, past, GROUP), lambda b: (b, 0, 0)),
            pl.BlockSpec((None, past, GROUP), lambda b: (b, 0, 0)),
            pl.BlockSpec((1, PAIR), lambda b: (0, 0)),
        ],
        out_specs=pl.BlockSpec((sd, GROUP), lambda b: (b, 0)),
        out_shape=jax.ShapeDtypeStruct((nb * sd, GROUP), BF16),
        compiler_params=_cparams(("parallel",)),
        name="attn_a_sample",
    )(lam, qa16, ka16, va16, cache_k, cache_v, subln_g.reshape(1, PAIR))


def _band_bias(rel_bias, n_q_chunks):
    nq = n_q_chunks * CHUNK
    nk = nq + BAND_PAST
    heads = rel_bias.shape[0]
    n_lo = BAND_PAST - REL_CLIP
    period = max(nq + nk, n_lo + (2 * REL_CLIP + 1) + nq - 1)
    n_hi = period - (nq - 1) - n_lo - (2 * REL_CLIP + 1)
    assert n_lo >= 0
    lo = jnp.broadcast_to(rel_bias[:, :1], (heads, n_lo))
    hi = jnp.broadcast_to(rel_bias[:, -1:], (heads, n_hi))
    wrap = jnp.broadcast_to(rel_bias[:, :1], (heads, nq - 1))
    e = jnp.concatenate([lo, rel_bias, hi, wrap], axis=1).astype(F32)
    flat = jnp.broadcast_to(e[:, None, :], (heads, nq, period)).reshape(heads, nq * period)
    bias = flat[:, :nq * (period - 1)].reshape(heads, nq, period - 1)[:, :, :nk]
    qpos = jnp.arange(nq, dtype=I32)[:, None]
    kpos = jnp.arange(nk, dtype=I32)[None, :] - BAND_PAST
    dch = qpos // CHUNK - jnp.floor_divide(kpos, CHUNK)
    ok = (dch >= 0) & (dch <= BAND_CHUNKS)
    return jnp.where(ok[None], bias, NEG_INF)


def _band_heads(q_ref, seg_refs, bias_ref, o_ref):
    for j in range(H_B // 2):
        cols = slice(j * PAIR, (j + 1) * PAIR)
        qp = q_ref[:, cols]
        res = []
        for half in range(2):
            segs = []
            for k_ref, v_ref, col0, mask in seg_refs:
                n = k_ref.shape[0]
                segs.append((k_ref[:, cols].astype(BF16), v_ref[:, cols].astype(BF16),
                             bias_ref[2 * j + half, :, col0:col0 + n], mask))
            res.append(_attend(_half_only(qp, half == 1), segs))
        o = jnp.where(_half_mask(res[0].shape, False), res[0], res[1])
        o_ref[:, cols] = o.astype(BF16)


def _attn_b_prompt_kernel(q_ref, k0_ref, k1_ref, k2_ref, v0_ref, v1_ref, v2_ref, bias_ref, o_ref):
    i = pl.program_id(0)
    kpos = lax.broadcasted_iota(I32, (BAND_TQ, BAND_TQ), 1) + (i * BAND_TQ - BAND_PAST)
    seg_refs = [(k0_ref, v0_ref, 0, kpos >= 0),
                (k1_ref, v1_ref, BAND_TQ, kpos + BAND_TQ >= 0),
                (k2_ref, v2_ref, 2 * BAND_TQ, None)]
    _band_heads(q_ref, seg_refs, bias_ref, o_ref)


def _attn_b_prompt(qb16, kb16, vb16, bias, s_len):
    assert BAND_PAST == 2 * BAND_TQ
    cur = lambda i: (i, 0)
    p1 = lambda i: (jnp.maximum(i - 1, 0), 0)
    p2 = lambda i: (jnp.maximum(i - 2, 0), 0)
    blk = lambda f: pl.BlockSpec((BAND_TQ, GROUP), f)
    return pl.pallas_call(
        _attn_b_prompt_kernel,
        grid=(s_len // BAND_TQ,),
        in_specs=[blk(cur), blk(p2), blk(p1), blk(cur), blk(p2), blk(p1), blk(cur),
                  pl.BlockSpec(bias.shape, lambda i: (0, 0, 0))],
        out_specs=blk(cur),
        out_shape=jax.ShapeDtypeStruct((s_len, GROUP), BF16),
        compiler_params=_cparams(("parallel",)),
        name="attn_b_prompt",
    )(qb16, kb16, kb16, kb16, vb16, vb16, vb16, bias)


def _attn_b_sample_kernel(q_ref, kn_ref, vn_ref, kc_ref, vc_ref, bias_ref, o_ref):
    seg_refs = [(kc_ref, vc_ref, 0, None), (kn_ref, vn_ref, kc_ref.shape[0], None)]
    _band_heads(q_ref, seg_refs, bias_ref, o_ref)


def _attn_b_sample(qb16, kb16, vb16, cache_k, cache_v, bias, row0):
    nb, lb, _ = cache_k.shape
    sd = CHUNK
    new = lambda b: (row0 // sd + b, 0)
    return pl.pallas_call(
        _attn_b_sample_kernel,
        grid=(nb,),
        in_specs=[
            pl.BlockSpec((sd, GROUP), new), pl.BlockSpec((sd, GROUP), new), pl.BlockSpec((sd, GROUP), new),
            pl.BlockSpec((None, lb, GROUP), lambda b: (b, 0, 0)),
            pl.BlockSpec((None, lb, GROUP), lambda b: (b, 0, 0)),
            pl.BlockSpec(bias.shape, lambda b: (0, 0, 0)),
        ],
        out_specs=pl.BlockSpec((sd, GROUP), lambda b: (b, 0)),
        out_shape=jax.ShapeDtypeStruct((nb * sd, GROUP), BF16),
        compiler_params=_cparams(("parallel",)),
        name="attn_b_sample",
    )(qb16, kb16, vb16, cache_k, cache_v, bias)


def _layer_norm(z, g, b):
    mu = jnp.mean(z, axis=-1, keepdims=True)
    var = jnp.mean(jnp.square(z - mu), axis=-1, keepdims=True)
    return (z - mu) * lax.rsqrt(var + LN_EPS) * g + b


def _to_token_tiles(ref, x):
    rows = x.shape[0]
    for j in range(TT):
        ref[pl.ds(j, rows, stride=TT), :] = x[:, j * LANES:(j + 1) * LANES]


def _from_token_tiles(ref, rows):
    return jnp.concatenate([ref[pl.ds(j, rows, stride=TT), :] for j in range(TT)], axis=1)


def _merge_kernel(oap_ref, obp_ref, oas_ref, obs_ref, xp_ref, xs_ref, wo_ref, g_ref, b_ref, wr_ref,
                  br_ref, x1_ref, ti_ref, gate_ref, *, n_prompt_blocks):
    is_prompt = pl.program_id(0) < n_prompt_blocks
    oa = jnp.where(is_prompt, oap_ref[...], oas_ref[...])
    ob = jnp.where(is_prompt, obp_ref[...], obs_ref[...])
    x = jnp.where(is_prompt, xp_ref[...], xs_ref[...])
    mix = jnp.dot(oa, wo_ref[:GROUP, :], preferred_element_type=F32)
    mix = mix + jnp.dot(ob, wo_ref[GROUP:, :], preferred_element_type=F32)
    x1 = _layer_norm(DEEPNORM_ALPHA * x + mix, g_ref[...], b_ref[...])
    _to_token_tiles(x1_ref, x1)
    logits = jnp.dot(x1.astype(BF16), wr_ref[...], preferred_element_type=F32) + br_ref[...]
    lane = lax.broadcasted_iota(I32, logits.shape, 1)
    logits = jnp.where(lane < N_EXPERTS, logits, -jnp.inf)
    vals, idxs = [], []
    for _ in range(TOP_K):
        mx = jnp.max(logits, axis=1, keepdims=True)
        ix = jnp.min(jnp.where(logits == mx, lane, LANES), axis=1, keepdims=True)
        vals.append(mx)
        idxs.append(ix)
        logits = jnp.where(lane == ix, -jnp.inf, logits)
    col = lax.broadcasted_iota(I32, ti_ref.shape, 1)
    top_v = jnp.broadcast_to(vals[-1], ti_ref.shape)
    top_i = jnp.broadcast_to(idxs[-1], ti_ref.shape)
    for k in range(TOP_K - 1):
        top_v = jnp.where(col == k, vals[k], top_v)
        top_i = jnp.where(col == k, idxs[k], top_i)
    ti_ref[...] = top_i
    e = jnp.exp(top_v - vals[0])
    gate_ref[...] = e / jnp.sum(e, axis=1, keepdims=True)


def _merge(oa_p, ob_p, oa_s, ob_s, x_p, x_s, wo16, ln_g, ln_b, wr16, br):
    t = x_p.shape[0] + x_s.shape[0]
    npb = x_p.shape[0] // MERGE_ROWS
    row = lambda i: (i, 0)
    const = lambda i: (0, 0)
    blk = lambda cols: pl.BlockSpec((MERGE_ROWS, cols), row)
    prompt = lambda cols: pl.BlockSpec((MERGE_ROWS, cols), lambda i: (jnp.minimum(i, npb - 1), 0))
    sample = lambda cols: pl.BlockSpec((MERGE_ROWS, cols), lambda i: (jnp.maximum(i - npb, 0), 0))
    return pl.pallas_call(
        functools.partial(_merge_kernel, n_prompt_blocks=npb),
        grid=(t // MERGE_ROWS,),
        in_specs=[prompt(GROUP), prompt(GROUP), sample(GROUP), sample(GROUP),
                  prompt(D_MODEL), sample(D_MODEL),
                  pl.BlockSpec(wo16.shape, const), pl.BlockSpec((1, D_MODEL), const),
                  pl.BlockSpec((1, D_MODEL), const), pl.BlockSpec(wr16.shape, const),
                  pl.BlockSpec((1, LANES), const)],
        out_specs=[pl.BlockSpec((MERGE_ROWS * TT, LANES), row), blk(TOP_K), blk(TOP_K)],
        out_shape=[jax.ShapeDtypeStruct((t * TT, LANES), F32), jax.ShapeDtypeStruct((t, TOP_K), I32),
                   jax.ShapeDtypeStruct((t, TOP_K), F32)],
        compiler_params=_cparams(("parallel",)),
        name="merge",
    )(oa_p, ob_p, oa_s, ob_s, x_p, x_s, wo16, ln_g.reshape(1, D_MODEL), ln_b.reshape(1, D_MODEL),
      wr16, br)


def _moe_kernel(blk_e_ref, nused_ref, idx_hbm, x_hbm, wg_ref, wu_ref, wd_ref, bg_ref, bu_ref, bd_ref,
                out_hbm, idx_smem, xbuf, ybuf, isem, gsem, ssem, *, n_tok):
    del blk_e_ref
    i = pl.program_id(0)
    nused = nused_ref[0]
    n_assign = n_tok * TOP_K
    last = nused - 1
    par = lax.rem(i, 2)

    def idx_copy(block, slot):
        return pltpu.make_async_copy(idx_hbm.at[block], idx_smem.at[slot], isem.at[slot])

    def start_gather(islot, xslot):
        for r in range(MOE_ROWS):
            src = pl.multiple_of(idx_smem[islot, 0, r], TT)
            pltpu.make_async_copy(x_hbm.at[pl.ds(src, TT)], xbuf.at[xslot, pl.ds(r * TT, TT)],
                                  gsem.at[xslot]).start()

    def wait_gather(xslot):
        pltpu.make_async_copy(x_hbm.at[pl.ds(0, MOE_ROWS * TT)], xbuf.at[xslot], gsem.at[xslot]).wait()

    def start_scatter(islot, yslot):
        for r in range(MOE_ROWS):
            dst = pl.multiple_of(idx_smem[islot, 1, r], TT)
            pltpu.make_async_copy(ybuf.at[yslot, pl.ds(r * TT, TT)], out_hbm.at[pl.ds(dst, TT)],
                                  ssem.at[yslot]).start()

    def slot_flush(yslot, row0):
        return pltpu.make_async_copy(ybuf.at[yslot], out_hbm.at[pl.ds(row0 * TT, MOE_ROWS * TT)],
                                     ssem.at[yslot])

    @pl.when(i == 0)
    def _():
        ybuf[...] = jnp.zeros(ybuf.shape, F32)
        for p in range(2):
            slot_flush(p, n_assign + p * MOE_ROWS).start()
        idx_copy(0, 0).start()
        idx_copy(0, 0).wait()
        start_gather(0, 0)
        idx_copy(jnp.minimum(1, last), 1).start()

    @pl.when(i < nused)
    def _():
        s_cur = lax.rem(i, 3)
        s_nxt = lax.rem(i + 1, 3)
        s_nn = lax.rem(i + 2, 3)
        idx_copy(0, s_nxt).wait()
        start_gather(s_nxt, 1 - par)
        idx_copy(jnp.minimum(i + 2, last), s_nn).start()

        wait_gather(par)
        x = _from_token_tiles(xbuf.at[par], MOE_ROWS).astype(BF16)
        hg = jnp.dot(x, wg_ref[...], preferred_element_type=F32) + bg_ref[...]
        hu = jnp.dot(x, wu_ref[...], preferred_element_type=F32) + bu_ref[...]
        gate = jnp.minimum(hg, SWIGLU_LIMIT)
        up = jnp.clip(hu, -SWIGLU_LIMIT, SWIGLU_LIMIT)
        glu = gate * (1.0 / (1.0 + jnp.exp(-(gate * SWIGLU_ALPHA))))
        act = ((up + 1.0) * glu).astype(BF16)
        y = jnp.dot(act, wd_ref[...], preferred_element_type=F32) + bd_ref[...]

        slot_flush(par, 0).wait()
        _to_token_tiles(ybuf.at[par], y)
        start_scatter(s_cur, par)

        @pl.when(i == last)
        def _():
            slot_flush(par, 0).wait()
            slot_flush(1 - par, 0).wait()
            wait_gather(1 - par)
            idx_copy(0, s_nn).wait()


def _moe(x1, idx, blk_e, nused, wg16, wu16, wd16, bg, bu, bd):
    t = x1.shape[0] // TT
    nblk = idx.shape[0]
    wmap = lambda i, be, nu: (be[i], 0, 0)
    grid_spec = pltpu.PrefetchScalarGridSpec(
        num_scalar_prefetch=2,
        grid=(nblk,),
        in_specs=[
            pl.BlockSpec(memory_space=pl.ANY),
            pl.BlockSpec(memory_space=pl.ANY),
            pl.BlockSpec((None, D_MODEL, D_MODEL), wmap),
            pl.BlockSpec((None, D_MODEL, D_MODEL), wmap),
            pl.BlockSpec((None, D_MODEL, D_MODEL), wmap),
            pl.BlockSpec((None, 1, D_MODEL), wmap),
            pl.BlockSpec((None, 1, D_MODEL), wmap),
            pl.BlockSpec((None, 1, D_MODEL), wmap),
        ],
        out_specs=pl.BlockSpec(memory_space=pl.ANY),
        scratch_shapes=[
            pltpu.SMEM((3, 2, MOE_ROWS), I32),
            pltpu.VMEM((2, MOE_ROWS * TT, LANES), F32),
            pltpu.VMEM((2, MOE_ROWS * TT, LANES), F32),
            pltpu.SemaphoreType.DMA((3,)),
            pltpu.SemaphoreType.DMA((2,)),
            pltpu.SemaphoreType.DMA((2,)),
        ],
    )
    return pl.pallas_call(
        functools.partial(_moe_kernel, n_tok=t),
        grid_spec=grid_spec,
        out_shape=jax.ShapeDtypeStruct(((t * TOP_K + 2 * MOE_ROWS) * TT, LANES), F32),
        compiler_params=_cparams(("arbitrary",)),
        name="moe",
    )(blk_e, nused, idx, x1, wg16, wu16, wd16, bg, bu, bd)


def _route(top_i):
    t = top_i.shape[0]
    n = t * TOP_K
    nblk = n // MOE_ROWS + N_EXPERTS
    flat_e = top_i.reshape(-1)
    order = jnp.argsort(flat_e, stable=True).astype(I32)
    experts = jnp.arange(N_EXPERTS, dtype=I32)
    counts = jnp.sum((flat_e[:, None] == experts[None, :]).astype(I32), axis=0)
    padded = ((counts + MOE_ROWS - 1) // MOE_ROWS) * MOE_ROWS
    pad_end = jnp.cumsum(padded)
    pad_start = pad_end - padded
    start = jnp.cumsum(counts) - counts
    nused = (pad_end[-1] // MOE_ROWS).astype(I32).reshape(1)
    blk_first = jnp.arange(nblk, dtype=I32) * MOE_ROWS
    blk_e = jnp.minimum(jnp.sum((blk_first[:, None] >= pad_end[None, :]).astype(I32), axis=1),
                        N_EXPERTS - 1)
    row = jnp.arange(MOE_ROWS, dtype=I32)[None, :]
    j = blk_first[:, None] + row - pad_start[blk_e][:, None]
    valid = j < counts[blk_e][:, None]
    a = order[jnp.clip(start[blk_e][:, None] + j, 0, n - 1)]
    tok = lax.shift_right_logical(a, TOP_SHIFT)
    src = jnp.where(valid, tok, 0)
    dump = n + (jnp.arange(nblk, dtype=I32)[:, None] % 2) * MOE_ROWS + row
    dst = jnp.where(valid, (a & (TOP_K - 1)) * t + tok, dump)
    return jnp.stack([src * TT, dst * TT], axis=1).astype(I32), blk_e.astype(I32), nused


def _final_kernel(x1_ref, gate_ref, y0_ref, y1_ref, y2_ref, y3_ref, g_ref, b_ref, o_ref):
    gates = gate_ref[...]
    y = gates[:, 0:1] * _from_token_tiles(y0_ref, FINAL_ROWS)
    for k, ref in enumerate((y1_ref, y2_ref, y3_ref), start=1):
        y = y + gates[:, k:k + 1] * _from_token_tiles(ref, FINAL_ROWS)
    x1 = _from_token_tiles(x1_ref, FINAL_ROWS)
    o_ref[...] = _layer_norm(DEEPNORM_ALPHA * x1 + y, g_ref[...], b_ref[...])


def _final(x1, gates, planes, ln_g, ln_b, row0, rows):
    nb_all = gates.shape[0] // FINAL_ROWS
    b0 = row0 // FINAL_ROWS
    row = lambda i: (b0 + i, 0)
    const = lambda i: (0, 0)
    tiles = lambda f: pl.BlockSpec((FINAL_ROWS * TT, LANES), f)
    plane = lambda k: tiles(lambda i: (k * nb_all + b0 + i, 0))
    return pl.pallas_call(
        _final_kernel,
        grid=(rows // FINAL_ROWS,),
        in_specs=[tiles(row), pl.BlockSpec((FINAL_ROWS, TOP_K), row),
                  plane(0), plane(1), plane(2), plane(3),
                  pl.BlockSpec((1, D_MODEL), const), pl.BlockSpec((1, D_MODEL), const)],
        out_specs=pl.BlockSpec((FINAL_ROWS, D_MODEL), lambda i: (i, 0)),
        out_shape=jax.ShapeDtypeStruct((rows, D_MODEL), F32),
        compiler_params=_cparams(("parallel",)),
        name="final",
    )(x1, gates, planes, planes, planes, planes, ln_g.reshape(1, D_MODEL), ln_b.reshape(1, D_MODEL))


def kernel(x_prompt, x_sample, cache_a_k, cache_a_v, cache_b_k, cache_b_v, w_in, lambda_qk, subln_g,
           rel_bias, w_out, ln1_g, ln1_b, w_router, b_router, w_gate_up, b_gate_up, w_down, b_down,
           ln2_g, ln2_b):
    b, s, d = x_prompt.shape
    bd, sd, _ = x_sample.shape
    depth, _, past, _, _ = cache_a_k.shape
    lb = cache_b_k.shape[2]
    assert depth == 1 and b == 1 and d == D_MODEL
    assert sd == CHUNK and past % CHUNK == 0 and lb == BAND_PAST and s >= BAND_PAST
    assert s % ATT_TQ == 0 and s % BAND_TQ == 0
    ts = bd * sd
    t = s + ts
    lam_init = _lambda_init(0)

    x_p = x_prompt.reshape(s, d)
    x_s = x_sample.reshape(ts, d)
    pos_p = jnp.arange(s, dtype=I32)
    pos_s = jnp.tile(past + jnp.arange(sd, dtype=I32), bd)
    lq = lambda_qk[0].astype(F32)
    lam = (jnp.exp(jnp.sum(lq[0] * lq[1])) - jnp.exp(jnp.sum(lq[2] * lq[3])) + lam_init).reshape(1)

    w_in16 = w_in[0].astype(BF16)
    (qa_p, ka32_p, ka_p, va32_p, _, vat_p, qb_p, kb32_p, kb_p, vb32_p, vb_p) = _project(x_p, w_in16, pos_p)
    (qa_s, ka32_s, ka_s, va32_s, va_s, _, qb_s, kb32_s, kb_s, vb32_s, vb_s) = _project(x_s, w_in16, pos_s)

    oa_p = _attn_a_prompt(qa_p, ka_p, vat_p, lam, subln_g[0], s, lam_init)
    oa_s = _attn_a_sample(qa_s, ka_s, va_s, cache_a_k[0].reshape(bd, past, GROUP).astype(BF16),
                          cache_a_v[0].reshape(bd, past, GROUP).astype(BF16), lam, subln_g[0], 0, lam_init)

    ob_p = _attn_b_prompt(qb_p, kb_p, vb_p, _band_bias(rel_bias[0], BAND_TQ // CHUNK), s)
    ob_s = _attn_b_sample(qb_s, kb_s, vb_s, cache_b_k[0].reshape(bd, lb, GROUP).astype(BF16),
                          cache_b_v[0].reshape(bd, lb, GROUP).astype(BF16), _band_bias(rel_bias[0], 1), 0)

    wr16 = jnp.pad(w_router[0], ((0, 0), (0, LANES - N_EXPERTS))).astype(BF16)
    br = jnp.pad(b_router[0], (0, LANES - N_EXPERTS)).reshape(1, LANES)
    x1, top_i, gates = _merge(oa_p, ob_p, oa_s, ob_s, x_p, x_s, w_out[0].astype(BF16), ln1_g[0],
                              ln1_b[0], wr16, br)

    idx, blk_e, nused = _route(top_i)
    wgu = w_gate_up[0].reshape(N_EXPERTS, D_MODEL, D_MODEL, 2)
    bgu = b_gate_up[0].reshape(N_EXPERTS, 1, D_MODEL, 2)
    planes = _moe(x1, idx, blk_e, nused, wgu[..., 0].astype(BF16), wgu[..., 1].astype(BF16),
                  w_down[0].astype(BF16), bgu[..., 0], bgu[..., 1], b_down[0].reshape(N_EXPERTS, 1, D_MODEL))
    y_p = _final(x1, gates, planes, ln2_g[0], ln2_b[0], 0, s)
    y_s = _final(x1, gates, planes, ln2_g[0], ln2_b[0], s, ts)

    heads_a = lambda a, n: a.reshape(1, n, -1, N_COMP, D_HEAD_A)
    vals_a = lambda a, n: a.reshape(1, n, -1, H_A, 2 * D_HEAD_A)
    heads_b = lambda a, n: a.reshape(1, n, -1, H_B, D_HEAD_B)
    keep_s = min(BAND_PAST, lb + sd)
    kb_new = jnp.concatenate([cache_b_k[0], heads_b(kb32_s, bd)[0]], axis=1)[:, lb + sd - keep_s:]
    vb_new = jnp.concatenate([cache_b_v[0], heads_b(vb32_s, bd)[0]], axis=1)[:, lb + sd - keep_s:]
    return (y_p.reshape(b, s, d), y_s.reshape(bd, sd, d),
            heads_a(ka32_p, b), vals_a(va32_p, b),
            heads_b(kb32_p[s - BAND_PAST:], b), heads_b(vb32_p[s - BAND_PAST:], b),
            heads_a(ka32_s, bd), vals_a(va32_s, bd),
            kb_new[None], vb_new[None])
```

```python
import functools
import math

import jax
import jax.numpy as jnp
from jax import lax
from jax.experimental import pallas as pl
from jax.experimental.pallas import tpu as pltpu

F32 = jnp.float32
BF16 = jnp.bfloat16
I32 = jnp.int32

D_MODEL = 1024
CHUNK = 64
D_HEAD_A = 64
H_A = 4
N_COMP = 2 * H_A
ROPE_DIM = D_HEAD_A // 4
ROPE_THETA = 500000.0
D_HEAD_B = 64
H_B = 8
BAND_CHUNKS = 8
BAND_PAST = BAND_CHUNKS * CHUNK
REL_CLIP = 128
GROUP = 512
N_EXPERTS = 32
TOP_K = 4
TOP_SHIFT = 2
SWIGLU_LIMIT = 7.0
SWIGLU_ALPHA = 1.702
LN_EPS = 1e-5
SUBLN_EPS = 1e-5
DEPTH = 1
DEEPNORM_ALPHA = (2.0 * DEPTH) ** 0.25
NEG_INF = -1e30
LANES = 128
PAIR = 2 * D_HEAD_A
TT = D_MODEL // LANES
VT_ROWS = PAIR + 16
LOG2_E = math.log2(math.e)

VMEM_LIMIT = 56 * 1024 * 1024

PROJ_ROWS = 256
ATT_TQ = 512
ATT_TK = 1024
ATT_AHEAD = 2
BAND_TQ = 256
MERGE_ROWS = 256
MOE_ROWS = 256
FINAL_ROWS = 256


def _lambda_init(layer_idx):
    return 0.8 - 0.6 * math.exp(-0.3 * layer_idx)


def _cparams(sem):
    return pltpu.CompilerParams(dimension_semantics=sem, vmem_limit_bytes=VMEM_LIMIT)


def _half_mask(shape, upper):
    lane = lax.broadcasted_iota(I32, shape, len(shape) - 1)
    return (lane >= D_HEAD_A) if upper else (lane < D_HEAD_A)


def _proj_kernel(x_ref, w_ref, cos_ref, s1_ref, s2_ref,
                 qa_ref, ka32_ref, ka16_ref, va32_ref, va16_ref, vat_ref,
                 qb_ref, kb32_ref, kb16_ref, vb32_ref, vb16_ref):
    xb = x_ref[...].astype(BF16)

    def group(g):
        return jnp.dot(xb, w_ref[:, g * GROUP:(g + 1) * GROUP], preferred_element_type=F32)

    cos = cos_ref[...]
    s1 = s1_ref[...]
    s2 = s2_ref[...]

    def rope(h):
        parts = []
        for j in range(GROUP // LANES):
            hj = h[:, j * LANES:(j + 1) * LANES]
            nxt = pltpu.roll(hj, LANES - ROPE_DIM // 2, 1)
            prv = pltpu.roll(hj, ROPE_DIM // 2, 1)
            parts.append(hj * cos + nxt * s1 + prv * s2)
        return jnp.concatenate(parts, axis=1)

    qa = rope(group(0)) * (D_HEAD_A ** -0.5 * LOG2_E)
    qa_ref[...] = qa.astype(BF16)
    ka = rope(group(1))
    ka32_ref[...] = ka
    ka16_ref[...] = ka.astype(BF16)
    va = group(2)
    va32_ref[...] = va
    va16_ref[...] = va.astype(BF16)
    vt = va.T
    rows = vt.shape[1]
    tail = jnp.concatenate([jnp.ones((1, rows), F32), jnp.zeros((VT_ROWS - PAIR - 1, rows), F32)], axis=0)
    parts = []
    for h in range(H_A):
        parts += [vt[h * PAIR:(h + 1) * PAIR], tail]
    vat_ref[...] = jnp.concatenate(parts, axis=0).astype(BF16)
    qb_ref[...] = (group(3) * (D_HEAD_B ** -0.5)).astype(BF16)
    kb = group(4)
    kb32_ref[...] = kb
    kb16_ref[...] = kb.astype(BF16)
    vb = group(5)
    vb32_ref[...] = vb
    vb16_ref[...] = vb.astype(BF16)


def _rope_tables(pos):
    half = ROPE_DIM // 2
    inv = ROPE_THETA ** (-jnp.arange(half, dtype=F32) * 2.0 / ROPE_DIM)
    ang = pos.astype(F32)[:, None] * jnp.tile(inv, LANES // half)[None, :]
    in_head = (jnp.arange(LANES, dtype=I32) % D_HEAD_A)[None, :]
    cos = jnp.where(in_head < ROPE_DIM, jnp.cos(ang), 1.0)
    sin = jnp.sin(ang)
    s1 = jnp.where(in_head < half, -sin, 0.0)
    s2 = jnp.where((in_head >= half) & (in_head < ROPE_DIM), sin, 0.0)
    return cos, s1, s2


def _project(x, w16, pos):
    t = x.shape[0]
    cos, s1, s2 = _rope_tables(pos)
    row = lambda i: (i, 0)
    blk = lambda cols: pl.BlockSpec((PROJ_ROWS, cols), row)
    f32o = jax.ShapeDtypeStruct((t, GROUP), F32)
    b16o = jax.ShapeDtypeStruct((t, GROUP), BF16)
    return pl.pallas_call(
        _proj_kernel,
        grid=(t // PROJ_ROWS,),
        in_specs=[blk(D_MODEL), pl.BlockSpec(w16.shape, lambda i: (0, 0)),
                  blk(LANES), blk(LANES), blk(LANES)],
        out_specs=[blk(GROUP), blk(GROUP), blk(GROUP), blk(GROUP), blk(GROUP),
                   pl.BlockSpec((H_A * VT_ROWS, PROJ_ROWS), lambda i: (0, i)),
                   blk(GROUP), blk(GROUP), blk(GROUP), blk(GROUP), blk(GROUP)],
        out_shape=[b16o, f32o, b16o, f32o, b16o, jax.ShapeDtypeStruct((H_A * VT_ROWS, t), BF16),
                   b16o, f32o, b16o, f32o, b16o],
        compiler_params=_cparams(("parallel",)),
        name="proj",
    )(x, w16, cos, s1, s2)


def _subln(o, g, lam_init, axis):
    o = o * lax.rsqrt(jnp.mean(jnp.square(o), axis=axis, keepdims=True) + SUBLN_EPS)
    return o * g * (1.0 - lam_init)


def _half_only(qp, upper):
    return jnp.where(_half_mask(qp.shape, upper), qp, jnp.zeros_like(qp))


def _attn_a_prompt_kernel(qi_ref, kj_ref, lam_ref, q_ref, k_ref, vt_ref, dmask_ref, g_ref, o_ref,
                          m_ref, acc_ref, *, lam_init):
    s = pl.program_id(0)
    qi = qi_ref[s]
    kj = kj_ref[s]

    @pl.when(kj == 0)
    def _():
        m_ref[...] = jnp.full(m_ref.shape, -jnp.inf, F32)
        acc_ref[...] = jnp.zeros(acc_ref.shape, F32)

    def step(diagonal):
        def scores(c):
            h = c // 2
            kp = k_ref[:, h * PAIR:(h + 1) * PAIR]
            qz = _half_only(q_ref[:, h * PAIR:(h + 1) * PAIR], c % 2 == 1)
            st = lax.dot_general(kp, qz, (((1,), (1,)), ((), ())), preferred_element_type=F32)
            return st + dmask_ref[...] if diagonal else st

        pending = [scores(c) for c in range(ATT_AHEAD)]
        for c in range(N_COMP):
            st = pending.pop(0)
            if c + ATT_AHEAD < N_COMP:
                pending.append(scores(c + ATT_AHEAD))
            m_old = m_ref[c:c + 1, :]
            m_new = jnp.maximum(m_old, jnp.max(st, axis=0, keepdims=True))
            alpha = jnp.exp2(m_old - m_new)
            p = jnp.exp2(st - m_new).astype(BF16)
            h = c // 2
            pv = jnp.dot(vt_ref[h * VT_ROWS:(h + 1) * VT_ROWS, :], p, preferred_element_type=F32)
            acc_ref[c] = alpha * acc_ref[c] + pv
            m_ref[c:c + 1, :] = m_new

    last_kj = qi // (ATT_TK // ATT_TQ)

    @pl.when(kj < last_kj)
    def _():
        step(False)

    @pl.when(kj == last_kj)
    def _():
        step(True)
        lam = lam_ref[0]
        for h in range(H_A):
            a0 = acc_ref[2 * h]
            a1 = acc_ref[2 * h + 1]
            o0 = a0[:PAIR] / a0[PAIR:PAIR + 1]
            o1 = a1[:PAIR] / a1[PAIR:PAIR + 1]
            ot = _subln(o0 - lam * o1, g_ref[...], lam_init, 0)
            o_ref[:, h * PAIR:(h + 1) * PAIR] = ot.T.astype(BF16)


def _attn_a_prompt(qa16, ka16, vat16, lam, subln_g, s_len, lam_init):
    ratio = ATT_TK // ATT_TQ
    assert ATT_TK == ratio * ATT_TQ and s_len % ATT_TK == 0
    kchunk = jnp.arange(ATT_TK, dtype=I32)[None, :, None] // CHUNK
    qchunk = (jnp.arange(ratio, dtype=I32)[:, None, None] * ATT_TQ
              + jnp.arange(ATT_TQ, dtype=I32)[None, None, :]) // CHUNK
    dmask = jnp.where(kchunk <= qchunk, 0.0, NEG_INF).astype(F32)
    nq = s_len // ATT_TQ
    qi_tab, kj_tab = [], []
    for i in range(nq):
        for j in range(i // ratio + 1):
            qi_tab.append(i)
            kj_tab.append(j)
    qi_tab = jnp.asarray(qi_tab, I32)
    kj_tab = jnp.asarray(kj_tab, I32)
    grid_spec = pltpu.PrefetchScalarGridSpec(
        num_scalar_prefetch=2,
        grid=(int(qi_tab.shape[0]),),
        in_specs=[
            pl.BlockSpec(memory_space=pltpu.SMEM),
            pl.BlockSpec((ATT_TQ, GROUP), lambda s, qi, kj: (qi[s], 0)),
            pl.BlockSpec((ATT_TK, GROUP), lambda s, qi, kj: (kj[s], 0)),
            pl.BlockSpec((H_A * VT_ROWS, ATT_TK), lambda s, qi, kj: (0, kj[s])),
            pl.BlockSpec((None, ATT_TK, ATT_TQ), lambda s, qi, kj: (qi[s] % ratio, 0, 0)),
            pl.BlockSpec((PAIR, 1), lambda s, qi, kj: (0, 0)),
        ],
        out_specs=pl.BlockSpec((ATT_TQ, GROUP), lambda s, qi, kj: (qi[s], 0)),
        scratch_shapes=[pltpu.VMEM((N_COMP, ATT_TQ), F32),
                        pltpu.VMEM((N_COMP, VT_ROWS, ATT_TQ), F32)],
    )
    return pl.pallas_call(
        functools.partial(_attn_a_prompt_kernel, lam_init=lam_init),
        grid_spec=grid_spec,
        out_shape=jax.ShapeDtypeStruct((s_len, GROUP), BF16),
        compiler_params=_cparams(("arbitrary",)),
        name="attn_a_prompt",
    )(qi_tab, kj_tab, lam, qa16, ka16, vat16, dmask, subln_g.reshape(PAIR, 1))


def _attend(qz, segs, exp=jnp.exp):
    scores = []
    for k, _, bias, mask in segs:
        sc = lax.dot_general(qz, k, (((1,), (1,)), ((), ())), preferred_element_type=F32)
        if bias is not None:
            sc = sc + bias
        if mask is not None:
            sc = jnp.where(mask, sc, NEG_INF)
        scores.append(sc)
    m = functools.reduce(jnp.maximum, [jnp.max(sc, axis=1, keepdims=True) for sc in scores])
    l = None
    o = None
    for sc, (_, v, _, _) in zip(scores, segs):
        p = exp(sc - m)
        ls = jnp.sum(p, axis=1, keepdims=True)
        os_ = jnp.dot(p.astype(BF16), v, preferred_element_type=F32)
        l = ls if l is None else l + ls
        o = os_ if o is None else o + os_
    return o / l


def _attn_a_sample_kernel(lam_ref, q_ref, kn_ref, vn_ref, kc_ref, vc_ref, g_ref, o_ref, *, lam_init):
    lam = lam_ref[0]
    for h in range(H_A):
        cols = slice(h * PAIR, (h + 1) * PAIR)
        segs = [(kc_ref[:, cols].astype(BF16), vc_ref[:, cols].astype(BF16), None, None),
                (kn_ref[:, cols], vn_ref[:, cols], None, None)]
        qp = q_ref[:, cols]
        o0 = _attend(_half_only(qp, False), segs, jnp.exp2)
        o1 = _attend(_half_only(qp, True), segs, jnp.exp2)
        o = _subln(o0 - lam * o1, g_ref[...], lam_init, 1)
        o_ref[:, cols] = o.astype(BF16)


def _attn_a_sample(qa16, ka16, va16, cache_k, cache_v, lam, subln_g, row0, lam_init):
    nb, past, _ = cache_k.shape
    sd = CHUNK
    new = lambda b: (row0 // sd + b, 0)
    return pl.pallas_call(
        functools.partial(_attn_a_sample_kernel, lam_init=lam_init),
        grid=(nb,),
        in_specs=[
            pl.BlockSpec(memory_space=pltpu.SMEM),
            pl.BlockSpec((sd, GROUP), new), pl.BlockSpec((sd, GROUP), new), pl.BlockSpec((sd, GROUP), new),
            pl.BlockSpec((None, past, GROUP), lambda b: (b, 0, 0)),
            pl.BlockSpec((None, past, GROUP), lambda b: (b, 0, 0)),
            pl.BlockSpec((1, PAIR), lambda b: (0, 0)),
        ],
        out_specs=pl.BlockSpec((sd, GROUP), lambda b: (b, 0)),
        out_shape=jax.ShapeDtypeStruct((nb * sd, GROUP), BF16),
        compiler_params=_cparams(("parallel",)),
        name="attn_a_sample",
    )(lam, qa16, ka16, va16, cache_k, cache_v, subln_g.reshape(1, PAIR))


def _band_bias(rel_bias, n_q_chunks):
    nq = n_q_chunks * CHUNK
    nk = nq + BAND_PAST
    heads = rel_bias.shape[0]
    n_lo = BAND_PAST - REL_CLIP
    period = max(nq + nk, n_lo + (2 * REL_CLIP + 1) + nq - 1)
    n_hi = period - (nq - 1) - n_lo - (2 * REL_CLIP + 1)
    assert n_lo >= 0
    lo = jnp.broadcast_to(rel_bias[:, :1], (heads, n_lo))
    hi = jnp.broadcast_to(rel_bias[:, -1:], (heads, n_hi))
    wrap = jnp.broadcast_to(rel_bias[:, :1], (heads, nq - 1))
    e = jnp.concatenate([lo, rel_bias, hi, wrap], axis=1).astype(F32)
    flat = jnp.broadcast_to(e[:, None, :], (heads, nq, period)).reshape(heads, nq * period)
    bias = flat[:, :nq * (period - 1)].reshape(heads, nq, period - 1)[:, :, :nk]
    qpos = jnp.arange(nq, dtype=I32)[:, None]
    kpos = jnp.arange(nk, dtype=I32)[None, :] - BAND_PAST
    dch = qpos // CHUNK - jnp.floor_divide(kpos, CHUNK)
    ok = (dch >= 0) & (dch <= BAND_CHUNKS)
    return jnp.where(ok[None], bias, NEG_INF)


def _band_heads(q_ref, seg_refs, bias_ref, o_ref):
    for j in range(H_B // 2):
        cols = slice(j * PAIR, (j + 1) * PAIR)
        qp = q_ref[:, cols]
        res = []
        for half in range(2):
            segs = []
            for k_ref, v_ref, col0, mask in seg_refs:
                n = k_ref.shape[0]
                segs.append((k_ref[:, cols].astype(BF16), v_ref[:, cols].astype(BF16),
                             bias_ref[2 * j + half, :, col0:col0 + n], mask))
            res.append(_attend(_half_only(qp, half == 1), segs))
        o = jnp.where(_half_mask(res[0].shape, False), res[0], res[1])
        o_ref[:, cols] = o.astype(BF16)


def _attn_b_prompt_kernel(q_ref, k0_ref, k1_ref, k2_ref, v0_ref, v1_ref, v2_ref, bias_ref, o_ref):
    i = pl.program_id(0)
    kpos = lax.broadcasted_iota(I32, (BAND_TQ, BAND_TQ), 1) + (i * BAND_TQ - BAND_PAST)
    seg_refs = [(k0_ref, v0_ref, 0, kpos >= 0),
                (k1_ref, v1_ref, BAND_TQ, kpos + BAND_TQ >= 0),
                (k2_ref, v2_ref, 2 * BAND_TQ, None)]
    _band_heads(q_ref, seg_refs, bias_ref, o_ref)


def _attn_b_prompt(qb16, kb16, vb16, bias, s_len):
    assert BAND_PAST == 2 * BAND_TQ
    cur = lambda i: (i, 0)
    p1 = lambda i: (jnp.maximum(i - 1, 0), 0)
    p2 = lambda i: (jnp.maximum(i - 2, 0), 0)
    blk = lambda f: pl.BlockSpec((BAND_TQ, GROUP), f)
    return pl.pallas_call(
        _attn_b_prompt_kernel,
        grid=(s_len // BAND_TQ,),
        in_specs=[blk(cur), blk(p2), blk(p1), blk(cur), blk(p2), blk(p1), blk(cur),
                  pl.BlockSpec(bias.shape, lambda i: (0, 0, 0))],
        out_specs=blk(cur),
        out_shape=jax.ShapeDtypeStruct((s_len, GROUP), BF16),
        compiler_params=_cparams(("parallel",)),
        name="attn_b_prompt",
    )(qb16, kb16, kb16, kb16, vb16, vb16, vb16, bias)


def _attn_b_sample_kernel(q_ref, kn_ref, vn_ref, kc_ref, vc_ref, bias_ref, o_ref):
    seg_refs = [(kc_ref, vc_ref, 0, None), (kn_ref, vn_ref, kc_ref.shape[0], None)]
    _band_heads(q_ref, seg_refs, bias_ref, o_ref)


def _attn_b_sample(qb16, kb16, vb16, cache_k, cache_v, bias, row0):
    nb, lb, _ = cache_k.shape
    sd = CHUNK
    new = lambda b: (row0 // sd + b, 0)
    return pl.pallas_call(
        _attn_b_sample_kernel,
        grid=(nb,),
        in_specs=[
            pl.BlockSpec((sd, GROUP), new), pl.BlockSpec((sd, GROUP), new), pl.BlockSpec((sd, GROUP), new),
            pl.BlockSpec((None, lb, GROUP), lambda b: (b, 0, 0)),
            pl.BlockSpec((None, lb, GROUP), lambda b: (b, 0, 0)),
            pl.BlockSpec(bias.shape, lambda b: (0, 0, 0)),
        ],
        out_specs=pl.BlockSpec((sd, GROUP), lambda b: (b, 0)),
        out_shape=jax.ShapeDtypeStruct((nb * sd, GROUP), BF16),
        compiler_params=_cparams(("parallel",)),
        name="attn_b_sample",
    )(qb16, kb16, vb16, cache_k, cache_v, bias)


def _layer_norm(z, g, b):
    mu = jnp.mean(z, axis=-1, keepdims=True)
    var = jnp.mean(jnp.square(z - mu), axis=-1, keepdims=True)
    return (z - mu) * lax.rsqrt(var + LN_EPS) * g + b


def _to_token_tiles(ref, x):
    rows = x.shape[0]
    for j in range(TT):
        ref[pl.ds(j, rows, stride=TT), :] = x[:, j * LANES:(j + 1) * LANES]


def _from_token_tiles(ref, rows):
    return jnp.concatenate([ref[pl.ds(j, rows, stride=TT), :] for j in range(TT)], axis=1)


def _merge_kernel(oap_ref, obp_ref, oas_ref, obs_ref, xp_ref, xs_ref, wo_ref, g_ref, b_ref, wr_ref,
                  br_ref, x1_ref, ti_ref, gate_ref, *, n_prompt_blocks):
    is_prompt = pl.program_id(0) < n_prompt_blocks
    oa = jnp.where(is_prompt, oap_ref[...], oas_ref[...])
    ob = jnp.where(is_prompt, obp_ref[...], obs_ref[...])
    x = jnp.where(is_prompt, xp_ref[...], xs_ref[...])
    mix = jnp.dot(oa, wo_ref[:GROUP, :], preferred_element_type=F32)
    mix = mix + jnp.dot(ob, wo_ref[GROUP:, :], preferred_element_type=F32)
    x1 = _layer_norm(DEEPNORM_ALPHA * x + mix, g_ref[...], b_ref[...])
    _to_token_tiles(x1_ref, x1)
    logits = jnp.dot(x1.astype(BF16), wr_ref[...], preferred_element_type=F32) + br_ref[...]
    lane = lax.broadcasted_iota(I32, logits.shape, 1)
    logits = jnp.where(lane < N_EXPERTS, logits, -jnp.inf)
    vals, idxs = [], []
    for _ in range(TOP_K):
        mx = jnp.max(logits, axis=1, keepdims=True)
        ix = jnp.min(jnp.where(logits == mx, lane, LANES), axis=1, keepdims=True)
        vals.append(mx)
        idxs.append(ix)
        logits = jnp.where(lane == ix, -jnp.inf, logits)
    col = lax.broadcasted_iota(I32, ti_ref.shape, 1)
    top_v = jnp.broadcast_to(vals[-1], ti_ref.shape)
    top_i = jnp.broadcast_to(idxs[-1], ti_ref.shape)
    for k in range(TOP_K - 1):
        top_v = jnp.where(col == k, vals[k], top_v)
        top_i = jnp.where(col == k, idxs[k], top_i)
    ti_ref[...] = top_i
    e = jnp.exp(top_v - vals[0])
    gate_ref[...] = e / jnp.sum(e, axis=1, keepdims=True)


def _merge(oa_p, ob_p, oa_s, ob_s, x_p, x_s, wo16, ln_g, ln_b, wr16, br):
    t = x_p.shape[0] + x_s.shape[0]
    npb = x_p.shape[0] // MERGE_ROWS
    row = lambda i: (i, 0)
    const = lambda i: (0, 0)
    blk = lambda cols: pl.BlockSpec((MERGE_ROWS, cols), row)
    prompt = lambda cols: pl.BlockSpec((MERGE_ROWS, cols), lambda i: (jnp.minimum(i, npb - 1), 0))
    sample = lambda cols: pl.BlockSpec((MERGE_ROWS, cols), lambda i: (jnp.maximum(i - npb, 0), 0))
    return pl.pallas_call(
        functools.partial(_merge_kernel, n_prompt_blocks=npb),
        grid=(t // MERGE_ROWS,),
        in_specs=[prompt(GROUP), prompt(GROUP), sample(GROUP), sample(GROUP),
                  prompt(D_MODEL), sample(D_MODEL),
                  pl.BlockSpec(wo16.shape, const), pl.BlockSpec((1, D_MODEL), const),
                  pl.BlockSpec((1, D_MODEL), const), pl.BlockSpec(wr16.shape, const),
                  pl.BlockSpec((1, LANES), const)],
        out_specs=[pl.BlockSpec((MERGE_ROWS * TT, LANES), row), blk(TOP_K), blk(TOP_K)],
        out_shape=[jax.ShapeDtypeStruct((t * TT, LANES), F32), jax.ShapeDtypeStruct((t, TOP_K), I32),
                   jax.ShapeDtypeStruct((t, TOP_K), F32)],
        compiler_params=_cparams(("parallel",)),
        name="merge",
    )(oa_p, ob_p, oa_s, ob_s, x_p, x_s, wo16, ln_g.reshape(1, D_MODEL), ln_b.reshape(1, D_MODEL),
      wr16, br)


def _moe_kernel(blk_e_ref, nused_ref, idx_hbm, x_hbm, wg_ref, wu_ref, wd_ref, bg_ref, bu_ref, bd_ref,
                out_hbm, idx_smem, xbuf, ybuf, isem, gsem, ssem, *, n_tok):
    del blk_e_ref
    i = pl.program_id(0)
    nused = nused_ref[0]
    n_assign = n_tok * TOP_K
    last = nused - 1
    par = lax.rem(i, 2)

    def idx_copy(block, slot):
        return pltpu.make_async_copy(idx_hbm.at[block], idx_smem.at[slot], isem.at[slot])

    def start_gather(islot, xslot):
        for r in range(MOE_ROWS):
            src = pl.multiple_of(idx_smem[islot, 0, r], TT)
            pltpu.make_async_copy(x_hbm.at[pl.ds(src, TT)], xbuf.at[xslot, pl.ds(r * TT, TT)],
                                  gsem.at[xslot]).start()

    def wait_gather(xslot):
        pltpu.make_async_copy(x_hbm.at[pl.ds(0, MOE_ROWS * TT)], xbuf.at[xslot], gsem.at[xslot]).wait()

    def start_scatter(islot, yslot):
        for r in range(MOE_ROWS):
            dst = pl.multiple_of(idx_smem[islot, 1, r], TT)
            pltpu.make_async_copy(ybuf.at[yslot, pl.ds(r * TT, TT)], out_hbm.at[pl.ds(dst, TT)],
                                  ssem.at[yslot]).start()

    def slot_flush(yslot, row0):
        return pltpu.make_async_copy(ybuf.at[yslot], out_hbm.at[pl.ds(row0 * TT, MOE_ROWS * TT)],
                                     ssem.at[yslot])

    @pl.when(i == 0)
    def _():
        ybuf[...] = jnp.zeros(ybuf.shape, F32)
        for p in range(2):
            slot_flush(p, n_assign + p * MOE_ROWS).start()
        idx_copy(0, 0).start()
        idx_copy(0, 0).wait()
        start_gather(0, 0)
        idx_copy(jnp.minimum(1, last), 1).start()

    @pl.when(i < nused)
    def _():
        s_cur = lax.rem(i, 3)
        s_nxt = lax.rem(i + 1, 3)
        s_nn = lax.rem(i + 2, 3)
        idx_copy(0, s_nxt).wait()
        start_gather(s_nxt, 1 - par)
        idx_copy(jnp.minimum(i + 2, last), s_nn).start()

        wait_gather(par)
        x = _from_token_tiles(xbuf.at[par], MOE_ROWS).astype(BF16)
        hg = jnp.dot(x, wg_ref[...], preferred_element_type=F32) + bg_ref[...]
        hu = jnp.dot(x, wu_ref[...], preferred_element_type=F32) + bu_ref[...]
        gate = jnp.minimum(hg, SWIGLU_LIMIT)
        up = jnp.clip(hu, -SWIGLU_LIMIT, SWIGLU_LIMIT)
        glu = gate * (1.0 / (1.0 + jnp.exp(-(gate * SWIGLU_ALPHA))))
        act = ((up + 1.0) * glu).astype(BF16)
        y = jnp.dot(act, wd_ref[...], preferred_element_type=F32) + bd_ref[...]

        slot_flush(par, 0).wait()
        _to_token_tiles(ybuf.at[par], y)
        start_scatter(s_cur, par)

        @pl.when(i == last)
        def _():
            slot_flush(par, 0).wait()
            slot_flush(1 - par, 0).wait()
            wait_gather(1 - par)
            idx_copy(0, s_nn).wait()


def _moe(x1, idx, blk_e, nused, wg16, wu16, wd16, bg, bu, bd):
    t = x1.shape[0] // TT
    nblk = idx.shape[0]
    wmap = lambda i, be, nu: (be[i], 0, 0)
    grid_spec = pltpu.PrefetchScalarGridSpec(
        num_scalar_prefetch=2,
        grid=(nblk,),
        in_specs=[
            pl.BlockSpec(memory_space=pl.ANY),
            pl.BlockSpec(memory_space=pl.ANY),
            pl.BlockSpec((None, D_MODEL, D_MODEL), wmap),
            pl.BlockSpec((None, D_MODEL, D_MODEL), wmap),
            pl.BlockSpec((None, D_MODEL, D_MODEL), wmap),
            pl.BlockSpec((None, 1, D_MODEL), wmap),
            pl.BlockSpec((None, 1, D_MODEL), wmap),
            pl.BlockSpec((None, 1, D_MODEL), wmap),
        ],
        out_specs=pl.BlockSpec(memory_space=pl.ANY),
        scratch_shapes=[
            pltpu.SMEM((3, 2, MOE_ROWS), I32),
            pltpu.VMEM((2, MOE_ROWS * TT, LANES), F32),
            pltpu.VMEM((2, MOE_ROWS * TT, LANES), F32),
            pltpu.SemaphoreType.DMA((3,)),
            pltpu.SemaphoreType.DMA((2,)),
            pltpu.SemaphoreType.DMA((2,)),
        ],
    )
    return pl.pallas_call(
        functools.partial(_moe_kernel, n_tok=t),
        grid_spec=grid_spec,
        out_shape=jax.ShapeDtypeStruct(((t * TOP_K + 2 * MOE_ROWS) * TT, LANES), F32),
        compiler_params=_cparams(("arbitrary",)),
        name="moe",
    )(blk_e, nused, idx, x1, wg16, wu16, wd16, bg, bu, bd)


def _route(top_i):
    t = top_i.shape[0]
    n = t * TOP_K
    nblk = n // MOE_ROWS + N_EXPERTS
    flat_e = top_i.reshape(-1)
    order = jnp.argsort(flat_e, stable=True).astype(I32)
    experts = jnp.arange(N_EXPERTS, dtype=I32)
    counts = jnp.sum((flat_e[:, None] == experts[None, :]).astype(I32), axis=0)
    padded = ((counts + MOE_ROWS - 1) // MOE_ROWS) * MOE_ROWS
    pad_end = jnp.cumsum(padded)
    pad_start = pad_end - padded
    start = jnp.cumsum(counts) - counts
    nused = (pad_end[-1] // MOE_ROWS).astype(I32).reshape(1)
    blk_first = jnp.arange(nblk, dtype=I32) * MOE_ROWS
    blk_e = jnp.minimum(jnp.sum((blk_first[:, None] >= pad_end[None, :]).astype(I32), axis=1),
                        N_EXPERTS - 1)
    row = jnp.arange(MOE_ROWS, dtype=I32)[None, :]
    j = blk_first[:, None] + row - pad_start[blk_e][:, None]
    valid = j < counts[blk_e][:, None]
    a = order[jnp.clip(start[blk_e][:, None] + j, 0, n - 1)]
    tok = lax.shift_right_logical(a, TOP_SHIFT)
    src = jnp.where(valid, tok, 0)
    dump = n + (jnp.arange(nblk, dtype=I32)[:, None] % 2) * MOE_ROWS + row
    dst = jnp.where(valid, (a & (TOP_K - 1)) * t + tok, dump)
    return jnp.stack([src * TT, dst * TT], axis=1).astype(I32), blk_e.astype(I32), nused


def _final_kernel(x1_ref, gate_ref, y0_ref, y1_ref, y2_ref, y3_ref, g_ref, b_ref, o_ref):
    gates = gate_ref[...]
    y = gates[:, 0:1] * _from_token_tiles(y0_ref, FINAL_ROWS)
    for k, ref in enumerate((y1_ref, y2_ref, y3_ref), start=1):
        y = y + gates[:, k:k + 1] * _from_token_tiles(ref, FINAL_ROWS)
    x1 = _from_token_tiles(x1_ref, FINAL_ROWS)
    o_ref[...] = _layer_norm(DEEPNORM_ALPHA * x1 + y, g_ref[...], b_ref[...])


def _final(x1, gates, planes, ln_g, ln_b, row0, rows):
    nb_all = gates.shape[0] // FINAL_ROWS
    b0 = row0 // FINAL_ROWS
    row = lambda i: (b0 + i, 0)
    const = lambda i: (0, 0)
    tiles = lambda f: pl.BlockSpec((FINAL_ROWS * TT, LANES), f)
    plane = lambda k: tiles(lambda i: (k * nb_all + b0 + i, 0))
    return pl.pallas_call(
        _final_kernel,
        grid=(rows // FINAL_ROWS,),
        in_specs=[tiles(row), pl.BlockSpec((FINAL_ROWS, TOP_K), row),
                  plane(0), plane(1), plane(2), plane(3),
                  pl.BlockSpec((1, D_MODEL), const), pl.BlockSpec((1, D_MODEL), const)],
        out_specs=pl.BlockSpec((FINAL_ROWS, D_MODEL), lambda i: (i, 0)),
        out_shape=jax.ShapeDtypeStruct((rows, D_MODEL), F32),
        compiler_params=_cparams(("parallel",)),
        name="final",
    )(x1, gates, planes, planes, planes, planes, ln_g.reshape(1, D_MODEL), ln_b.reshape(1, D_MODEL))


def kernel(x_prompt, x_sample, cache_a_k, cache_a_v, cache_b_k, cache_b_v, w_in, lambda_qk, subln_g,
           rel_bias, w_out, ln1_g, ln1_b, w_router, b_router, w_gate_up, b_gate_up, w_down, b_down,
           ln2_g, ln2_b):
    b, s, d = x_prompt.shape
    bd, sd, _ = x_sample.shape
    depth, _, past, _, _ = cache_a_k.shape
    lb = cache_b_k.shape[2]
    assert depth == 1 and b == 1 and d == D_MODEL
    assert sd == CHUNK and past % CHUNK == 0 and lb == BAND_PAST and s >= BAND_PAST
    assert s % ATT_TQ == 0 and s % BAND_TQ == 0
    ts = bd * sd
    t = s + ts
    lam_init = _lambda_init(0)

    x_p = x_prompt.reshape(s, d)
    x_s = x_sample.reshape(ts, d)
    pos_p = jnp.arange(s, dtype=I32)
    pos_s = jnp.tile(past + jnp.arange(sd, dtype=I32), bd)
    lq = lambda_qk[0].astype(F32)
    lam = (jnp.exp(jnp.sum(lq[0] * lq[1])) - jnp.exp(jnp.sum(lq[2] * lq[3])) + lam_init).reshape(1)

    w_in16 = w_in[0].astype(BF16)
    (qa_p, ka32_p, ka_p, va32_p, _, vat_p, qb_p, kb32_p, kb_p, vb32_p, vb_p) = _project(x_p, w_in16, pos_p)
    (qa_s, ka32_s, ka_s, va32_s, va_s, _, qb_s, kb32_s, kb_s, vb32_s, vb_s) = _project(x_s, w_in16, pos_s)

    oa_p = _attn_a_prompt(qa_p, ka_p, vat_p, lam, subln_g[0], s, lam_init)
    oa_s = _attn_a_sample(qa_s, ka_s, va_s, cache_a_k[0].reshape(bd, past, GROUP),
                          cache_a_v[0].reshape(bd, past, GROUP), lam, subln_g[0], 0, lam_init)

    ob_p = _attn_b_prompt(qb_p, kb_p, vb_p, _band_bias(rel_bias[0], BAND_TQ // CHUNK), s)
    ob_s = _attn_b_sample(qb_s, kb_s, vb_s, cache_b_k[0].reshape(bd, lb, GROUP),
                          cache_b_v[0].reshape(bd, lb, GROUP), _band_bias(rel_bias[0], 1), 0)

    wr16 = jnp.pad(w_router[0], ((0, 0), (0, LANES - N_EXPERTS))).astype(BF16)
    br = jnp.pad(b_router[0], (0, LANES - N_EXPERTS)).reshape(1, LANES)
    x1, top_i, gates = _merge(oa_p, ob_p, oa_s, ob_s, x_p, x_s, w_out[0].astype(BF16), ln1_g[0],
                              ln1_b[0], wr16, br)

    idx, blk_e, nused = _route(top_i)
    wgu = w_gate_up[0].reshape(N_EXPERTS, D_MODEL, D_MODEL, 2)
    bgu = b_gate_up[0].reshape(N_EXPERTS, 1, D_MODEL, 2)
    planes = _moe(x1, idx, blk_e, nused, wgu[..., 0].astype(BF16), wgu[..., 1].astype(BF16),
                  w_down[0].astype(BF16), bgu[..., 0], bgu[..., 1], b_down[0].reshape(N_EXPERTS, 1, D_MODEL))
    y_p = _final(x1, gates, planes, ln2_g[0], ln2_b[0], 0, s)
    y_s = _final(x1, gates, planes, ln2_g[0], ln2_b[0], s, ts)

    heads_a = lambda a, n: a.reshape(1, n, -1, N_COMP, D_HEAD_A)
    vals_a = lambda a, n: a.reshape(1, n, -1, H_A, 2 * D_HEAD_A)
    heads_b = lambda a, n: a.reshape(1, n, -1, H_B, D_HEAD_B)
    keep_s = min(BAND_PAST, lb + sd)
    kb_new = jnp.concatenate([cache_b_k[0], heads_b(kb32_s, bd)[0]], axis=1)[:, lb + sd - keep_s:]
    vb_new = jnp.concatenate([cache_b_v[0], heads_b(vb32_s, bd)[0]], axis=1)[:, lb + sd - keep_s:]
    return (y_p.reshape(b, s, d), y_s.reshape(bd, sd, d),
            heads_a(ka32_p, b), vals_a(va32_p, b),
            heads_b(kb32_p[s - BAND_PAST:], b), heads_b(vb32_p[s - BAND_PAST:], b),
            heads_a(ka32_s, bd), vals_a(va32_s, bd),
            kb_new[None], vb_new[None])
```

```python
import functools
import math

import jax
import jax.numpy as jnp
from jax import lax
from jax.experimental import pallas as pl
from jax.experimental.pallas import tpu as pltpu

F32 = jnp.float32
BF16 = jnp.bfloat16
I32 = jnp.int32

D_MODEL = 1024
CHUNK = 64
D_HEAD_A = 64
H_A = 4
N_COMP = 2 * H_A
ROPE_DIM = D_HEAD_A // 4
ROPE_THETA = 500000.0
D_HEAD_B = 64
H_B = 8
BAND_CHUNKS = 8
BAND_PAST = BAND_CHUNKS * CHUNK
REL_CLIP = 128
GROUP = 512
N_EXPERTS = 32
TOP_K = 4
TOP_SHIFT = 2
SWIGLU_LIMIT = 7.0
SWIGLU_ALPHA = 1.702
LN_EPS = 1e-5
SUBLN_EPS = 1e-5
DEPTH = 1
DEEPNORM_ALPHA = (2.0 * DEPTH) ** 0.25
NEG_INF = -1e30
LANES = 128
PAIR = 2 * D_HEAD_A
TT = D_MODEL // LANES
VT_ROWS = PAIR + 16
LOG2_E = math.log2(math.e)

VMEM_LIMIT = 56 * 1024 * 1024

PROJ_ROWS = 256
ATT_TQ = 512
ATT_TK = 1024
ATT_AHEAD = 2
BAND_TQ = 256
MERGE_ROWS = 256
MOE_ROWS = 256
GU_GROUP = 256
GU_GROUPS = 2 * D_MODEL // GU_GROUP
FINAL_ROWS = 256


def _lambda_init(layer_idx):
    return 0.8 - 0.6 * math.exp(-0.3 * layer_idx)


def _cparams(sem):
    return pltpu.CompilerParams(dimension_semantics=sem, vmem_limit_bytes=VMEM_LIMIT)


def _half_mask(shape, upper):
    lane = lax.broadcasted_iota(I32, shape, len(shape) - 1)
    return (lane >= D_HEAD_A) if upper else (lane < D_HEAD_A)


def _proj_kernel(x_ref, w_ref, cos_ref, s1_ref, s2_ref,
                 qa_ref, ka32_ref, ka16_ref, va32_ref, va16_ref, vat_ref,
                 qb_ref, kb32_ref, kb16_ref, vb32_ref, vb16_ref):
    xb = x_ref[...].astype(BF16)

    def group(g):
        return jnp.dot(xb, w_ref[:, g * GROUP:(g + 1) * GROUP], preferred_element_type=F32)

    cos = cos_ref[...]
    s1 = s1_ref[...]
    s2 = s2_ref[...]

    def rope(h):
        parts = []
        for j in range(GROUP // LANES):
            hj = h[:, j * LANES:(j + 1) * LANES]
            nxt = pltpu.roll(hj, LANES - ROPE_DIM // 2, 1)
            prv = pltpu.roll(hj, ROPE_DIM // 2, 1)
            parts.append(hj * cos + nxt * s1 + prv * s2)
        return jnp.concatenate(parts, axis=1)

    qa = rope(group(0)) * (D_HEAD_A ** -0.5 * LOG2_E)
    qa_ref[...] = qa.astype(BF16)
    ka = rope(group(1))
    ka32_ref[...] = ka
    ka16_ref[...] = ka.astype(BF16)
    va = group(2)
    va32_ref[...] = va
    va16_ref[...] = va.astype(BF16)
    vt = va.T
    rows = vt.shape[1]
    tail = jnp.concatenate([jnp.ones((1, rows), F32), jnp.zeros((VT_ROWS - PAIR - 1, rows), F32)], axis=0)
    parts = []
    for h in range(H_A):
        parts += [vt[h * PAIR:(h + 1) * PAIR], tail]
    vat_ref[...] = jnp.concatenate(parts, axis=0).astype(BF16)
    qb_ref[...] = (group(3) * (D_HEAD_B ** -0.5)).astype(BF16)
    kb = group(4)
    kb32_ref[...] = kb
    kb16_ref[...] = kb.astype(BF16)
    vb = group(5)
    vb32_ref[...] = vb
    vb16_ref[...] = vb.astype(BF16)


def _rope_tables(pos):
    half = ROPE_DIM // 2
    inv = ROPE_THETA ** (-jnp.arange(half, dtype=F32) * 2.0 / ROPE_DIM)
    ang = pos.astype(F32)[:, None] * inv[None, :]
    widen = lambda a: jnp.tile(a, (1, LANES // half))
    in_head = (jnp.arange(LANES, dtype=I32) % D_HEAD_A)[None, :]
    cos = jnp.where(in_head < ROPE_DIM, widen(jnp.cos(ang)), 1.0)
    sin = widen(jnp.sin(ang))
    s1 = jnp.where(in_head < half, -sin, 0.0)
    s2 = jnp.where((in_head >= half) & (in_head < ROPE_DIM), sin, 0.0)
    return cos, s1, s2


def _project(x, w16, pos):
    t = x.shape[0]
    cos, s1, s2 = _rope_tables(pos)
    row = lambda i: (i, 0)
    blk = lambda cols: pl.BlockSpec((PROJ_ROWS, cols), row)
    f32o = jax.ShapeDtypeStruct((t, GROUP), F32)
    b16o = jax.ShapeDtypeStruct((t, GROUP), BF16)
    return pl.pallas_call(
        _proj_kernel,
        grid=(t // PROJ_ROWS,),
        in_specs=[blk(D_MODEL), pl.BlockSpec(w16.shape, lambda i: (0, 0)),
                  blk(LANES), blk(LANES), blk(LANES)],
        out_specs=[blk(GROUP), blk(GROUP), blk(GROUP), blk(GROUP), blk(GROUP),
                   pl.BlockSpec((H_A * VT_ROWS, PROJ_ROWS), lambda i: (0, i)),
                   blk(GROUP), blk(GROUP), blk(GROUP), blk(GROUP), blk(GROUP)],
        out_shape=[b16o, f32o, b16o, f32o, b16o, jax.ShapeDtypeStruct((H_A * VT_ROWS, t), BF16),
                   b16o, f32o, b16o, f32o, b16o],
        compiler_params=_cparams(("parallel",)),
        name="proj",
    )(x, w16, cos, s1, s2)


def _subln(o, g, lam_init, axis):
    o = o * lax.rsqrt(jnp.mean(jnp.square(o), axis=axis, keepdims=True) + SUBLN_EPS)
    return o * g * (1.0 - lam_init)


def _half_only(qp, upper):
    return jnp.where(_half_mask(qp.shape, upper), qp, jnp.zeros_like(qp))


def _attn_a_prompt_kernel(qi_ref, kj_ref, lam_ref, q_ref, k_ref, vt_ref, dmask_ref, g_ref, o_ref,
                          m_ref, acc_ref, *, lam_init):
    s = pl.program_id(0)
    qi = qi_ref[s]
    kj = kj_ref[s]

    @pl.when(kj == 0)
    def _():
        m_ref[...] = jnp.full(m_ref.shape, -jnp.inf, F32)
        acc_ref[...] = jnp.zeros(acc_ref.shape, F32)

    def step(diagonal):
        def scores(c):
            h = c // 2
            kp = k_ref[:, h * PAIR:(h + 1) * PAIR]
            qz = _half_only(q_ref[:, h * PAIR:(h + 1) * PAIR], c % 2 == 1)
            st = lax.dot_general(kp, qz, (((1,), (1,)), ((), ())), preferred_element_type=F32)
            return st + dmask_ref[...] if diagonal else st

        pending = [scores(c) for c in range(ATT_AHEAD)]
        for c in range(N_COMP):
            st = pending.pop(0)
            if c + ATT_AHEAD < N_COMP:
                pending.append(scores(c + ATT_AHEAD))
            m_old = m_ref[c:c + 1, :]
            m_new = jnp.maximum(m_old, jnp.max(st, axis=0, keepdims=True))
            alpha = jnp.exp2(m_old - m_new)
            p = jnp.exp2(st - m_new).astype(BF16)
            h = c // 2
            pv = jnp.dot(vt_ref[h * VT_ROWS:(h + 1) * VT_ROWS, :], p, preferred_element_type=F32)
            acc_ref[c] = alpha * acc_ref[c] + pv
            m_ref[c:c + 1, :] = m_new

    last_kj = qi // (ATT_TK // ATT_TQ)

    @pl.when(kj < last_kj)
    def _():
        step(False)

    @pl.when(kj == last_kj)
    def _():
        step(True)
        lam = lam_ref[0]
        for h in range(H_A):
            a0 = acc_ref[2 * h]
            a1 = acc_ref[2 * h + 1]
            o0 = a0[:PAIR] / a0[PAIR:PAIR + 1]
            o1 = a1[:PAIR] / a1[PAIR:PAIR + 1]
            ot = _subln(o0 - lam * o1, g_ref[...], lam_init, 0)
            o_ref[:, h * PAIR:(h + 1) * PAIR] = ot.T.astype(BF16)


def _attn_a_prompt(qa16, ka16, vat16, lam, subln_g, s_len, lam_init):
    ratio = ATT_TK // ATT_TQ
    assert ATT_TK == ratio * ATT_TQ and s_len % ATT_TK == 0
    kchunk = jnp.arange(ATT_TK, dtype=I32)[None, :, None] // CHUNK
    qchunk = (jnp.arange(ratio, dtype=I32)[:, None, None] * ATT_TQ
              + jnp.arange(ATT_TQ, dtype=I32)[None, None, :]) // CHUNK
    dmask = jnp.where(kchunk <= qchunk, 0.0, NEG_INF).astype(F32)
    nq = s_len // ATT_TQ
    qi_tab, kj_tab = [], []
    for i in range(nq):
        for j in range(i // ratio + 1):
            qi_tab.append(i)
            kj_tab.append(j)
    qi_tab = jnp.asarray(qi_tab, I32)
    kj_tab = jnp.asarray(kj_tab, I32)
    grid_spec = pltpu.PrefetchScalarGridSpec(
        num_scalar_prefetch=2,
        grid=(int(qi_tab.shape[0]),),
        in_specs=[
            pl.BlockSpec(memory_space=pltpu.SMEM),
            pl.BlockSpec((ATT_TQ, GROUP), lambda s, qi, kj: (qi[s], 0)),
            pl.BlockSpec((ATT_TK, GROUP), lambda s, qi, kj: (kj[s], 0)),
            pl.BlockSpec((H_A * VT_ROWS, ATT_TK), lambda s, qi, kj: (0, kj[s])),
            pl.BlockSpec((None, ATT_TK, ATT_TQ), lambda s, qi, kj: (qi[s] % ratio, 0, 0)),
            pl.BlockSpec((PAIR, 1), lambda s, qi, kj: (0, 0)),
        ],
        out_specs=pl.BlockSpec((ATT_TQ, GROUP), lambda s, qi, kj: (qi[s], 0)),
        scratch_shapes=[pltpu.VMEM((N_COMP, ATT_TQ), F32),
                        pltpu.VMEM((N_COMP, VT_ROWS, ATT_TQ), F32)],
    )
    return pl.pallas_call(
        functools.partial(_attn_a_prompt_kernel, lam_init=lam_init),
        grid_spec=grid_spec,
        out_shape=jax.ShapeDtypeStruct((s_len, GROUP), BF16),
        compiler_params=_cparams(("arbitrary",)),
        name="attn_a_prompt",
    )(qi_tab, kj_tab, lam, qa16, ka16, vat16, dmask, subln_g.reshape(PAIR, 1))


def _attend(qz, segs, exp=jnp.exp):
    scores = []
    for k, _, bias, mask in segs:
        sc = lax.dot_general(qz, k, (((1,), (1,)), ((), ())), preferred_element_type=F32)
        if bias is not None:
            sc = sc + bias
        if mask is not None:
            sc = jnp.where(mask, sc, NEG_INF)
        scores.append(sc)
    m = functools.reduce(jnp.maximum, [jnp.max(sc, axis=1, keepdims=True) for sc in scores])
    l = None
    o = None
    for sc, (_, v, _, _) in zip(scores, segs):
        p = exp(sc - m)
        ls = jnp.sum(p, axis=1, keepdims=True)
        os_ = jnp.dot(p.astype(BF16), v, preferred_element_type=F32)
        l = ls if l is None else l + ls
        o = os_ if o is None else o + os_
    return o / l


def _attn_a_sample_kernel(lam_ref, q_ref, kn_ref, vn_ref, kc_ref, vc_ref, g_ref, o_ref, *, lam_init):
    lam = lam_ref[0]
    for h in range(H_A):
        cols = slice(h * PAIR, (h + 1) * PAIR)
        segs = [(kc_ref[:, cols].astype(BF16), vc_ref[:, cols].astype(BF16), None, None),
                (kn_ref[:, cols], vn_ref[:, cols], None, None)]
        qp = q_ref[:, cols]
        o0 = _attend(_half_only(qp, False), segs, jnp.exp2)
        o1 = _attend(_half_only(qp, True), segs, jnp.exp2)
        o = _subln(o0 - lam * o1, g_ref[...], lam_init, 1)
        o_ref[:, cols] = o.astype(BF16)


def _attn_a_sample(qa16, ka16, va16, cache_k, cache_v, lam, subln_g, row0, lam_init):
    nb, past, _ = cache_k.shape
    sd = CHUNK
    new = lambda b: (row0 // sd + b, 0)
    return pl.pallas_call(
        functools.partial(_attn_a_sample_kernel, lam_init=lam_init),
        grid=(nb,),
        in_specs=[
            pl.BlockSpec(memory_space=pltpu.SMEM),
            pl.BlockSpec((sd, GROUP), new), pl.BlockSpec((sd, GROUP), new), pl.BlockSpec((sd, GROUP), new),
            pl.BlockSpec((None, past, GROUP), lambda b: (b, 0, 0)),
            pl.BlockSpec((None, past, GROUP), lambda b: (b, 0, 0)),
            pl.BlockSpec((1, PAIR), lambda b: (0, 0)),
        ],
        out_specs=pl.BlockSpec((sd, GROUP), lambda b: (b, 0)),
        out_shape=jax.ShapeDtypeStruct((nb * sd, GROUP), BF16),
        compiler_params=_cparams(("parallel",)),
        name="attn_a_sample",
    )(lam, qa16, ka16, va16, cache_k, cache_v, subln_g.reshape(1, PAIR))


def _band_bias(rel_bias, n_q_chunks):
    nq = n_q_chunks * CHUNK
    nk = nq + BAND_PAST
    heads = rel_bias.shape[0]
    n_lo = BAND_PAST - REL_CLIP
    period = max(nq + nk, n_lo + (2 * REL_CLIP + 1) + nq - 1)
    n_hi = period - (nq - 1) - n_lo - (2 * REL_CLIP + 1)
    assert n_lo >= 0
    lo = jnp.broadcast_to(rel_bias[:, :1], (heads, n_lo))
    hi = jnp.broadcast_to(rel_bias[:, -1:], (heads, n_hi))
    wrap = jnp.broadcast_to(rel_bias[:, :1], (heads, nq - 1))
    e = jnp.concatenate([lo, rel_bias, hi, wrap], axis=1).astype(F32)
    flat = jnp.broadcast_to(e[:, None, :], (heads, nq, period)).reshape(heads, nq * period)
    bias = flat[:, :nq * (period - 1)].reshape(heads, nq, period - 1)[:, :, :nk]
    qpos = jnp.arange(nq, dtype=I32)[:, None]
    kpos = jnp.arange(nk, dtype=I32)[None, :] - BAND_PAST
    dch = qpos // CHUNK - jnp.floor_divide(kpos, CHUNK)
    ok = (dch >= 0) & (dch <= BAND_CHUNKS)
    return jnp.where(ok[None], bias, NEG_INF)


def _band_heads(q_ref, seg_refs, bias_ref, o_ref):
    for j in range(H_B // 2):
        cols = slice(j * PAIR, (j + 1) * PAIR)
        qp = q_ref[:, cols]
        res = []
        for half in range(2):
            segs = []
            for k_ref, v_ref, col0, mask in seg_refs:
                n = k_ref.shape[0]
                segs.append((k_ref[:, cols].astype(BF16), v_ref[:, cols].astype(BF16),
                             bias_ref[2 * j + half, :, col0:col0 + n], mask))
            res.append(_attend(_half_only(qp, half == 1), segs))
        o = jnp.where(_half_mask(res[0].shape, False), res[0], res[1])
        o_ref[:, cols] = o.astype(BF16)


def _attn_b_prompt_kernel(q_ref, k0_ref, k1_ref, k2_ref, v0_ref, v1_ref, v2_ref, bias_ref, o_ref):
    i = pl.program_id(0)
    kpos = lax.broadcasted_iota(I32, (BAND_TQ, BAND_TQ), 1) + (i * BAND_TQ - BAND_PAST)
    seg_refs = [(k0_ref, v0_ref, 0, kpos >= 0),
                (k1_ref, v1_ref, BAND_TQ, kpos + BAND_TQ >= 0),
                (k2_ref, v2_ref, 2 * BAND_TQ, None)]
    _band_heads(q_ref, seg_refs, bias_ref, o_ref)


def _attn_b_prompt(qb16, kb16, vb16, bias, s_len):
    assert BAND_PAST == 2 * BAND_TQ
    cur = lambda i: (i, 0)
    p1 = lambda i: (jnp.maximum(i - 1, 0), 0)
    p2 = lambda i: (jnp.maximum(i - 2, 0), 0)
    blk = lambda f: pl.BlockSpec((BAND_TQ, GROUP), f)
    return pl.pallas_call(
        _attn_b_prompt_kernel,
        grid=(s_len // BAND_TQ,),
        in_specs=[blk(cur), blk(p2), blk(p1), blk(cur), blk(p2), blk(p1), blk(cur),
                  pl.BlockSpec(bias.shape, lambda i: (0, 0, 0))],
        out_specs=blk(cur),
        out_shape=jax.ShapeDtypeStruct((s_len, GROUP), BF16),
        compiler_params=_cparams(("parallel",)),
        name="attn_b_prompt",
    )(qb16, kb16, kb16, kb16, vb16, vb16, vb16, bias)


def _attn_b_sample_kernel(q_ref, kn_ref, vn_ref, kc_ref, vc_ref, bias_ref, o_ref):
    seg_refs = [(kc_ref, vc_ref, 0, None), (kn_ref, vn_ref, kc_ref.shape[0], None)]
    _band_heads(q_ref, seg_refs, bias_ref, o_ref)


def _attn_b_sample(qb16, kb16, vb16, cache_k, cache_v, bias, row0):
    nb, lb, _ = cache_k.shape
    sd = CHUNK
    new = lambda b: (row0 // sd + b, 0)
    return pl.pallas_call(
        _attn_b_sample_kernel,
        grid=(nb,),
        in_specs=[
            pl.BlockSpec((sd, GROUP), new), pl.BlockSpec((sd, GROUP), new), pl.BlockSpec((sd, GROUP), new),
            pl.BlockSpec((None, lb, GROUP), lambda b: (b, 0, 0)),
            pl.BlockSpec((None, lb, GROUP), lambda b: (b, 0, 0)),
            pl.BlockSpec(bias.shape, lambda b: (0, 0, 0)),
        ],
        out_specs=pl.BlockSpec((sd, GROUP), lambda b: (b, 0)),
        out_shape=jax.ShapeDtypeStruct((nb * sd, GROUP), BF16),
        compiler_params=_cparams(("parallel",)),
        name="attn_b_sample",
    )(qb16, kb16, vb16, cache_k, cache_v, bias)


def _layer_norm(z, g, b):
    mu = jnp.mean(z, axis=-1, keepdims=True)
    var = jnp.mean(jnp.square(z - mu), axis=-1, keepdims=True)
    return (z - mu) * lax.rsqrt(var + LN_EPS) * g + b


def _to_token_tiles(ref, x):
    rows = x.shape[0]
    for j in range(TT):
        ref[pl.ds(j, rows, stride=TT), :] = x[:, j * LANES:(j + 1) * LANES]


def _from_token_tiles(ref, rows):
    return jnp.concatenate([ref[pl.ds(j, rows, stride=TT), :] for j in range(TT)], axis=1)


def _merge_kernel(oap_ref, obp_ref, oas_ref, obs_ref, xp_ref, xs_ref, wo_ref, g_ref, b_ref, wr_ref,
                  br_ref, x1_ref, ti_ref, gate_ref, *, n_prompt_blocks):
    is_prompt = pl.program_id(0) < n_prompt_blocks
    oa = jnp.where(is_prompt, oap_ref[...], oas_ref[...])
    ob = jnp.where(is_prompt, obp_ref[...], obs_ref[...])
    x = jnp.where(is_prompt, xp_ref[...], xs_ref[...])
    mix = jnp.dot(oa, wo_ref[:GROUP, :], preferred_element_type=F32)
    mix = mix + jnp.dot(ob, wo_ref[GROUP:, :], preferred_element_type=F32)
    x1 = _layer_norm(DEEPNORM_ALPHA * x + mix, g_ref[...], b_ref[...])
    _to_token_tiles(x1_ref, x1)
    logits = jnp.dot(x1.astype(BF16), wr_ref[...], preferred_element_type=F32) + br_ref[...]
    lane = lax.broadcasted_iota(I32, logits.shape, 1)
    logits = jnp.where(lane < N_EXPERTS, logits, -jnp.inf)
    vals, idxs = [], []
    for _ in range(TOP_K):
        mx = jnp.max(logits, axis=1, keepdims=True)
        ix = jnp.min(jnp.where(logits == mx, lane, LANES), axis=1, keepdims=True)
        vals.append(mx)
        idxs.append(ix)
        logits = jnp.where(lane == ix, -jnp.inf, logits)
    col = lax.broadcasted_iota(I32, ti_ref.shape, 1)
    top_v = jnp.broadcast_to(vals[-1], ti_ref.shape)
    top_i = jnp.broadcast_to(idxs[-1], ti_ref.shape)
    for k in range(TOP_K - 1):
        top_v = jnp.where(col == k, vals[k], top_v)
        top_i = jnp.where(col == k, idxs[k], top_i)
    ti_ref[...] = top_i
    e = jnp.exp(top_v - vals[0])
    gate_ref[...] = e / jnp.sum(e, axis=1, keepdims=True)


def _merge(oa_p, ob_p, oa_s, ob_s, x_p, x_s, wo16, ln_g, ln_b, wr16, br):
    t = x_p.shape[0] + x_s.shape[0]
    npb = x_p.shape[0] // MERGE_ROWS
    row = lambda i: (i, 0)
    const = lambda i: (0, 0)
    blk = lambda cols: pl.BlockSpec((MERGE_ROWS, cols), row)
    prompt = lambda cols: pl.BlockSpec((MERGE_ROWS, cols), lambda i: (jnp.minimum(i, npb - 1), 0))
    sample = lambda cols: pl.BlockSpec((MERGE_ROWS, cols), lambda i: (jnp.maximum(i - npb, 0), 0))
    return pl.pallas_call(
        functools.partial(_merge_kernel, n_prompt_blocks=npb),
        grid=(t // MERGE_ROWS,),
        in_specs=[prompt(GROUP), prompt(GROUP), sample(GROUP), sample(GROUP),
                  prompt(D_MODEL), sample(D_MODEL),
                  pl.BlockSpec(wo16.shape, const), pl.BlockSpec((1, D_MODEL), const),
                  pl.BlockSpec((1, D_MODEL), const), pl.BlockSpec(wr16.shape, const),
                  pl.BlockSpec((1, LANES), const)],
        out_specs=[pl.BlockSpec((MERGE_ROWS * TT, LANES), row), blk(TOP_K), blk(TOP_K)],
        out_shape=[jax.ShapeDtypeStruct((t * TT, LANES), F32), jax.ShapeDtypeStruct((t, TOP_K), I32),
                   jax.ShapeDtypeStruct((t, TOP_K), F32)],
        compiler_params=_cparams(("parallel",)),
        name="merge",
    )(oa_p, ob_p, oa_s, ob_s, x_p, x_s, wo16, ln_g.reshape(1, D_MODEL), ln_b.reshape(1, D_MODEL),
      wr16, br)


def _moe_kernel(blk_e_ref, nused_ref, fresh_ref, idx_hbm, x_hbm, wgu_ref, wd_ref, perm_ref, bgu_ref,
                bd_ref, out_hbm, idx_smem, xbuf, ybuf, wgu16_ref, wd16_ref, isem, gsem, ssem,
                *, n_tok):
    del blk_e_ref
    i = pl.program_id(0)
    nused = nused_ref[0]
    n_assign = n_tok * TOP_K
    last = nused - 1
    par = lax.rem(i, 2)

    def idx_copy(block, slot):
        return pltpu.make_async_copy(idx_hbm.at[block], idx_smem.at[slot], isem.at[slot])

    def start_gather(islot, xslot):
        for r in range(MOE_ROWS):
            src = pl.multiple_of(idx_smem[islot, 0, r], TT)
            pltpu.make_async_copy(x_hbm.at[pl.ds(src, TT)], xbuf.at[xslot, pl.ds(r * TT, TT)],
                                  gsem.at[xslot]).start()

    def wait_gather(xslot):
        pltpu.make_async_copy(x_hbm.at[pl.ds(0, MOE_ROWS * TT)], xbuf.at[xslot], gsem.at[xslot]).wait()

    def start_scatter(islot, yslot):
        for r in range(MOE_ROWS):
            dst = pl.multiple_of(idx_smem[islot, 1, r], TT)
            pltpu.make_async_copy(ybuf.at[yslot, pl.ds(r * TT, TT)], out_hbm.at[pl.ds(dst, TT)],
                                  ssem.at[yslot]).start()

    def slot_flush(yslot, row0):
        return pltpu.make_async_copy(ybuf.at[yslot], out_hbm.at[pl.ds(row0 * TT, MOE_ROWS * TT)],
                                     ssem.at[yslot])

    @pl.when(i == 0)
    def _():
        ybuf[...] = jnp.zeros(ybuf.shape, F32)
        for p in range(2):
            slot_flush(p, n_assign + p * MOE_ROWS).start()
        idx_copy(0, 0).start()
        idx_copy(0, 0).wait()
        start_gather(0, 0)
        idx_copy(jnp.minimum(1, last), 1).start()

    @pl.when(i < nused)
    def _():
        s_cur = lax.rem(i, 3)
        s_nxt = lax.rem(i + 1, 3)
        s_nn = lax.rem(i + 2, 3)
        idx_copy(0, s_nxt).wait()
        start_gather(s_nxt, 1 - par)
        idx_copy(jnp.minimum(i + 2, last), s_nn).start()

        @pl.when(fresh_ref[i] == 1)
        def _():
            for g in range(GU_GROUPS):
                cols = slice(g * GU_GROUP, (g + 1) * GU_GROUP)
                sorted_cols = jnp.dot(wgu_ref[:, cols].astype(BF16), perm_ref[...],
                                      preferred_element_type=F32)
                wgu16_ref[:, cols] = sorted_cols.astype(BF16)
            wd16_ref[...] = wd_ref[...].astype(BF16)

        wait_gather(par)
        x = _from_token_tiles(xbuf.at[par], MOE_ROWS).astype(BF16)
        h = jnp.dot(x, wgu16_ref[...], preferred_element_type=F32) + bgu_ref[...]
        half = GU_GROUP // 2
        hg = jnp.concatenate([h[:, g * GU_GROUP:g * GU_GROUP + half] for g in range(GU_GROUPS)], axis=1)
        hu = jnp.concatenate([h[:, g * GU_GROUP + half:(g + 1) * GU_GROUP] for g in range(GU_GROUPS)],
                             axis=1)
        gate = jnp.minimum(hg, SWIGLU_LIMIT)
        up = jnp.clip(hu, -SWIGLU_LIMIT, SWIGLU_LIMIT)
        glu = gate * (1.0 / (1.0 + jnp.exp(-(gate * SWIGLU_ALPHA))))
        act = ((up + 1.0) * glu).astype(BF16)
        y = jnp.dot(act, wd16_ref[...], preferred_element_type=F32) + bd_ref[...]

        slot_flush(par, 0).wait()
        _to_token_tiles(ybuf.at[par], y)
        start_scatter(s_cur, par)

        @pl.when(i == last)
        def _():
            slot_flush(par, 0).wait()
            slot_flush(1 - par, 0).wait()
            wait_gather(1 - par)
            idx_copy(0, s_nn).wait()


def _gate_up_sorter():
    src = jnp.arange(GU_GROUP, dtype=I32)[:, None]
    dst = jnp.arange(GU_GROUP, dtype=I32)[None, :]
    half = GU_GROUP // 2
    return (src == jnp.where(dst < half, 2 * dst, 2 * (dst - half) + 1)).astype(BF16)


def _sort_gate_up(b):
    lead = b.shape[:-1]
    return b.reshape(lead + (GU_GROUPS, GU_GROUP // 2, 2)).swapaxes(-1, -2).reshape(lead + (-1,))


def _moe(x1, idx, blk_e, nused, fresh, w_gate_up, w_down, bgu, bd):
    t = x1.shape[0] // TT
    nblk = idx.shape[0]
    wmap = lambda i, be, nu, fr: (be[i], 0, 0)
    const = lambda i, be, nu, fr: (0, 0)
    grid_spec = pltpu.PrefetchScalarGridSpec(
        num_scalar_prefetch=3,
        grid=(nblk,),
        in_specs=[
            pl.BlockSpec(memory_space=pl.ANY),
            pl.BlockSpec(memory_space=pl.ANY),
            pl.BlockSpec((None, D_MODEL, 2 * D_MODEL), wmap),
            pl.BlockSpec((None, D_MODEL, D_MODEL), wmap),
            pl.BlockSpec((GU_GROUP, GU_GROUP), const),
            pl.BlockSpec((None, 1, 2 * D_MODEL), wmap),
            pl.BlockSpec((None, 1, D_MODEL), wmap),
        ],
        out_specs=pl.BlockSpec(memory_space=pl.ANY),
        scratch_shapes=[
            pltpu.SMEM((3, 2, MOE_ROWS), I32),
            pltpu.VMEM((2, MOE_ROWS * TT, LANES), F32),
            pltpu.VMEM((2, MOE_ROWS * TT, LANES), F32),
            pltpu.VMEM((D_MODEL, 2 * D_MODEL), BF16),
            pltpu.VMEM((D_MODEL, D_MODEL), BF16),
            pltpu.SemaphoreType.DMA((3,)),
            pltpu.SemaphoreType.DMA((2,)),
            pltpu.SemaphoreType.DMA((2,)),
        ],
    )
    return pl.pallas_call(
        functools.partial(_moe_kernel, n_tok=t),
        grid_spec=grid_spec,
        out_shape=jax.ShapeDtypeStruct(((t * TOP_K + 2 * MOE_ROWS) * TT, LANES), F32),
        compiler_params=_cparams(("arbitrary",)),
        name="moe",
    )(blk_e, nused, fresh, idx, x1, w_gate_up, w_down, _gate_up_sorter(), bgu, bd)


def _route(top_i):
    t = top_i.shape[0]
    n = t * TOP_K
    nblk = n // MOE_ROWS + N_EXPERTS
    flat_e = top_i.reshape(-1)
    order = jnp.argsort(flat_e, stable=True).astype(I32)
    experts = jnp.arange(N_EXPERTS, dtype=I32)
    counts = jnp.sum((flat_e[:, None] == experts[None, :]).astype(I32), axis=0)
    padded = ((counts + MOE_ROWS - 1) // MOE_ROWS) * MOE_ROWS
    pad_end = jnp.cumsum(padded)
    pad_start = pad_end - padded
    start = jnp.cumsum(counts) - counts
    nused = (pad_end[-1] // MOE_ROWS).astype(I32).reshape(1)
    blk_first = jnp.arange(nblk, dtype=I32) * MOE_ROWS
    blk_e = jnp.minimum(jnp.sum((blk_first[:, None] >= pad_end[None, :]).astype(I32), axis=1),
                        N_EXPERTS - 1)
    row = jnp.arange(MOE_ROWS, dtype=I32)[None, :]
    j = blk_first[:, None] + row - pad_start[blk_e][:, None]
    valid = j < counts[blk_e][:, None]
    a = order[jnp.clip(start[blk_e][:, None] + j, 0, n - 1)]
    tok = lax.shift_right_logical(a, TOP_SHIFT)
    src = jnp.where(valid, tok, 0)
    dump = n + (jnp.arange(nblk, dtype=I32)[:, None] % 2) * MOE_ROWS + row
    dst = jnp.where(valid, (a & (TOP_K - 1)) * t + tok, dump)
    blk_e = blk_e.astype(I32)
    fresh = jnp.concatenate([jnp.ones((1,), I32), (blk_e[1:] != blk_e[:-1]).astype(I32)])
    return jnp.stack([src * TT, dst * TT], axis=1).astype(I32), blk_e, nused, fresh


def _final_kernel(x1_ref, gate_ref, y0_ref, y1_ref, y2_ref, y3_ref, g_ref, b_ref, o_ref):
    gates = gate_ref[...]
    y = gates[:, 0:1] * _from_token_tiles(y0_ref, FINAL_ROWS)
    for k, ref in enumerate((y1_ref, y2_ref, y3_ref), start=1):
        y = y + gates[:, k:k + 1] * _from_token_tiles(ref, FINAL_ROWS)
    x1 = _from_token_tiles(x1_ref, FINAL_ROWS)
    o_ref[...] = _layer_norm(DEEPNORM_ALPHA * x1 + y, g_ref[...], b_ref[...])


def _final(x1, gates, planes, ln_g, ln_b, row0, rows):
    nb_all = gates.shape[0] // FINAL_ROWS
    b0 = row0 // FINAL_ROWS
    row = lambda i: (b0 + i, 0)
    const = lambda i: (0, 0)
    tiles = lambda f: pl.BlockSpec((FINAL_ROWS * TT, LANES), f)
    plane = lambda k: tiles(lambda i: (k * nb_all + b0 + i, 0))
    return pl.pallas_call(
        _final_kernel,
        grid=(rows // FINAL_ROWS,),
        in_specs=[tiles(row), pl.BlockSpec((FINAL_ROWS, TOP_K), row),
                  plane(0), plane(1), plane(2), plane(3),
                  pl.BlockSpec((1, D_MODEL), const), pl.BlockSpec((1, D_MODEL), const)],
        out_specs=pl.BlockSpec((FINAL_ROWS, D_MODEL), lambda i: (i, 0)),
        out_shape=jax.ShapeDtypeStruct((rows, D_MODEL), F32),
        compiler_params=_cparams(("parallel",)),
        name="final",
    )(x1, gates, planes, planes, planes, planes, ln_g.reshape(1, D_MODEL), ln_b.reshape(1, D_MODEL))


def kernel(x_prompt, x_sample, cache_a_k, cache_a_v, cache_b_k, cache_b_v, w_in, lambda_qk, subln_g,
           rel_bias, w_out, ln1_g, ln1_b, w_router, b_router, w_gate_up, b_gate_up, w_down, b_down,
           ln2_g, ln2_b):
    b, s, d = x_prompt.shape
    bd, sd, _ = x_sample.shape
    depth, _, past, _, _ = cache_a_k.shape
    lb = cache_b_k.shape[2]
    assert depth == 1 and b == 1 and d == D_MODEL
    assert sd == CHUNK and past % CHUNK == 0 and lb == BAND_PAST and s >= BAND_PAST
    assert s % ATT_TQ == 0 and s % BAND_TQ == 0
    ts = bd * sd
    t = s + ts
    lam_init = _lambda_init(0)

    x_p = x_prompt.reshape(s, d)
    x_s = x_sample.reshape(ts, d)
    pos_p = jnp.arange(s, dtype=I32)
    pos_s = jnp.tile(past + jnp.arange(sd, dtype=I32), bd)
    lq = lambda_qk[0].astype(F32)
    lam = (jnp.exp(jnp.sum(lq[0] * lq[1])) - jnp.exp(jnp.sum(lq[2] * lq[3])) + lam_init).reshape(1)

    w_in16 = w_in[0].astype(BF16)
    (qa_p, ka32_p, ka_p, va32_p, _, vat_p, qb_p, kb32_p, kb_p, vb32_p, vb_p) = _project(x_p, w_in16, pos_p)
    (qa_s, ka32_s, ka_s, va32_s, va_s, _, qb_s, kb32_s, kb_s, vb32_s, vb_s) = _project(x_s, w_in16, pos_s)

    oa_p = _attn_a_prompt(qa_p, ka_p, vat_p, lam, subln_g[0], s, lam_init)
    oa_s = _attn_a_sample(qa_s, ka_s, va_s, cache_a_k[0].reshape(bd, past, GROUP),
                          cache_a_v[0].reshape(bd, past, GROUP), lam, subln_g[0], 0, lam_init)

    ob_p = _attn_b_prompt(qb_p, kb_p, vb_p, _band_bias(rel_bias[0], BAND_TQ // CHUNK), s)
    ob_s = _attn_b_sample(qb_s, kb_s, vb_s, cache_b_k[0].reshape(bd, lb, GROUP),
                          cache_b_v[0].reshape(bd, lb, GROUP), _band_bias(rel_bias[0], 1), 0)

    wr16 = jnp.pad(w_router[0], ((0, 0), (0, LANES - N_EXPERTS))).astype(BF16)
    br = jnp.pad(b_router[0], (0, LANES - N_EXPERTS)).reshape(1, LANES)
    x1, top_i, gates = _merge(oa_p, ob_p, oa_s, ob_s, x_p, x_s, w_out[0].astype(BF16), ln1_g[0],
                              ln1_b[0], wr16, br)

    idx, blk_e, nused, fresh = _route(top_i)
    planes = _moe(x1, idx, blk_e, nused, fresh, w_gate_up[0], w_down[0],
                  _sort_gate_up(b_gate_up[0]).reshape(N_EXPERTS, 1, 2 * D_MODEL),
                  b_down[0].reshape(N_EXPERTS, 1, D_MODEL))
    y_p = _final(x1, gates, planes, ln2_g[0], ln2_b[0], 0, s)
    y_s = _final(x1, gates, planes, ln2_g[0], ln2_b[0], s, ts)

    heads_a = lambda a, n: a.reshape(1, n, -1, N_COMP, D_HEAD_A)
    vals_a = lambda a, n: a.reshape(1, n, -1, H_A, 2 * D_HEAD_A)
    heads_b = lambda a, n: a.reshape(1, n, -1, H_B, D_HEAD_B)
    keep_s = min(BAND_PAST, lb + sd)
    kb_new = jnp.concatenate([cache_b_k[0], heads_b(kb32_s, bd)[0]], axis=1)[:, lb + sd - keep_s:]
    vb_new = jnp.concatenate([cache_b_v[0], heads_b(vb32_s, bd)[0]], axis=1)[:, lb + sd - keep_s:]
    return (y_p.reshape(b, s, d), y_s.reshape(bd, sd, d),
            heads_a(ka32_p, b), vals_a(va32_p, b),
            heads_b(kb32_p[s - BAND_PAST:], b), heads_b(vb32_p[s - BAND_PAST:], b),
            heads_a(ka32_s, bd), vals_a(va32_s, bd),
            kb_new[None], vb_new[None])
```

```python
import functools
import math

import jax
import jax.numpy as jnp
from jax import lax
from jax.experimental import pallas as pl
from jax.experimental.pallas import tpu as pltpu

F32 = jnp.float32
BF16 = jnp.bfloat16
I32 = jnp.int32

D_MODEL = 1024
CHUNK = 64
D_HEAD_A = 64
H_A = 4
N_COMP = 2 * H_A
ROPE_DIM = D_HEAD_A // 4
ROPE_THETA = 500000.0
D_HEAD_B = 64
H_B = 8
BAND_CHUNKS = 8
BAND_PAST = BAND_CHUNKS * CHUNK
REL_CLIP = 128
GROUP = 512
N_EXPERTS = 32
TOP_K = 4
TOP_SHIFT = 2
SWIGLU_LIMIT = 7.0
SWIGLU_ALPHA = 1.702
LN_EPS = 1e-5
SUBLN_EPS = 1e-5
DEPTH = 1
DEEPNORM_ALPHA = (2.0 * DEPTH) ** 0.25
NEG_INF = -1e30
LANES = 128
PAIR = 2 * D_HEAD_A
TT = D_MODEL // LANES
VT_ROWS = PAIR + 16
LOG2_E = math.log2(math.e)

VMEM_LIMIT = 56 * 1024 * 1024

PROJ_ROWS = 256
ATT_TQ = 512
ATT_TK = 1024
ATT_AHEAD = 2
BAND_TQ = 256
BAND_AHEAD = 2
MERGE_ROWS = 256
MOE_ROWS = 256
GU_GROUP = 256
GU_GROUPS = 2 * D_MODEL // GU_GROUP
FINAL_ROWS = 256


def _lambda_init(layer_idx):
    return 0.8 - 0.6 * math.exp(-0.3 * layer_idx)


def _cparams(sem):
    return pltpu.CompilerParams(dimension_semantics=sem, vmem_limit_bytes=VMEM_LIMIT)


def _half_mask(shape, upper):
    lane = lax.broadcasted_iota(I32, shape, len(shape) - 1)
    return (lane >= D_HEAD_A) if upper else (lane < D_HEAD_A)


def _proj_kernel(x_ref, w_ref, cos_ref, s1_ref, s2_ref,
                 qa_ref, ka32_ref, ka16_ref, va32_ref, va16_ref, vat_ref,
                 qb_ref, kb32_ref, kb16_ref, vb32_ref, vb16_ref):
    xb = x_ref[...].astype(BF16)

    def group(g):
        return jnp.dot(xb, w_ref[:, g * GROUP:(g + 1) * GROUP], preferred_element_type=F32)

    cos = cos_ref[...]
    s1 = s1_ref[...]
    s2 = s2_ref[...]

    def rope(h):
        parts = []
        for j in range(GROUP // LANES):
            hj = h[:, j * LANES:(j + 1) * LANES]
            nxt = pltpu.roll(hj, LANES - ROPE_DIM // 2, 1)
            prv = pltpu.roll(hj, ROPE_DIM // 2, 1)
            parts.append(hj * cos + nxt * s1 + prv * s2)
        return jnp.concatenate(parts, axis=1)

    qa = rope(group(0)) * (D_HEAD_A ** -0.5 * LOG2_E)
    qa_ref[...] = qa.astype(BF16)
    ka = rope(group(1))
    ka32_ref[...] = ka
    ka16_ref[...] = ka.astype(BF16)
    va = group(2)
    va32_ref[...] = va
    va16_ref[...] = va.astype(BF16)
    vt = va.T
    rows = vt.shape[1]
    tail = jnp.concatenate([jnp.ones((1, rows), F32), jnp.zeros((VT_ROWS - PAIR - 1, rows), F32)], axis=0)
    parts = []
    for h in range(H_A):
        parts += [vt[h * PAIR:(h + 1) * PAIR], tail]
    vat_ref[...] = jnp.concatenate(parts, axis=0).astype(BF16)
    qb_ref[...] = (group(3) * (D_HEAD_B ** -0.5 * LOG2_E)).astype(BF16)
    kb = group(4)
    kb32_ref[...] = kb
    kb16_ref[...] = kb.astype(BF16)
    vb = group(5)
    vb32_ref[...] = vb
    vb16_ref[...] = vb.astype(BF16)


def _rope_tables(pos):
    half = ROPE_DIM // 2
    inv = ROPE_THETA ** (-jnp.arange(half, dtype=F32) * 2.0 / ROPE_DIM)
    ang = pos.astype(F32)[:, None] * inv[None, :]
    widen = lambda a: jnp.tile(a, (1, LANES // half))
    in_head = (jnp.arange(LANES, dtype=I32) % D_HEAD_A)[None, :]
    cos = jnp.where(in_head < ROPE_DIM, widen(jnp.cos(ang)), 1.0)
    sin = widen(jnp.sin(ang))
    s1 = jnp.where(in_head < half, -sin, 0.0)
    s2 = jnp.where((in_head >= half) & (in_head < ROPE_DIM), sin, 0.0)
    return cos, s1, s2


def _project(x, w16, pos):
    t = x.shape[0]
    cos, s1, s2 = _rope_tables(pos)
    row = lambda i: (i, 0)
    blk = lambda cols: pl.BlockSpec((PROJ_ROWS, cols), row)
    f32o = jax.ShapeDtypeStruct((t, GROUP), F32)
    b16o = jax.ShapeDtypeStruct((t, GROUP), BF16)
    return pl.pallas_call(
        _proj_kernel,
        grid=(t // PROJ_ROWS,),
        in_specs=[blk(D_MODEL), pl.BlockSpec(w16.shape, lambda i: (0, 0)),
                  blk(LANES), blk(LANES), blk(LANES)],
        out_specs=[blk(GROUP), blk(GROUP), blk(GROUP), blk(GROUP), blk(GROUP),
                   pl.BlockSpec((H_A * VT_ROWS, PROJ_ROWS), lambda i: (0, i)),
                   blk(GROUP), blk(GROUP), blk(GROUP), blk(GROUP), blk(GROUP)],
        out_shape=[b16o, f32o, b16o, f32o, b16o, jax.ShapeDtypeStruct((H_A * VT_ROWS, t), BF16),
                   b16o, f32o, b16o, f32o, b16o],
        compiler_params=_cparams(("parallel",)),
        name="proj",
    )(x, w16, cos, s1, s2)


def _subln(o, g, lam_init, axis):
    o = o * lax.rsqrt(jnp.mean(jnp.square(o), axis=axis, keepdims=True) + SUBLN_EPS)
    return o * g * (1.0 - lam_init)


def _half_only(qp, upper):
    return jnp.where(_half_mask(qp.shape, upper), qp, jnp.zeros_like(qp))


def _attn_a_prompt_kernel(qi_ref, kj_ref, lam_ref, q_ref, k_ref, vt_ref, dmask_ref, g_ref, o_ref,
                          m_ref, acc_ref, *, lam_init):
    s = pl.program_id(0)
    qi = qi_ref[s]
    kj = kj_ref[s]

    @pl.when(kj == 0)
    def _():
        m_ref[...] = jnp.full(m_ref.shape, -jnp.inf, F32)
        acc_ref[...] = jnp.zeros(acc_ref.shape, F32)

    def step(diagonal):
        def scores(c):
            h = c // 2
            kp = k_ref[:, h * PAIR:(h + 1) * PAIR]
            qz = _half_only(q_ref[:, h * PAIR:(h + 1) * PAIR], c % 2 == 1)
            st = lax.dot_general(kp, qz, (((1,), (1,)), ((), ())), preferred_element_type=F32)
            return st + dmask_ref[...] if diagonal else st

        pending = [scores(c) for c in range(ATT_AHEAD)]
        for c in range(N_COMP):
            st = pending.pop(0)
            if c + ATT_AHEAD < N_COMP:
                pending.append(scores(c + ATT_AHEAD))
            m_old = m_ref[c:c + 1, :]
            m_new = jnp.maximum(m_old, jnp.max(st, axis=0, keepdims=True))
            alpha = jnp.exp2(m_old - m_new)
            p = jnp.exp2(st - m_new).astype(BF16)
            h = c // 2
            pv = jnp.dot(vt_ref[h * VT_ROWS:(h + 1) * VT_ROWS, :], p, preferred_element_type=F32)
            acc_ref[c] = alpha * acc_ref[c] + pv
            m_ref[c:c + 1, :] = m_new

    last_kj = qi // (ATT_TK // ATT_TQ)

    @pl.when(kj < last_kj)
    def _():
        step(False)

    @pl.when(kj == last_kj)
    def _():
        step(True)
        lam = lam_ref[0]
        for h in range(H_A):
            a0 = acc_ref[2 * h]
            a1 = acc_ref[2 * h + 1]
            o0 = a0[:PAIR] / a0[PAIR:PAIR + 1]
            o1 = a1[:PAIR] / a1[PAIR:PAIR + 1]
            ot = _subln(o0 - lam * o1, g_ref[...], lam_init, 0)
            o_ref[:, h * PAIR:(h + 1) * PAIR] = ot.T.astype(BF16)


def _attn_a_prompt(qa16, ka16, vat16, lam, subln_g, s_len, lam_init):
    ratio = ATT_TK // ATT_TQ
    assert ATT_TK == ratio * ATT_TQ and s_len % ATT_TK == 0
    kchunk = jnp.arange(ATT_TK, dtype=I32)[None, :, None] // CHUNK
    qchunk = (jnp.arange(ratio, dtype=I32)[:, None, None] * ATT_TQ
              + jnp.arange(ATT_TQ, dtype=I32)[None, None, :]) // CHUNK
    dmask = jnp.where(kchunk <= qchunk, 0.0, NEG_INF).astype(F32)
    nq = s_len // ATT_TQ
    qi_tab, kj_tab = [], []
    for i in range(nq):
        for j in range(i // ratio + 1):
            qi_tab.append(i)
            kj_tab.append(j)
    qi_tab = jnp.asarray(qi_tab, I32)
    kj_tab = jnp.asarray(kj_tab, I32)
    grid_spec = pltpu.PrefetchScalarGridSpec(
        num_scalar_prefetch=2,
        grid=(int(qi_tab.shape[0]),),
        in_specs=[
            pl.BlockSpec(memory_space=pltpu.SMEM),
            pl.BlockSpec((ATT_TQ, GROUP), lambda s, qi, kj: (qi[s], 0)),
            pl.BlockSpec((ATT_TK, GROUP), lambda s, qi, kj: (kj[s], 0)),
            pl.BlockSpec((H_A * VT_ROWS, ATT_TK), lambda s, qi, kj: (0, kj[s])),
            pl.BlockSpec((None, ATT_TK, ATT_TQ), lambda s, qi, kj: (qi[s] % ratio, 0, 0)),
            pl.BlockSpec((PAIR, 1), lambda s, qi, kj: (0, 0)),
        ],
        out_specs=pl.BlockSpec((ATT_TQ, GROUP), lambda s, qi, kj: (qi[s], 0)),
        scratch_shapes=[pltpu.VMEM((N_COMP, ATT_TQ), F32),
                        pltpu.VMEM((N_COMP, VT_ROWS, ATT_TQ), F32)],
    )
    return pl.pallas_call(
        functools.partial(_attn_a_prompt_kernel, lam_init=lam_init),
        grid_spec=grid_spec,
        out_shape=jax.ShapeDtypeStruct((s_len, GROUP), BF16),
        compiler_params=_cparams(("arbitrary",)),
        name="attn_a_prompt",
    )(qi_tab, kj_tab, lam, qa16, ka16, vat16, dmask, subln_g.reshape(PAIR, 1))


def _attn_a_sample_kernel(lam_ref, q_ref, kn_ref, vn_ref, kc_ref, vc_ref, g_ref, o_ref, *, lam_init):
    lam = lam_ref[0]
    k_refs = (kc_ref, kn_ref)
    v_refs = (vc_ref, vn_ref)
    rounded = {}

    def bf16_cols(ref, h):
        if (id(ref), h) not in rounded:
            rounded[id(ref), h] = ref[:, h * PAIR:(h + 1) * PAIR].astype(BF16)
        return rounded[id(ref), h]

    def scores(c):
        cols = slice(c // 2 * PAIR, (c // 2 + 1) * PAIR)
        qz = _half_only(q_ref[:, cols], c % 2 == 1)
        return [lax.dot_general(qz, bf16_cols(k_ref, c // 2), (((1,), (1,)), ((), ())),
                                preferred_element_type=F32) for k_ref in k_refs]

    pending = [scores(c) for c in range(ATT_AHEAD)]
    outs = []
    for c in range(N_COMP):
        scs = pending.pop(0)
        if c + ATT_AHEAD < N_COMP:
            pending.append(scores(c + ATT_AHEAD))
        m = functools.reduce(jnp.maximum, [jnp.max(sc, axis=1, keepdims=True) for sc in scs])
        l = None
        o = None
        for sc, v_ref in zip(scs, v_refs):
            p = jnp.exp2(sc - m)
            ls = jnp.sum(p, axis=1, keepdims=True)
            os_ = jnp.dot(p.astype(BF16), bf16_cols(v_ref, c // 2), preferred_element_type=F32)
            l = ls if l is None else l + ls
            o = os_ if o is None else o + os_
        outs.append(o / l)
    for h in range(H_A):
        o = _subln(outs[2 * h] - lam * outs[2 * h + 1], g_ref[...], lam_init, 1)
        o_ref[:, h * PAIR:(h + 1) * PAIR] = o.astype(BF16)


def _attn_a_sample(qa16, ka16, va16, cache_k, cache_v, lam, subln_g, row0, lam_init):
    nb, past, _ = cache_k.shape
    sd = CHUNK
    new = lambda b: (row0 // sd + b, 0)
    return pl.pallas_call(
        functools.partial(_attn_a_sample_kernel, lam_init=lam_init),
        grid=(nb,),
        in_specs=[
            pl.BlockSpec(memory_space=pltpu.SMEM),
            pl.BlockSpec((sd, GROUP), new), pl.BlockSpec((sd, GROUP), new), pl.BlockSpec((sd, GROUP), new),
            pl.BlockSpec((None, past, GROUP), lambda b: (b, 0, 0)),
            pl.BlockSpec((None, past, GROUP), lambda b: (b, 0, 0)),
            pl.BlockSpec((1, PAIR), lambda b: (0, 0)),
        ],
        out_specs=pl.BlockSpec((sd, GROUP), lambda b: (b, 0)),
        out_shape=jax.ShapeDtypeStruct((nb * sd, GROUP), BF16),
        compiler_params=_cparams(("parallel",)),
        name="attn_a_sample",
    )(lam, qa16, ka16, va16, cache_k, cache_v, subln_g.reshape(1, PAIR))


def _band_bias(rel_bias, n_q_chunks, first_pos):
    nq = n_q_chunks * CHUNK
    nk = nq + BAND_PAST
    heads = rel_bias.shape[0]
    n_lo = BAND_PAST - REL_CLIP
    period = max(nq + nk, n_lo + (2 * REL_CLIP + 1) + nq - 1)
    n_hi = period - (nq - 1) - n_lo - (2 * REL_CLIP + 1)
    assert n_lo >= 0
    lo = jnp.broadcast_to(rel_bias[:, :1], (heads, n_lo))
    hi = jnp.broadcast_to(rel_bias[:, -1:], (heads, n_hi))
    wrap = jnp.broadcast_to(rel_bias[:, :1], (heads, nq - 1))
    e = jnp.concatenate([lo, rel_bias, hi, wrap], axis=1).astype(F32)
    flat = jnp.broadcast_to(e[:, None, :], (heads, nq, period)).reshape(heads, nq * period)
    bias = flat[:, :nq * (period - 1)].reshape(heads, nq, period - 1)[:, :, :nk]
    qpos = jnp.arange(nq, dtype=I32)[:, None]
    kpos = jnp.arange(nk, dtype=I32)[None, :] - BAND_PAST
    dch = qpos // CHUNK - jnp.floor_divide(kpos, CHUNK)
    ok = (dch >= 0) & (dch <= BAND_CHUNKS) & (kpos + first_pos >= 0)
    return jnp.where(ok[None], bias * LOG2_E, NEG_INF)


def _band_heads(q_ref, ks, vs, bias_ref, o_ref):
    starts = [sum(k.shape[0] for k in ks[:n]) for n in range(len(ks))]

    def scores(head):
        cols = slice(head // 2 * PAIR, (head // 2 + 1) * PAIR)
        qz = _half_only(q_ref[:, cols], head % 2 == 1)
        out = []
        for k, col0 in zip(ks, starts):
            sc = lax.dot_general(qz, k[:, cols], (((1,), (1,)), ((), ())), preferred_element_type=F32)
            out.append(sc + bias_ref[head, :, col0:col0 + k.shape[0]])
        return out

    pending = [scores(h) for h in range(BAND_AHEAD)]
    res = []
    for head in range(H_B):
        scs = pending.pop(0)
        if head + BAND_AHEAD < H_B:
            pending.append(scores(head + BAND_AHEAD))
        cols = slice(head // 2 * PAIR, (head // 2 + 1) * PAIR)
        m = functools.reduce(jnp.maximum, [jnp.max(sc, axis=1, keepdims=True) for sc in scs])
        acc = None
        for sc, v in zip(scs, vs):
            vp = v[:, cols]
            v_and_ones = jnp.where(_half_mask(vp.shape, head % 2 == 1), vp, jnp.ones_like(vp))
            part = jnp.dot(jnp.exp2(sc - m).astype(BF16), v_and_ones, preferred_element_type=F32)
            acc = part if acc is None else acc + part
        res.append(acc)
    for j in range(H_B // 2):
        lower = _half_mask(res[2 * j].shape, False)
        num = jnp.where(lower, res[2 * j], res[2 * j + 1])
        den = jnp.where(lower, pltpu.roll(res[2 * j], D_HEAD_B, 1), pltpu.roll(res[2 * j + 1], D_HEAD_B, 1))
        o_ref[:, j * PAIR:(j + 1) * PAIR] = (num / den).astype(BF16)


def _attn_b_prompt_kernel(q_ref, k0_ref, k1_ref, k2_ref, v0_ref, v1_ref, v2_ref, bias_ref, o_ref):
    ks = [jnp.concatenate([k0_ref[...], k1_ref[...], k2_ref[...]], axis=0)]
    vs = [jnp.concatenate([v0_ref[...], v1_ref[...], v2_ref[...]], axis=0)]
    _band_heads(q_ref, ks, vs, bias_ref, o_ref)


def _attn_b_prompt(qb16, kb16, vb16, bias, s_len):
    assert BAND_PAST == 2 * BAND_TQ
    n_lead = bias.shape[0] - 1
    cur = lambda i: (i, 0)
    p1 = lambda i: (jnp.maximum(i - 1, 0), 0)
    p2 = lambda i: (jnp.maximum(i - 2, 0), 0)
    blk = lambda f: pl.BlockSpec((BAND_TQ, GROUP), f)
    return pl.pallas_call(
        _attn_b_prompt_kernel,
        grid=(s_len // BAND_TQ,),
        in_specs=[blk(cur), blk(p2), blk(p1), blk(cur), blk(p2), blk(p1), blk(cur),
                  pl.BlockSpec((None,) + bias.shape[1:], lambda i: (jnp.minimum(i, n_lead), 0, 0, 0))],
        out_specs=blk(cur),
        out_shape=jax.ShapeDtypeStruct((s_len, GROUP), BF16),
        compiler_params=_cparams(("parallel",)),
        name="attn_b_prompt",
    )(qb16, kb16, kb16, kb16, vb16, vb16, vb16, bias)


def _attn_b_sample_kernel(q_ref, kn_ref, vn_ref, kc_ref, vc_ref, bias_ref, o_ref):
    ks = [kc_ref[...].astype(BF16), kn_ref[...]]
    vs = [vc_ref[...].astype(BF16), vn_ref[...]]
    _band_heads(q_ref, ks, vs, bias_ref, o_ref)


def _attn_b_sample(qb16, kb16, vb16, cache_k, cache_v, bias, row0):
    nb, lb, _ = cache_k.shape
    sd = CHUNK
    new = lambda b: (row0 // sd + b, 0)
    return pl.pallas_call(
        _attn_b_sample_kernel,
        grid=(nb,),
        in_specs=[
            pl.BlockSpec((sd, GROUP), new), pl.BlockSpec((sd, GROUP), new), pl.BlockSpec((sd, GROUP), new),
            pl.BlockSpec((None, lb, GROUP), lambda b: (b, 0, 0)),
            pl.BlockSpec((None, lb, GROUP), lambda b: (b, 0, 0)),
            pl.BlockSpec(bias.shape, lambda b: (0, 0, 0)),
        ],
        out_specs=pl.BlockSpec((sd, GROUP), lambda b: (b, 0)),
        out_shape=jax.ShapeDtypeStruct((nb * sd, GROUP), BF16),
        compiler_params=_cparams(("parallel",)),
        name="attn_b_sample",
    )(qb16, kb16, vb16, cache_k, cache_v, bias)


def _layer_norm(z, g, b):
    mu = jnp.mean(z, axis=-1, keepdims=True)
    var = jnp.mean(jnp.square(z - mu), axis=-1, keepdims=True)
    return (z - mu) * lax.rsqrt(var + LN_EPS) * g + b


def _to_token_tiles(ref, x):
    rows = x.shape[0]
    for j in range(TT):
        ref[pl.ds(j, rows, stride=TT), :] = x[:, j * LANES:(j + 1) * LANES]


def _from_token_tiles(ref, rows):
    return jnp.concatenate([ref[pl.ds(j, rows, stride=TT), :] for j in range(TT)], axis=1)


def _merge_kernel(oap_ref, obp_ref, oas_ref, obs_ref, xp_ref, xs_ref, wo_ref, g_ref, b_ref, wr_ref,
                  br_ref, x1_ref, ti_ref, gate_ref, *, n_prompt_blocks):
    is_prompt = pl.program_id(0) < n_prompt_blocks
    oa = jnp.where(is_prompt, oap_ref[...], oas_ref[...])
    ob = jnp.where(is_prompt, obp_ref[...], obs_ref[...])
    x = jnp.where(is_prompt, xp_ref[...], xs_ref[...])
    mix = jnp.dot(oa, wo_ref[:GROUP, :], preferred_element_type=F32)
    mix = mix + jnp.dot(ob, wo_ref[GROUP:, :], preferred_element_type=F32)
    x1 = _layer_norm(DEEPNORM_ALPHA * x + mix, g_ref[...], b_ref[...])
    _to_token_tiles(x1_ref, x1)
    logits = jnp.dot(x1.astype(BF16), wr_ref[...], preferred_element_type=F32) + br_ref[...]
    lane = lax.broadcasted_iota(I32, logits.shape, 1)
    logits = jnp.where(lane < N_EXPERTS, logits, -jnp.inf)
    vals, idxs = [], []
    for _ in range(TOP_K):
        mx = jnp.max(logits, axis=1, keepdims=True)
        ix = jnp.min(jnp.where(logits == mx, lane, LANES), axis=1, keepdims=True)
        vals.append(mx)
        idxs.append(ix)
        logits = jnp.where(lane == ix, -jnp.inf, logits)
    col = lax.broadcasted_iota(I32, ti_ref.shape, 1)
    top_v = jnp.broadcast_to(vals[-1], ti_ref.shape)
    top_i = jnp.broadcast_to(idxs[-1], ti_ref.shape)
    for k in range(TOP_K - 1):
        top_v = jnp.where(col == k, vals[k], top_v)
        top_i = jnp.where(col == k, idxs[k], top_i)
    ti_ref[...] = top_i
    e = jnp.exp(top_v - vals[0])
    gate_ref[...] = e / jnp.sum(e, axis=1, keepdims=True)


def _merge(oa_p, ob_p, oa_s, ob_s, x_p, x_s, wo16, ln_g, ln_b, wr16, br):
    t = x_p.shape[0] + x_s.shape[0]
    npb = x_p.shape[0] // MERGE_ROWS
    row = lambda i: (i, 0)
    const = lambda i: (0, 0)
    blk = lambda cols: pl.BlockSpec((MERGE_ROWS, cols), row)
    prompt = lambda cols: pl.BlockSpec((MERGE_ROWS, cols), lambda i: (jnp.minimum(i, npb - 1), 0))
    sample = lambda cols: pl.BlockSpec((MERGE_ROWS, cols), lambda i: (jnp.maximum(i - npb, 0), 0))
    return pl.pallas_call(
        functools.partial(_merge_kernel, n_prompt_blocks=npb),
        grid=(t // MERGE_ROWS,),
        in_specs=[prompt(GROUP), prompt(GROUP), sample(GROUP), sample(GROUP),
                  prompt(D_MODEL), sample(D_MODEL),
                  pl.BlockSpec(wo16.shape, const), pl.BlockSpec((1, D_MODEL), const),
                  pl.BlockSpec((1, D_MODEL), const), pl.BlockSpec(wr16.shape, const),
                  pl.BlockSpec((1, LANES), const)],
        out_specs=[pl.BlockSpec((MERGE_ROWS * TT, LANES), row), blk(TOP_K), blk(TOP_K)],
        out_shape=[jax.ShapeDtypeStruct((t * TT, LANES), F32), jax.ShapeDtypeStruct((t, TOP_K), I32),
                   jax.ShapeDtypeStruct((t, TOP_K), F32)],
        compiler_params=_cparams(("parallel",)),
        name="merge",
    )(oa_p, ob_p, oa_s, ob_s, x_p, x_s, wo16, ln_g.reshape(1, D_MODEL), ln_b.reshape(1, D_MODEL),
      wr16, br)


def _moe_kernel(blk_e_ref, nused_ref, fresh_ref, wslot_ref, next_e_ref, idx_hbm, x_hbm, wgu_hbm, wd_hbm,
                perm_ref, bgu_ref, bd_ref, out_hbm, idx_smem, xbuf, ybuf, wgu32, wd32, wgu16_ref,
                wd16_ref, isem, gsem, ssem, wsem, *, n_tok):
    i = pl.program_id(0)
    nused = nused_ref[0]
    n_assign = n_tok * TOP_K
    last = nused - 1
    par = lax.rem(i, 2)

    def idx_copy(block, slot):
        return pltpu.make_async_copy(idx_hbm.at[block], idx_smem.at[slot], isem.at[slot])

    def start_gather(islot, xslot):
        for r in range(MOE_ROWS):
            src = pl.multiple_of(idx_smem[islot, 0, r], TT)
            pltpu.make_async_copy(x_hbm.at[pl.ds(src, TT)], xbuf.at[xslot, pl.ds(r * TT, TT)],
                                  gsem.at[xslot]).start()

    def wait_gather(xslot):
        pltpu.make_async_copy(x_hbm.at[pl.ds(0, MOE_ROWS * TT)], xbuf.at[xslot], gsem.at[xslot]).wait()

    def start_scatter(islot, yslot):
        for r in range(MOE_ROWS):
            dst = pl.multiple_of(idx_smem[islot, 1, r], TT)
            pltpu.make_async_copy(ybuf.at[yslot, pl.ds(r * TT, TT)], out_hbm.at[pl.ds(dst, TT)],
                                  ssem.at[yslot]).start()

    def weight_copies(expert, wslot):
        return (pltpu.make_async_copy(wgu_hbm.at[expert], wgu32.at[wslot], wsem.at[wslot, 0]),
                pltpu.make_async_copy(wd_hbm.at[expert], wd32.at[wslot], wsem.at[wslot, 1]))

    def slot_flush(yslot, row0):
        return pltpu.make_async_copy(ybuf.at[yslot], out_hbm.at[pl.ds(row0 * TT, MOE_ROWS * TT)],
                                     ssem.at[yslot])

    @pl.when(i == 0)
    def _():
        ybuf[...] = jnp.zeros(ybuf.shape, F32)
        for p in range(2):
            slot_flush(p, n_assign + p * MOE_ROWS).start()
        idx_copy(0, 0).start()
        for c in weight_copies(blk_e_ref[0], 0):
            c.start()
        idx_copy(0, 0).wait()
        start_gather(0, 0)
        idx_copy(jnp.minimum(1, last), 1).start()

    @pl.when(i < nused)
    def _():
        s_cur = lax.rem(i, 3)
        s_nxt = lax.rem(i + 1, 3)
        s_nn = lax.rem(i + 2, 3)
        idx_copy(0, s_nxt).wait()
        start_gather(s_nxt, 1 - par)
        idx_copy(jnp.minimum(i + 2, last), s_nn).start()

        wslot = wslot_ref[i]

        @pl.when(fresh_ref[i] == 1)
        def _():
            for c in weight_copies(0, wslot):
                c.wait()
            for c in weight_copies(next_e_ref[i], 1 - wslot):
                c.start()
            for g in range(GU_GROUPS):
                cols = slice(g * GU_GROUP, (g + 1) * GU_GROUP)
                sorted_cols = jnp.dot(wgu32[wslot, :, cols].astype(BF16), perm_ref[...],
                                      preferred_element_type=F32)
                wgu16_ref[:, cols] = sorted_cols.astype(BF16)
            wd16_ref[...] = wd32[wslot].astype(BF16)

        wait_gather(par)
        x = _from_token_tiles(xbuf.at[par], MOE_ROWS).astype(BF16)
        h = jnp.dot(x, wgu16_ref[...], preferred_element_type=F32) + bgu_ref[...]
        half = GU_GROUP // 2
        hg = jnp.concatenate([h[:, g * GU_GROUP:g * GU_GROUP + half] for g in range(GU_GROUPS)], axis=1)
        hu = jnp.concatenate([h[:, g * GU_GROUP + half:(g + 1) * GU_GROUP] for g in range(GU_GROUPS)],
                             axis=1)
        gate = jnp.minimum(hg, SWIGLU_LIMIT)
        up = jnp.clip(hu, -SWIGLU_LIMIT, SWIGLU_LIMIT)
        glu = gate * (1.0 / (1.0 + jnp.exp(-(gate * SWIGLU_ALPHA))))
        act = ((up + 1.0) * glu).astype(BF16)
        y = jnp.dot(act, wd16_ref[...], preferred_element_type=F32) + bd_ref[...]

        slot_flush(par, 0).wait()
        _to_token_tiles(ybuf.at[par], y)
        start_scatter(s_cur, par)

        @pl.when(i == last)
        def _():
            slot_flush(par, 0).wait()
            slot_flush(1 - par, 0).wait()
            wait_gather(1 - par)
            idx_copy(0, s_nn).wait()
            for c in weight_copies(0, 1 - wslot):
                c.wait()


def _gate_up_sorter():
    src = jnp.arange(GU_GROUP, dtype=I32)[:, None]
    dst = jnp.arange(GU_GROUP, dtype=I32)[None, :]
    half = GU_GROUP // 2
    return (src == jnp.where(dst < half, 2 * dst, 2 * (dst - half) + 1)).astype(BF16)


def _sort_gate_up(b):
    lead = b.shape[:-1]
    return b.reshape(lead + (GU_GROUPS, GU_GROUP // 2, 2)).swapaxes(-1, -2).reshape(lead + (-1,))


def _moe(x1, idx, blk_e, nused, fresh, wslot, next_e, w_gate_up, w_down, bgu, bd):
    t = x1.shape[0] // TT
    nblk = idx.shape[0]
    wmap = lambda i, be, nu, fr, ws, ne: (be[i], 0, 0)
    const = lambda i, be, nu, fr, ws, ne: (0, 0)
    grid_spec = pltpu.PrefetchScalarGridSpec(
        num_scalar_prefetch=5,
        grid=(nblk,),
        in_specs=[
            pl.BlockSpec(memory_space=pl.ANY),
            pl.BlockSpec(memory_space=pl.ANY),
            pl.BlockSpec(memory_space=pl.ANY),
            pl.BlockSpec(memory_space=pl.ANY),
            pl.BlockSpec((GU_GROUP, GU_GROUP), const),
            pl.BlockSpec((None, 1, 2 * D_MODEL), wmap),
            pl.BlockSpec((None, 1, D_MODEL), wmap),
        ],
        out_specs=pl.BlockSpec(memory_space=pl.ANY),
        scratch_shapes=[
            pltpu.SMEM((3, 2, MOE_ROWS), I32),
            pltpu.VMEM((2, MOE_ROWS * TT, LANES), F32),
            pltpu.VMEM((2, MOE_ROWS * TT, LANES), F32),
            pltpu.VMEM((2, D_MODEL, 2 * D_MODEL), F32),
            pltpu.VMEM((2, D_MODEL, D_MODEL), F32),
            pltpu.VMEM((D_MODEL, 2 * D_MODEL), BF16),
            pltpu.VMEM((D_MODEL, D_MODEL), BF16),
            pltpu.SemaphoreType.DMA((3,)),
            pltpu.SemaphoreType.DMA((2,)),
            pltpu.SemaphoreType.DMA((2,)),
            pltpu.SemaphoreType.DMA((2, 2)),
        ],
    )
    return pl.pallas_call(
        functools.partial(_moe_kernel, n_tok=t),
        grid_spec=grid_spec,
        out_shape=jax.ShapeDtypeStruct(((t * TOP_K + 2 * MOE_ROWS) * TT, LANES), F32),
        compiler_params=_cparams(("arbitrary",)),
        name="moe",
    )(blk_e, nused, fresh, wslot, next_e, idx, x1, w_gate_up, w_down, _gate_up_sorter(), bgu, bd)


def _route(top_i):
    t = top_i.shape[0]
    n = t * TOP_K
    nblk = n // MOE_ROWS + N_EXPERTS
    flat_e = top_i.reshape(-1)
    order = jnp.argsort(flat_e, stable=True).astype(I32)
    experts = jnp.arange(N_EXPERTS, dtype=I32)
    counts = jnp.sum((flat_e[:, None] == experts[None, :]).astype(I32), axis=0)
    padded = ((counts + MOE_ROWS - 1) // MOE_ROWS) * MOE_ROWS
    pad_end = jnp.cumsum(padded)
    pad_start = pad_end - padded
    start = jnp.cumsum(counts) - counts
    nused = (pad_end[-1] // MOE_ROWS).astype(I32).reshape(1)
    blk_first = jnp.arange(nblk, dtype=I32) * MOE_ROWS
    blk_e = jnp.minimum(jnp.sum((blk_first[:, None] >= pad_end[None, :]).astype(I32), axis=1),
                        N_EXPERTS - 1)
    row = jnp.arange(MOE_ROWS, dtype=I32)[None, :]
    j = blk_first[:, None] + row - pad_start[blk_e][:, None]
    valid = j < counts[blk_e][:, None]
    a = order[jnp.clip(start[blk_e][:, None] + j, 0, n - 1)]
    tok = lax.shift_right_logical(a, TOP_SHIFT)
    src = jnp.where(valid, tok, 0)
    dump = n + (jnp.arange(nblk, dtype=I32)[:, None] % 2) * MOE_ROWS + row
    dst = jnp.where(valid, (a & (TOP_K - 1)) * t + tok, dump)
    blk_e = blk_e.astype(I32)
    fresh = jnp.concatenate([jnp.ones((1,), I32), (blk_e[1:] != blk_e[:-1]).astype(I32)])
    wslot = (jnp.cumsum(fresh) - 1) % 2
    blk = jnp.arange(nblk, dtype=I32)
    nxt = lax.cummin(jnp.where(fresh == 1, blk, nblk)[::-1])[::-1]
    nxt = jnp.concatenate([nxt[1:], jnp.full((1,), nblk, I32)])
    next_e = jnp.where(nxt < nblk, blk_e[jnp.minimum(nxt, nblk - 1)], blk_e)
    return (jnp.stack([src * TT, dst * TT], axis=1).astype(I32), blk_e, nused, fresh,
            wslot.astype(I32), next_e.astype(I32))


def _final_kernel(x1_ref, gate_ref, y0_ref, y1_ref, y2_ref, y3_ref, g_ref, b_ref, o_ref):
    gates = gate_ref[...]
    y = gates[:, 0:1] * _from_token_tiles(y0_ref, FINAL_ROWS)
    for k, ref in enumerate((y1_ref, y2_ref, y3_ref), start=1):
        y = y + gates[:, k:k + 1] * _from_token_tiles(ref, FINAL_ROWS)
    x1 = _from_token_tiles(x1_ref, FINAL_ROWS)
    o_ref[...] = _layer_norm(DEEPNORM_ALPHA * x1 + y, g_ref[...], b_ref[...])


def _final(x1, gates, planes, ln_g, ln_b, row0, rows):
    nb_all = gates.shape[0] // FINAL_ROWS
    b0 = row0 // FINAL_ROWS
    row = lambda i: (b0 + i, 0)
    const = lambda i: (0, 0)
    tiles = lambda f: pl.BlockSpec((FINAL_ROWS * TT, LANES), f)
    plane = lambda k: tiles(lambda i: (k * nb_all + b0 + i, 0))
    return pl.pallas_call(
        _final_kernel,
        grid=(rows // FINAL_ROWS,),
        in_specs=[tiles(row), pl.BlockSpec((FINAL_ROWS, TOP_K), row),
                  plane(0), plane(1), plane(2), plane(3),
                  pl.BlockSpec((1, D_MODEL), const), pl.BlockSpec((1, D_MODEL), const)],
        out_specs=pl.BlockSpec((FINAL_ROWS, D_MODEL), lambda i: (i, 0)),
        out_shape=jax.ShapeDtypeStruct((rows, D_MODEL), F32),
        compiler_params=_cparams(("parallel",)),
        name="final",
    )(x1, gates, planes, planes, planes, planes, ln_g.reshape(1, D_MODEL), ln_b.reshape(1, D_MODEL))


def kernel(x_prompt, x_sample, cache_a_k, cache_a_v, cache_b_k, cache_b_v, w_in, lambda_qk, subln_g,
           rel_bias, w_out, ln1_g, ln1_b, w_router, b_router, w_gate_up, b_gate_up, w_down, b_down,
           ln2_g, ln2_b):
    b, s, d = x_prompt.shape
    bd, sd, _ = x_sample.shape
    depth, _, past, _, _ = cache_a_k.shape
    lb = cache_b_k.shape[2]
    assert depth == 1 and b == 1 and d == D_MODEL
    assert sd == CHUNK and past % CHUNK == 0 and lb == BAND_PAST and s >= BAND_PAST
    assert s % ATT_TQ == 0 and s % BAND_TQ == 0
    ts = bd * sd
    t = s + ts
    lam_init = _lambda_init(0)

    x_p = x_prompt.reshape(s, d)
    x_s = x_sample.reshape(ts, d)
    pos_p = jnp.arange(s, dtype=I32)
    pos_s = jnp.tile(past + jnp.arange(sd, dtype=I32), bd)
    lq = lambda_qk[0].astype(F32)
    lam = (jnp.exp(jnp.sum(lq[0] * lq[1])) - jnp.exp(jnp.sum(lq[2] * lq[3])) + lam_init).reshape(1)

    w_in16 = w_in[0].astype(BF16)
    (qa_p, ka32_p, ka_p, va32_p, _, vat_p, qb_p, kb32_p, kb_p, vb32_p, vb_p) = _project(x_p, w_in16, pos_p)
    (qa_s, ka32_s, ka_s, va32_s, va_s, _, qb_s, kb32_s, kb_s, vb32_s, vb_s) = _project(x_s, w_in16, pos_s)

    oa_p = _attn_a_prompt(qa_p, ka_p, vat_p, lam, subln_g[0], s, lam_init)
    oa_s = _attn_a_sample(qa_s, ka_s, va_s, cache_a_k[0].reshape(bd, past, GROUP),
                          cache_a_v[0].reshape(bd, past, GROUP), lam, subln_g[0], 0, lam_init)

    n_lead = BAND_PAST // BAND_TQ
    bias_p = jnp.stack([_band_bias(rel_bias[0], BAND_TQ // CHUNK, i * BAND_TQ) for i in range(n_lead + 1)])
    ob_p = _attn_b_prompt(qb_p, kb_p, vb_p, bias_p, s)
    ob_s = _attn_b_sample(qb_s, kb_s, vb_s, cache_b_k[0].reshape(bd, lb, GROUP),
                          cache_b_v[0].reshape(bd, lb, GROUP), _band_bias(rel_bias[0], 1, past), 0)

    wr16 = jnp.pad(w_router[0], ((0, 0), (0, LANES - N_EXPERTS))).astype(BF16)
    br = jnp.pad(b_router[0], (0, LANES - N_EXPERTS)).reshape(1, LANES)
    x1, top_i, gates = _merge(oa_p, ob_p, oa_s, ob_s, x_p, x_s, w_out[0].astype(BF16), ln1_g[0],
                              ln1_b[0], wr16, br)

    idx, blk_e, nused, fresh, wslot, next_e = _route(top_i)
    planes = _moe(x1, idx, blk_e, nused, fresh, wslot, next_e, w_gate_up[0], w_down[0],
                  _sort_gate_up(b_gate_up[0]).reshape(N_EXPERTS, 1, 2 * D_MODEL),
                  b_down[0].reshape(N_EXPERTS, 1, D_MODEL))
    y_p = _final(x1, gates, planes, ln2_g[0], ln2_b[0], 0, s)
    y_s = _final(x1, gates, planes, ln2_g[0], ln2_b[0], s, ts)

    heads_a = lambda a, n: a.reshape(1, n, -1, N_COMP, D_HEAD_A)
    vals_a = lambda a, n: a.reshape(1, n, -1, H_A, 2 * D_HEAD_A)
    heads_b = lambda a, n: a.reshape(1, n, -1, H_B, D_HEAD_B)
    keep_s = min(BAND_PAST, lb + sd)
    kb_new = jnp.concatenate([cache_b_k[0], heads_b(kb32_s, bd)[0]], axis=1)[:, lb + sd - keep_s:]
    vb_new = jnp.concatenate([cache_b_v[0], heads_b(vb32_s, bd)[0]], axis=1)[:, lb + sd - keep_s:]
    return (y_p.reshape(b, s, d), y_s.reshape(bd, sd, d),
            heads_a(ka32_p, b), vals_a(va32_p, b),
            heads_b(kb32_p[s - BAND_PAST:], b), heads_b(vb32_p[s - BAND_PAST:], b),
            heads_a(ka32_s, bd), vals_a(va32_s, bd),
            kb_new[None], vb_new[None])
```

```python
import functools
import math

import jax
import jax.numpy as jnp
from jax import lax
from jax.experimental import pallas as pl
from jax.experimental.pallas import tpu as pltpu

F32 = jnp.float32
BF16 = jnp.bfloat16
I32 = jnp.int32

D_MODEL = 1024
CHUNK = 64
D_HEAD_A = 64
H_A = 4
N_COMP = 2 * H_A
ROPE_DIM = D_HEAD_A // 4
ROPE_THETA = 500000.0
D_HEAD_B = 64
H_B = 8
BAND_CHUNKS = 8
BAND_PAST = BAND_CHUNKS * CHUNK
REL_CLIP = 128
GROUP = 512
N_EXPERTS = 32
TOP_K = 4
TOP_SHIFT = 2
SWIGLU_LIMIT = 7.0
SWIGLU_ALPHA = 1.702
LN_EPS = 1e-5
SUBLN_EPS = 1e-5
DEPTH = 1
DEEPNORM_ALPHA = (2.0 * DEPTH) ** 0.25
NEG_INF = -1e30
LANES = 128
PAIR = 2 * D_HEAD_A
TT = D_MODEL // LANES
VT_ROWS = PAIR + 16
LOG2_E = math.log2(math.e)

VMEM_LIMIT = 56 * 1024 * 1024

PROJ_ROWS = 256
ATT_TQ = 512
ATT_TK = 1024
ATT_AHEAD = 2
BAND_TQ = 256
BAND_AHEAD = 2
MERGE_ROWS = 512
MOE_ROWS = 256
DMA_QUEUES = 2
GU_GROUP = 256
GU_GROUPS = 2 * D_MODEL // GU_GROUP
FINAL_ROWS = 256


def _lambda_init(layer_idx):
    return 0.8 - 0.6 * math.exp(-0.3 * layer_idx)


def _cparams(sem):
    return pltpu.CompilerParams(dimension_semantics=sem, vmem_limit_bytes=VMEM_LIMIT)


def _half_mask(shape, upper):
    lane = lax.broadcasted_iota(I32, shape, len(shape) - 1)
    return (lane >= D_HEAD_A) if upper else (lane < D_HEAD_A)


def _proj_kernel(x_ref, w_ref, cos_ref, s1_ref, s2_ref,
                 qa_ref, ka32_ref, ka16_ref, va32_ref, va16_ref, vat_ref,
                 qb_ref, kb32_ref, kb16_ref, vb32_ref, vb16_ref):
    xb = x_ref[...].astype(BF16)

    def group(g):
        return jnp.dot(xb, w_ref[:, g * GROUP:(g + 1) * GROUP], preferred_element_type=F32)

    cos = cos_ref[...]
    s1 = s1_ref[...]
    s2 = s2_ref[...]

    def rope(h):
        parts = []
        for j in range(GROUP // LANES):
            hj = h[:, j * LANES:(j + 1) * LANES]
            nxt = pltpu.roll(hj, LANES - ROPE_DIM // 2, 1)
            prv = pltpu.roll(hj, ROPE_DIM // 2, 1)
            parts.append(hj * cos + nxt * s1 + prv * s2)
        return jnp.concatenate(parts, axis=1)

    qa = rope(group(0)) * (D_HEAD_A ** -0.5 * LOG2_E)
    qa_ref[...] = qa.astype(BF16)
    ka = rope(group(1))
    ka32_ref[...] = ka
    ka16_ref[...] = ka.astype(BF16)
    va = group(2)
    va32_ref[...] = va
    va16_ref[...] = va.astype(BF16)
    vt = va.T
    rows = vt.shape[1]
    tail = jnp.concatenate([jnp.ones((1, rows), F32), jnp.zeros((VT_ROWS - PAIR - 1, rows), F32)], axis=0)
    parts = []
    for h in range(H_A):
        parts += [vt[h * PAIR:(h + 1) * PAIR], tail]
    vat_ref[...] = jnp.concatenate(parts, axis=0).astype(BF16)
    qb_ref[...] = (group(3) * (D_HEAD_B ** -0.5 * LOG2_E)).astype(BF16)
    kb = group(4)
    kb32_ref[...] = kb
    kb16_ref[...] = kb.astype(BF16)
    vb = group(5)
    vb32_ref[...] = vb
    vb16_ref[...] = vb.astype(BF16)


def _rope_tables(pos):
    half = ROPE_DIM // 2
    inv = ROPE_THETA ** (-jnp.arange(half, dtype=F32) * 2.0 / ROPE_DIM)
    ang = pos.astype(F32)[:, None] * inv[None, :]
    widen = lambda a: jnp.tile(a, (1, LANES // half))
    in_head = (jnp.arange(LANES, dtype=I32) % D_HEAD_A)[None, :]
    cos = jnp.where(in_head < ROPE_DIM, widen(jnp.cos(ang)), 1.0)
    sin = widen(jnp.sin(ang))
    s1 = jnp.where(in_head < half, -sin, 0.0)
    s2 = jnp.where((in_head >= half) & (in_head < ROPE_DIM), sin, 0.0)
    return cos, s1, s2


def _project(x, w16, pos):
    t = x.shape[0]
    cos, s1, s2 = _rope_tables(pos)
    row = lambda i: (i, 0)
    blk = lambda cols: pl.BlockSpec((PROJ_ROWS, cols), row)
    f32o = jax.ShapeDtypeStruct((t, GROUP), F32)
    b16o = jax.ShapeDtypeStruct((t, GROUP), BF16)
    return pl.pallas_call(
        _proj_kernel,
        grid=(t // PROJ_ROWS,),
        in_specs=[blk(D_MODEL), pl.BlockSpec(w16.shape, lambda i: (0, 0)),
                  blk(LANES), blk(LANES), blk(LANES)],
        out_specs=[blk(GROUP), blk(GROUP), blk(GROUP), blk(GROUP), blk(GROUP),
                   pl.BlockSpec((H_A * VT_ROWS, PROJ_ROWS), lambda i: (0, i)),
                   blk(GROUP), blk(GROUP), blk(GROUP), blk(GROUP), blk(GROUP)],
        out_shape=[b16o, f32o, b16o, f32o, b16o, jax.ShapeDtypeStruct((H_A * VT_ROWS, t), BF16),
                   b16o, f32o, b16o, f32o, b16o],
        compiler_params=_cparams(("parallel",)),
        name="proj",
    )(x, w16, cos, s1, s2)


def _subln(o, g, lam_init, axis):
    o = o * lax.rsqrt(jnp.mean(jnp.square(o), axis=axis, keepdims=True) + SUBLN_EPS)
    return o * g * (1.0 - lam_init)


def _half_only(qp, upper):
    return jnp.where(_half_mask(qp.shape, upper), qp, jnp.zeros_like(qp))


def _attn_a_prompt_kernel(qi_ref, kj_ref, lam_ref, q_ref, k_ref, vt_ref, dmask_ref, g_ref, o_ref,
                          m_ref, acc_ref, *, lam_init):
    s = pl.program_id(0)
    qi = qi_ref[s]
    kj = kj_ref[s]

    @pl.when(kj == 0)
    def _():
        m_ref[...] = jnp.full(m_ref.shape, -jnp.inf, F32)
        acc_ref[...] = jnp.zeros(acc_ref.shape, F32)

    def step(diagonal):
        def scores(c):
            h = c // 2
            kp = k_ref[:, h * PAIR:(h + 1) * PAIR]
            qz = _half_only(q_ref[:, h * PAIR:(h + 1) * PAIR], c % 2 == 1)
            st = lax.dot_general(kp, qz, (((1,), (1,)), ((), ())), preferred_element_type=F32)
            return st + dmask_ref[...] if diagonal else st

        pending = [scores(c) for c in range(ATT_AHEAD)]
        for c in range(N_COMP):
            st = pending.pop(0)
            if c + ATT_AHEAD < N_COMP:
                pending.append(scores(c + ATT_AHEAD))
            m_old = m_ref[c:c + 1, :]
            m_new = jnp.maximum(m_old, jnp.max(st, axis=0, keepdims=True))
            alpha = jnp.exp2(m_old - m_new)
            p = jnp.exp2(st - m_new).astype(BF16)
            h = c // 2
            pv = jnp.dot(vt_ref[h * VT_ROWS:(h + 1) * VT_ROWS, :], p, preferred_element_type=F32)
            acc_ref[c] = alpha * acc_ref[c] + pv
            m_ref[c:c + 1, :] = m_new

    last_kj = qi // (ATT_TK // ATT_TQ)

    @pl.when(kj < last_kj)
    def _():
        step(False)

    @pl.when(kj == last_kj)
    def _():
        step(True)
        lam = lam_ref[0]
        for h in range(H_A):
            a0 = acc_ref[2 * h]
            a1 = acc_ref[2 * h + 1]
            o0 = a0[:PAIR] / a0[PAIR:PAIR + 1]
            o1 = a1[:PAIR] / a1[PAIR:PAIR + 1]
            ot = _subln(o0 - lam * o1, g_ref[...], lam_init, 0)
            o_ref[:, h * PAIR:(h + 1) * PAIR] = ot.T.astype(BF16)


def _attn_a_prompt(qa16, ka16, vat16, lam, subln_g, s_len, lam_init):
    ratio = ATT_TK // ATT_TQ
    assert ATT_TK == ratio * ATT_TQ and s_len % ATT_TK == 0
    kchunk = jnp.arange(ATT_TK, dtype=I32)[None, :, None] // CHUNK
    qchunk = (jnp.arange(ratio, dtype=I32)[:, None, None] * ATT_TQ
              + jnp.arange(ATT_TQ, dtype=I32)[None, None, :]) // CHUNK
    dmask = jnp.where(kchunk <= qchunk, 0.0, NEG_INF).astype(F32)
    nq = s_len // ATT_TQ
    qi_tab, kj_tab = [], []
    for i in range(nq):
        for j in range(i // ratio + 1):
            qi_tab.append(i)
            kj_tab.append(j)
    qi_tab = jnp.asarray(qi_tab, I32)
    kj_tab = jnp.asarray(kj_tab, I32)
    grid_spec = pltpu.PrefetchScalarGridSpec(
        num_scalar_prefetch=2,
        grid=(int(qi_tab.shape[0]),),
        in_specs=[
            pl.BlockSpec(memory_space=pltpu.SMEM),
            pl.BlockSpec((ATT_TQ, GROUP), lambda s, qi, kj: (qi[s], 0)),
            pl.BlockSpec((ATT_TK, GROUP), lambda s, qi, kj: (kj[s], 0)),
            pl.BlockSpec((H_A * VT_ROWS, ATT_TK), lambda s, qi, kj: (0, kj[s])),
            pl.BlockSpec((None, ATT_TK, ATT_TQ), lambda s, qi, kj: (qi[s] % ratio, 0, 0)),
            pl.BlockSpec((PAIR, 1), lambda s, qi, kj: (0, 0)),
        ],
        out_specs=pl.BlockSpec((ATT_TQ, GROUP), lambda s, qi, kj: (qi[s], 0)),
        scratch_shapes=[pltpu.VMEM((N_COMP, ATT_TQ), F32),
                        pltpu.VMEM((N_COMP, VT_ROWS, ATT_TQ), F32)],
    )
    return pl.pallas_call(
        functools.partial(_attn_a_prompt_kernel, lam_init=lam_init),
        grid_spec=grid_spec,
        out_shape=jax.ShapeDtypeStruct((s_len, GROUP), BF16),
        compiler_params=_cparams(("arbitrary",)),
        name="attn_a_prompt",
    )(qi_tab, kj_tab, lam, qa16, ka16, vat16, dmask, subln_g.reshape(PAIR, 1))


def _attn_a_sample_kernel(lam_ref, q_ref, kn_ref, vn_ref, kc_ref, vc_ref, g_ref, o_ref, *, lam_init):
    lam = lam_ref[0]
    k_refs = (kc_ref, kn_ref)
    v_refs = (vc_ref, vn_ref)
    rounded = {}

    def bf16_cols(ref, h):
        if (id(ref), h) not in rounded:
            rounded[id(ref), h] = ref[:, h * PAIR:(h + 1) * PAIR].astype(BF16)
        return rounded[id(ref), h]

    def scores(c):
        cols = slice(c // 2 * PAIR, (c // 2 + 1) * PAIR)
        qz = _half_only(q_ref[:, cols], c % 2 == 1)
        return [lax.dot_general(qz, bf16_cols(k_ref, c // 2), (((1,), (1,)), ((), ())),
                                preferred_element_type=F32) for k_ref in k_refs]

    pending = [scores(c) for c in range(ATT_AHEAD)]
    outs = []
    for c in range(N_COMP):
        scs = pending.pop(0)
        if c + ATT_AHEAD < N_COMP:
            pending.append(scores(c + ATT_AHEAD))
        m = functools.reduce(jnp.maximum, [jnp.max(sc, axis=1, keepdims=True) for sc in scs])
        l = None
        o = None
        for sc, v_ref in zip(scs, v_refs):
            p = jnp.exp2(sc - m)
            ls = jnp.sum(p, axis=1, keepdims=True)
            os_ = jnp.dot(p.astype(BF16), bf16_cols(v_ref, c // 2), preferred_element_type=F32)
            l = ls if l is None else l + ls
            o = os_ if o is None else o + os_
        outs.append(o / l)
    for h in range(H_A):
        o = _subln(outs[2 * h] - lam * outs[2 * h + 1], g_ref[...], lam_init, 1)
        o_ref[:, h * PAIR:(h + 1) * PAIR] = o.astype(BF16)


def _attn_a_sample(qa16, ka16, va16, cache_k, cache_v, lam, subln_g, row0, lam_init):
    nb, past, _ = cache_k.shape
    sd = CHUNK
    new = lambda b: (row0 // sd + b, 0)
    return pl.pallas_call(
        functools.partial(_attn_a_sample_kernel, lam_init=lam_init),
        grid=(nb,),
        in_specs=[
            pl.BlockSpec(memory_space=pltpu.SMEM),
            pl.BlockSpec((sd, GROUP), new), pl.BlockSpec((sd, GROUP), new), pl.BlockSpec((sd, GROUP), new),
            pl.BlockSpec((None, past, GROUP), lambda b: (b, 0, 0)),
            pl.BlockSpec((None, past, GROUP), lambda b: (b, 0, 0)),
            pl.BlockSpec((1, PAIR), lambda b: (0, 0)),
        ],
        out_specs=pl.BlockSpec((sd, GROUP), lambda b: (b, 0)),
        out_shape=jax.ShapeDtypeStruct((nb * sd, GROUP), BF16),
        compiler_params=_cparams(("parallel",)),
        name="attn_a_sample",
    )(lam, qa16, ka16, va16, cache_k, cache_v, subln_g.reshape(1, PAIR))


def _band_bias(rel_bias, n_q_chunks, first_pos):
    nq = n_q_chunks * CHUNK
    nk = nq + BAND_PAST
    heads = rel_bias.shape[0]
    n_lo = BAND_PAST - REL_CLIP
    period = max(nq + nk, n_lo + (2 * REL_CLIP + 1) + nq - 1)
    n_hi = period - (nq - 1) - n_lo - (2 * REL_CLIP + 1)
    assert n_lo >= 0
    lo = jnp.broadcast_to(rel_bias[:, :1], (heads, n_lo))
    hi = jnp.broadcast_to(rel_bias[:, -1:], (heads, n_hi))
    wrap = jnp.broadcast_to(rel_bias[:, :1], (heads, nq - 1))
    e = jnp.concatenate([lo, rel_bias, hi, wrap], axis=1).astype(F32)
    flat = jnp.broadcast_to(e[:, None, :], (heads, nq, period)).reshape(heads, nq * period)
    bias = flat[:, :nq * (period - 1)].reshape(heads, nq, period - 1)[:, :, :nk]
    qpos = jnp.arange(nq, dtype=I32)[:, None]
    kpos = jnp.arange(nk, dtype=I32)[None, :] - BAND_PAST
    dch = qpos // CHUNK - jnp.floor_divide(kpos, CHUNK)
    ok = (dch >= 0) & (dch <= BAND_CHUNKS) & (kpos + first_pos >= 0)
    return jnp.where(ok[None], bias * LOG2_E, NEG_INF)


def _band_heads(q_ref, ks, vs, bias_ref, o_ref):
    starts = [sum(k.shape[0] for k in ks[:n]) for n in range(len(ks))]

    def scores(head):
        cols = slice(head // 2 * PAIR, (head // 2 + 1) * PAIR)
        qz = _half_only(q_ref[:, cols], head % 2 == 1)
        out = []
        for k, col0 in zip(ks, starts):
            sc = lax.dot_general(qz, k[:, cols], (((1,), (1,)), ((), ())), preferred_element_type=F32)
            out.append(sc + bias_ref[head, :, col0:col0 + k.shape[0]])
        return out

    pending = [scores(h) for h in range(BAND_AHEAD)]
    res = []
    for head in range(H_B):
        scs = pending.pop(0)
        if head + BAND_AHEAD < H_B:
            pending.append(scores(head + BAND_AHEAD))
        cols = slice(head // 2 * PAIR, (head // 2 + 1) * PAIR)
        m = functools.reduce(jnp.maximum, [jnp.max(sc, axis=1, keepdims=True) for sc in scs])
        acc = None
        for sc, v in zip(scs, vs):
            vp = v[:, cols]
            v_and_ones = jnp.where(_half_mask(vp.shape, head % 2 == 1), vp, jnp.ones_like(vp))
            part = jnp.dot(jnp.exp2(sc - m).astype(BF16), v_and_ones, preferred_element_type=F32)
            acc = part if acc is None else acc + part
        res.append(acc)
    for j in range(H_B // 2):
        lower = _half_mask(res[2 * j].shape, False)
        num = jnp.where(lower, res[2 * j], res[2 * j + 1])
        den = jnp.where(lower, pltpu.roll(res[2 * j], D_HEAD_B, 1), pltpu.roll(res[2 * j + 1], D_HEAD_B, 1))
        o_ref[:, j * PAIR:(j + 1) * PAIR] = (num / den).astype(BF16)


def _attn_b_prompt_kernel(q_ref, k0_ref, k1_ref, k2_ref, v0_ref, v1_ref, v2_ref, bias_ref, o_ref):
    ks = [jnp.concatenate([k0_ref[...], k1_ref[...], k2_ref[...]], axis=0)]
    vs = [jnp.concatenate([v0_ref[...], v1_ref[...], v2_ref[...]], axis=0)]
    _band_heads(q_ref, ks, vs, bias_ref, o_ref)


def _attn_b_prompt(qb16, kb16, vb16, bias, s_len):
    assert BAND_PAST == 2 * BAND_TQ
    n_lead = bias.shape[0] - 1
    cur = lambda i: (i, 0)
    p1 = lambda i: (jnp.maximum(i - 1, 0), 0)
    p2 = lambda i: (jnp.maximum(i - 2, 0), 0)
    blk = lambda f: pl.BlockSpec((BAND_TQ, GROUP), f)
    return pl.pallas_call(
        _attn_b_prompt_kernel,
        grid=(s_len // BAND_TQ,),
        in_specs=[blk(cur), blk(p2), blk(p1), blk(cur), blk(p2), blk(p1), blk(cur),
                  pl.BlockSpec((None,) + bias.shape[1:], lambda i: (jnp.minimum(i, n_lead), 0, 0, 0))],
        out_specs=blk(cur),
        out_shape=jax.ShapeDtypeStruct((s_len, GROUP), BF16),
        compiler_params=_cparams(("parallel",)),
        name="attn_b_prompt",
    )(qb16, kb16, kb16, kb16, vb16, vb16, vb16, bias)


def _attn_b_sample_kernel(q_ref, kn_ref, vn_ref, kc_ref, vc_ref, bias_ref, o_ref):
    ks = [kc_ref[...].astype(BF16), kn_ref[...]]
    vs = [vc_ref[...].astype(BF16), vn_ref[...]]
    _band_heads(q_ref, ks, vs, bias_ref, o_ref)


def _attn_b_sample(qb16, kb16, vb16, cache_k, cache_v, bias, row0):
    nb, lb, _ = cache_k.shape
    sd = CHUNK
    new = lambda b: (row0 // sd + b, 0)
    return pl.pallas_call(
        _attn_b_sample_kernel,
        grid=(nb,),
        in_specs=[
            pl.BlockSpec((sd, GROUP), new), pl.BlockSpec((sd, GROUP), new), pl.BlockSpec((sd, GROUP), new),
            pl.BlockSpec((None, lb, GROUP), lambda b: (b, 0, 0)),
            pl.BlockSpec((None, lb, GROUP), lambda b: (b, 0, 0)),
            pl.BlockSpec(bias.shape, lambda b: (0, 0, 0)),
        ],
        out_specs=pl.BlockSpec((sd, GROUP), lambda b: (b, 0)),
        out_shape=jax.ShapeDtypeStruct((nb * sd, GROUP), BF16),
        compiler_params=_cparams(("parallel",)),
        name="attn_b_sample",
    )(qb16, kb16, vb16, cache_k, cache_v, bias)


def _layer_norm(z, g, b):
    mu = jnp.mean(z, axis=-1, keepdims=True)
    var = jnp.mean(jnp.square(z - mu), axis=-1, keepdims=True)
    return (z - mu) * lax.rsqrt(var + LN_EPS) * g + b


def _to_token_tiles(ref, x):
    rows = x.shape[0]
    for j in range(TT):
        ref[pl.ds(j, rows, stride=TT), :] = x[:, j * LANES:(j + 1) * LANES]


def _from_token_tiles(ref, rows):
    return jnp.concatenate([ref[pl.ds(j, rows, stride=TT), :] for j in range(TT)], axis=1)


def _merge_kernel(oap_ref, obp_ref, oas_ref, obs_ref, xp_ref, xs_ref, wo_ref, g_ref, b_ref, wr_ref,
                  br_ref, x1_ref, ti_ref, gate_ref, *, n_prompt_blocks):
    is_prompt = pl.program_id(0) < n_prompt_blocks
    oa = jnp.where(is_prompt, oap_ref[...], oas_ref[...])
    ob = jnp.where(is_prompt, obp_ref[...], obs_ref[...])
    x = jnp.where(is_prompt, xp_ref[...], xs_ref[...])
    mix = jnp.dot(oa, wo_ref[:GROUP, :], preferred_element_type=F32)
    mix = mix + jnp.dot(ob, wo_ref[GROUP:, :], preferred_element_type=F32)
    x1 = _layer_norm(DEEPNORM_ALPHA * x + mix, g_ref[...], b_ref[...])
    _to_token_tiles(x1_ref, x1)
    logits = jnp.dot(x1.astype(BF16), wr_ref[...], preferred_element_type=F32) + br_ref[...]
    lane = lax.broadcasted_iota(I32, logits.shape, 1)
    logits = jnp.where(lane < N_EXPERTS, logits, -jnp.inf)
    vals, idxs = [], []
    for _ in range(TOP_K):
        mx = jnp.max(logits, axis=1, keepdims=True)
        ix = jnp.min(jnp.where(logits == mx, lane, LANES), axis=1, keepdims=True)
        vals.append(mx)
        idxs.append(ix)
        logits = jnp.where(lane == ix, -jnp.inf, logits)
    col = lax.broadcasted_iota(I32, ti_ref.shape, 1)
    top_v = jnp.broadcast_to(vals[-1], ti_ref.shape)
    top_i = jnp.broadcast_to(idxs[-1], ti_ref.shape)
    for k in range(TOP_K - 1):
        top_v = jnp.where(col == k, vals[k], top_v)
        top_i = jnp.where(col == k, idxs[k], top_i)
    ti_ref[...] = top_i
    e = jnp.exp(top_v - vals[0])
    gate_ref[...] = e / jnp.sum(e, axis=1, keepdims=True)


def _merge(oa_p, ob_p, oa_s, ob_s, x_p, x_s, wo16, ln_g, ln_b, wr16, br):
    t = x_p.shape[0] + x_s.shape[0]
    npb = x_p.shape[0] // MERGE_ROWS
    row = lambda i: (i, 0)
    const = lambda i: (0, 0)
    blk = lambda cols: pl.BlockSpec((MERGE_ROWS, cols), row)
    prompt = lambda cols: pl.BlockSpec((MERGE_ROWS, cols), lambda i: (jnp.minimum(i, npb - 1), 0))
    sample = lambda cols: pl.BlockSpec((MERGE_ROWS, cols), lambda i: (jnp.maximum(i - npb, 0), 0))
    return pl.pallas_call(
        functools.partial(_merge_kernel, n_prompt_blocks=npb),
        grid=(t // MERGE_ROWS,),
        in_specs=[prompt(GROUP), prompt(GROUP), sample(GROUP), sample(GROUP),
                  prompt(D_MODEL), sample(D_MODEL),
                  pl.BlockSpec(wo16.shape, const), pl.BlockSpec((1, D_MODEL), const),
                  pl.BlockSpec((1, D_MODEL), const), pl.BlockSpec(wr16.shape, const),
                  pl.BlockSpec((1, LANES), const)],
        out_specs=[pl.BlockSpec((MERGE_ROWS * TT, LANES), row), blk(TOP_K), blk(TOP_K)],
        out_shape=[jax.ShapeDtypeStruct((t * TT, LANES), F32), jax.ShapeDtypeStruct((t, TOP_K), I32),
                   jax.ShapeDtypeStruct((t, TOP_K), F32)],
        compiler_params=_cparams(("parallel",)),
        name="merge",
    )(oa_p, ob_p, oa_s, ob_s, x_p, x_s, wo16, ln_g.reshape(1, D_MODEL), ln_b.reshape(1, D_MODEL),
      wr16, br)


def _moe_kernel(blk_e_ref, nused_ref, fresh_ref, wslot_ref, next_e_ref, idx_hbm, x_hbm, wgu_hbm, wd_hbm,
                perm_ref, bgu_ref, bd_ref, out_hbm, idx_smem, xbuf, ybuf, wgu32, wd32, wgu16_ref,
                wd16_ref, isem, gsem, ssem, wsem, *, n_tok):
    i = pl.program_id(0)
    nused = nused_ref[0]
    n_assign = n_tok * TOP_K
    last = nused - 1
    par = lax.rem(i, 2)

    def idx_copy(block, slot):
        return pltpu.make_async_copy(idx_hbm.at[block], idx_smem.at[slot], isem.at[slot])

    def start_gather(islot, xslot):
        for r in range(MOE_ROWS):
            src = pl.multiple_of(idx_smem[islot, 0, r], TT)
            pltpu.make_async_copy(x_hbm.at[pl.ds(src, TT)], xbuf.at[xslot, pl.ds(r * TT, TT)],
                                  gsem.at[xslot]).start(priority=r % DMA_QUEUES)

    def wait_gather(xslot):
        pltpu.make_async_copy(x_hbm.at[pl.ds(0, MOE_ROWS * TT)], xbuf.at[xslot], gsem.at[xslot]).wait()

    def start_scatter(islot, yslot):
        for r in range(MOE_ROWS):
            dst = pl.multiple_of(idx_smem[islot, 1, r], TT)
            pltpu.make_async_copy(ybuf.at[yslot, pl.ds(r * TT, TT)], out_hbm.at[pl.ds(dst, TT)],
                                  ssem.at[yslot]).start(priority=r % DMA_QUEUES)

    def weight_copies(expert, wslot):
        return (pltpu.make_async_copy(wgu_hbm.at[expert], wgu32.at[wslot], wsem.at[wslot, 0]),
                pltpu.make_async_copy(wd_hbm.at[expert], wd32.at[wslot], wsem.at[wslot, 1]))

    def slot_flush(yslot, row0):
        return pltpu.make_async_copy(ybuf.at[yslot], out_hbm.at[pl.ds(row0 * TT, MOE_ROWS * TT)],
                                     ssem.at[yslot])

    @pl.when(i == 0)
    def _():
        ybuf[...] = jnp.zeros(ybuf.shape, F32)
        for p in range(2):
            slot_flush(p, n_assign + p * MOE_ROWS).start()
        idx_copy(0, 0).start()
        for c in weight_copies(blk_e_ref[0], 0):
            c.start()
        idx_copy(0, 0).wait()
        start_gather(0, 0)
        idx_copy(jnp.minimum(1, last), 1).start()

    @pl.when(i < nused)
    def _():
        s_cur = lax.rem(i, 3)
        s_nxt = lax.rem(i + 1, 3)
        s_nn = lax.rem(i + 2, 3)
        idx_copy(0, s_nxt).wait()
        start_gather(s_nxt, 1 - par)
        idx_copy(jnp.minimum(i + 2, last), s_nn).start()

        wslot = wslot_ref[i]

        @pl.when(fresh_ref[i] == 1)
        def _():
            for c in weight_copies(0, wslot):
                c.wait()
            for c in weight_copies(next_e_ref[i], 1 - wslot):
                c.start()
            for g in range(GU_GROUPS):
                cols = slice(g * GU_GROUP, (g + 1) * GU_GROUP)
                sorted_cols = jnp.dot(wgu32[wslot, :, cols].astype(BF16), perm_ref[...],
                                      preferred_element_type=F32)
                wgu16_ref[:, cols] = sorted_cols.astype(BF16)
            wd16_ref[...] = wd32[wslot].astype(BF16)

        wait_gather(par)
        x = _from_token_tiles(xbuf.at[par], MOE_ROWS).astype(BF16)
        h = jnp.dot(x, wgu16_ref[...], preferred_element_type=F32) + bgu_ref[...]
        half = GU_GROUP // 2
        hg = jnp.concatenate([h[:, g * GU_GROUP:g * GU_GROUP + half] for g in range(GU_GROUPS)], axis=1)
        hu = jnp.concatenate([h[:, g * GU_GROUP + half:(g + 1) * GU_GROUP] for g in range(GU_GROUPS)],
                             axis=1)
        gate = jnp.minimum(hg, SWIGLU_LIMIT)
        up = jnp.clip(hu, -SWIGLU_LIMIT, SWIGLU_LIMIT)
        glu = gate * (1.0 / (1.0 + jnp.exp(-(gate * SWIGLU_ALPHA))))
        act = ((up + 1.0) * glu).astype(BF16)
        y = jnp.dot(act, wd16_ref[...], preferred_element_type=F32) + bd_ref[...]

        slot_flush(par, 0).wait()
        _to_token_tiles(ybuf.at[par], y)
        start_scatter(s_cur, par)

        @pl.when(i == last)
        def _():
            slot_flush(par, 0).wait()
            slot_flush(1 - par, 0).wait()
            wait_gather(1 - par)
            idx_copy(0, s_nn).wait()
            for c in weight_copies(0, 1 - wslot):
                c.wait()


def _gate_up_sorter():
    src = jnp.arange(GU_GROUP, dtype=I32)[:, None]
    dst = jnp.arange(GU_GROUP, dtype=I32)[None, :]
    half = GU_GROUP // 2
    return (src == jnp.where(dst < half, 2 * dst, 2 * (dst - half) + 1)).astype(BF16)


def _sort_gate_up(b):
    lead = b.shape[:-1]
    return b.reshape(lead + (GU_GROUPS, GU_GROUP // 2, 2)).swapaxes(-1, -2).reshape(lead + (-1,))


def _moe(x1, idx, blk_e, nused, fresh, wslot, next_e, w_gate_up, w_down, bgu, bd):
    t = x1.shape[0] // TT
    nblk = idx.shape[0]
    wmap = lambda i, be, nu, fr, ws, ne: (be[i], 0, 0)
    const = lambda i, be, nu, fr, ws, ne: (0, 0)
    grid_spec = pltpu.PrefetchScalarGridSpec(
        num_scalar_prefetch=5,
        grid=(nblk,),
        in_specs=[
            pl.BlockSpec(memory_space=pl.ANY),
            pl.BlockSpec(memory_space=pl.ANY),
            pl.BlockSpec(memory_space=pl.ANY),
            pl.BlockSpec(memory_space=pl.ANY),
            pl.BlockSpec((GU_GROUP, GU_GROUP), const),
            pl.BlockSpec((None, 1, 2 * D_MODEL), wmap),
            pl.BlockSpec((None, 1, D_MODEL), wmap),
        ],
        out_specs=pl.BlockSpec(memory_space=pl.ANY),
        scratch_shapes=[
            pltpu.SMEM((3, 2, MOE_ROWS), I32),
            pltpu.VMEM((2, MOE_ROWS * TT, LANES), F32),
            pltpu.VMEM((2, MOE_ROWS * TT, LANES), F32),
            pltpu.VMEM((2, D_MODEL, 2 * D_MODEL), F32),
            pltpu.VMEM((2, D_MODEL, D_MODEL), F32),
            pltpu.VMEM((D_MODEL, 2 * D_MODEL), BF16),
            pltpu.VMEM((D_MODEL, D_MODEL), BF16),
            pltpu.SemaphoreType.DMA((3,)),
            pltpu.SemaphoreType.DMA((2,)),
            pltpu.SemaphoreType.DMA((2,)),
            pltpu.SemaphoreType.DMA((2, 2)),
        ],
    )
    return pl.pallas_call(
        functools.partial(_moe_kernel, n_tok=t),
        grid_spec=grid_spec,
        out_shape=jax.ShapeDtypeStruct(((t * TOP_K + 2 * MOE_ROWS) * TT, LANES), F32),
        compiler_params=_cparams(("arbitrary",)),
        name="moe",
    )(blk_e, nused, fresh, wslot, next_e, idx, x1, w_gate_up, w_down, _gate_up_sorter(), bgu, bd)


def _route(top_i):
    t = top_i.shape[0]
    n = t * TOP_K
    nblk = n // MOE_ROWS + N_EXPERTS
    flat_e = top_i.reshape(-1)
    order = jnp.argsort(flat_e, stable=True).astype(I32)
    experts = jnp.arange(N_EXPERTS, dtype=I32)
    counts = jnp.sum((flat_e[:, None] == experts[None, :]).astype(I32), axis=0)
    padded = ((counts + MOE_ROWS - 1) // MOE_ROWS) * MOE_ROWS
    pad_end = jnp.cumsum(padded)
    pad_start = pad_end - padded
    start = jnp.cumsum(counts) - counts
    nused = (pad_end[-1] // MOE_ROWS).astype(I32).reshape(1)
    blk_first = jnp.arange(nblk, dtype=I32) * MOE_ROWS
    blk_e = jnp.minimum(jnp.sum((blk_first[:, None] >= pad_end[None, :]).astype(I32), axis=1),
                        N_EXPERTS - 1)
    row = jnp.arange(MOE_ROWS, dtype=I32)[None, :]
    j = blk_first[:, None] + row - pad_start[blk_e][:, None]
    valid = j < counts[blk_e][:, None]
    a = order[jnp.clip(start[blk_e][:, None] + j, 0, n - 1)]
    tok = lax.shift_right_logical(a, TOP_SHIFT)
    src = jnp.where(valid, tok, 0)
    dump = n + (jnp.arange(nblk, dtype=I32)[:, None] % 2) * MOE_ROWS + row
    dst = jnp.where(valid, (a & (TOP_K - 1)) * t + tok, dump)
    blk_e = blk_e.astype(I32)
    fresh = jnp.concatenate([jnp.ones((1,), I32), (blk_e[1:] != blk_e[:-1]).astype(I32)])
    wslot = (jnp.cumsum(fresh) - 1) % 2
    blk = jnp.arange(nblk, dtype=I32)
    nxt = lax.cummin(jnp.where(fresh == 1, blk, nblk)[::-1])[::-1]
    nxt = jnp.concatenate([nxt[1:], jnp.full((1,), nblk, I32)])
    next_e = jnp.where(nxt < nblk, blk_e[jnp.minimum(nxt, nblk - 1)], blk_e)
    return (jnp.stack([src * TT, dst * TT], axis=1).astype(I32), blk_e, nused, fresh,
            wslot.astype(I32), next_e.astype(I32))


def _final_kernel(x1_ref, gate_ref, y0_ref, y1_ref, y2_ref, y3_ref, g_ref, b_ref, o_ref):
    gates = gate_ref[...]
    y = gates[:, 0:1] * _from_token_tiles(y0_ref, FINAL_ROWS)
    for k, ref in enumerate((y1_ref, y2_ref, y3_ref), start=1):
        y = y + gates[:, k:k + 1] * _from_token_tiles(ref, FINAL_ROWS)
    x1 = _from_token_tiles(x1_ref, FINAL_ROWS)
    o_ref[...] = _layer_norm(DEEPNORM_ALPHA * x1 + y, g_ref[...], b_ref[...])


def _final(x1, gates, planes, ln_g, ln_b, row0, rows):
    nb_all = gates.shape[0] // FINAL_ROWS
    b0 = row0 // FINAL_ROWS
    row = lambda i: (b0 + i, 0)
    const = lambda i: (0, 0)
    tiles = lambda f: pl.BlockSpec((FINAL_ROWS * TT, LANES), f)
    plane = lambda k: tiles(lambda i: (k * nb_all + b0 + i, 0))
    return pl.pallas_call(
        _final_kernel,
        grid=(rows // FINAL_ROWS,),
        in_specs=[tiles(row), pl.BlockSpec((FINAL_ROWS, TOP_K), row),
                  plane(0), plane(1), plane(2), plane(3),
                  pl.BlockSpec((1, D_MODEL), const), pl.BlockSpec((1, D_MODEL), const)],
        out_specs=pl.BlockSpec((FINAL_ROWS, D_MODEL), lambda i: (i, 0)),
        out_shape=jax.ShapeDtypeStruct((rows, D_MODEL), F32),
        compiler_params=_cparams(("parallel",)),
        name="final",
    )(x1, gates, planes, planes, planes, planes, ln_g.reshape(1, D_MODEL), ln_b.reshape(1, D_MODEL))


def kernel(x_prompt, x_sample, cache_a_k, cache_a_v, cache_b_k, cache_b_v, w_in, lambda_qk, subln_g,
           rel_bias, w_out, ln1_g, ln1_b, w_router, b_router, w_gate_up, b_gate_up, w_down, b_down,
           ln2_g, ln2_b):
    b, s, d = x_prompt.shape
    bd, sd, _ = x_sample.shape
    depth, _, past, _, _ = cache_a_k.shape
    lb = cache_b_k.shape[2]
    assert depth == 1 and b == 1 and d == D_MODEL
    assert sd == CHUNK and past % CHUNK == 0 and lb == BAND_PAST and s >= BAND_PAST
    assert s % ATT_TQ == 0 and s % BAND_TQ == 0
    ts = bd * sd
    t = s + ts
    lam_init = _lambda_init(0)

    x_p = x_prompt.reshape(s, d)
    x_s = x_sample.reshape(ts, d)
    pos_p = jnp.arange(s, dtype=I32)
    pos_s = jnp.tile(past + jnp.arange(sd, dtype=I32), bd)
    lq = lambda_qk[0].astype(F32)
    lam = (jnp.exp(jnp.sum(lq[0] * lq[1])) - jnp.exp(jnp.sum(lq[2] * lq[3])) + lam_init).reshape(1)

    w_in16 = w_in[0].astype(BF16)
    (qa_p, ka32_p, ka_p, va32_p, _, vat_p, qb_p, kb32_p, kb_p, vb32_p, vb_p) = _project(x_p, w_in16, pos_p)
    (qa_s, ka32_s, ka_s, va32_s, va_s, _, qb_s, kb32_s, kb_s, vb32_s, vb_s) = _project(x_s, w_in16, pos_s)

    oa_p = _attn_a_prompt(qa_p, ka_p, vat_p, lam, subln_g[0], s, lam_init)
    oa_s = _attn_a_sample(qa_s, ka_s, va_s, cache_a_k[0].reshape(bd, past, GROUP),
                          cache_a_v[0].reshape(bd, past, GROUP), lam, subln_g[0], 0, lam_init)

    n_lead = BAND_PAST // BAND_TQ
    bias_p = jnp.stack([_band_bias(rel_bias[0], BAND_TQ // CHUNK, i * BAND_TQ) for i in range(n_lead + 1)])
    ob_p = _attn_b_prompt(qb_p, kb_p, vb_p, bias_p, s)
    ob_s = _attn_b_sample(qb_s, kb_s, vb_s, cache_b_k[0].reshape(bd, lb, GROUP),
                          cache_b_v[0].reshape(bd, lb, GROUP), _band_bias(rel_bias[0], 1, past), 0)

    wr16 = jnp.pad(w_router[0], ((0, 0), (0, LANES - N_EXPERTS))).astype(BF16)
    br = jnp.pad(b_router[0], (0, LANES - N_EXPERTS)).reshape(1, LANES)
    x1, top_i, gates = _merge(oa_p, ob_p, oa_s, ob_s, x_p, x_s, w_out[0].astype(BF16), ln1_g[0],
                              ln1_b[0], wr16, br)

    idx, blk_e, nused, fresh, wslot, next_e = _route(top_i)
    planes = _moe(x1, idx, blk_e, nused, fresh, wslot, next_e, w_gate_up[0], w_down[0],
                  _sort_gate_up(b_gate_up[0]).reshape(N_EXPERTS, 1, 2 * D_MODEL),
                  b_down[0].reshape(N_EXPERTS, 1, D_MODEL))
    y_p = _final(x1, gates, planes, ln2_g[0], ln2_b[0], 0, s)
    y_s = _final(x1, gates, planes, ln2_g[0], ln2_b[0], s, ts)

    heads_a = lambda a, n: a.reshape(1, n, -1, N_COMP, D_HEAD_A)
    vals_a = lambda a, n: a.reshape(1, n, -1, H_A, 2 * D_HEAD_A)
    heads_b = lambda a, n: a.reshape(1, n, -1, H_B, D_HEAD_B)
    keep_s = min(BAND_PAST, lb + sd)
    kb_new = jnp.concatenate([cache_b_k[0], heads_b(kb32_s, bd)[0]], axis=1)[:, lb + sd - keep_s:]
    vb_new = jnp.concatenate([cache_b_v[0], heads_b(vb32_s, bd)[0]], axis=1)[:, lb + sd - keep_s:]
    return (y_p.reshape(b, s, d), y_s.reshape(bd, sd, d),
            heads_a(ka32_p, b), vals_a(va32_p, b),
            heads_b(kb32_p[s - BAND_PAST:], b), heads_b(vb32_p[s - BAND_PAST:], b),
            heads_a(ka32_s, bd), vals_a(va32_s, bd),
            kb_new[None], vb_new[None])
```

```python
import functools
import math

import jax
import jax.numpy as jnp
from jax import lax
from jax.experimental import pallas as pl
from jax.experimental.pallas import tpu as pltpu

F32 = jnp.float32
BF16 = jnp.bfloat16
I32 = jnp.int32

D_MODEL = 1024
CHUNK = 64
D_HEAD_A = 64
H_A = 4
N_COMP = 2 * H_A
ROPE_DIM = D_HEAD_A // 4
ROPE_THETA = 500000.0
D_HEAD_B = 64
H_B = 8
BAND_CHUNKS = 8
BAND_PAST = BAND_CHUNKS * CHUNK
REL_CLIP = 128
GROUP = 512
N_EXPERTS = 32
TOP_K = 4
TOP_SHIFT = 2
SWIGLU_LIMIT = 7.0
SWIGLU_ALPHA = 1.702
LN_EPS = 1e-5
SUBLN_EPS = 1e-5
DEPTH = 1
DEEPNORM_ALPHA = (2.0 * DEPTH) ** 0.25
NEG_INF = -1e30
LANES = 128
PAIR = 2 * D_HEAD_A
TT = D_MODEL // LANES
VT_ROWS = PAIR + 16
LOG2_E = math.log2(math.e)

VMEM_LIMIT = 56 * 1024 * 1024

PROJ_ROWS = 256
ATT_TQ = 512
ATT_TK = 1024
ATT_AHEAD = 2
BAND_TQ = 256
BAND_AHEAD = 2
MERGE_ROWS = 512
MOE_ROWS = 256
DMA_QUEUES = 2
GU_GROUP = 256
GU_GROUPS = 2 * D_MODEL // GU_GROUP
FINAL_ROWS = 256


def _lambda_init(layer_idx):
    return 0.8 - 0.6 * math.exp(-0.3 * layer_idx)


def _cparams(sem):
    return pltpu.CompilerParams(dimension_semantics=sem, vmem_limit_bytes=VMEM_LIMIT)


def _half_mask(shape, upper):
    lane = lax.broadcasted_iota(I32, shape, len(shape) - 1)
    return (lane >= D_HEAD_A) if upper else (lane < D_HEAD_A)


def _proj_kernel(x_ref, w_ref, cos_ref, s1_ref, s2_ref,
                 qa_ref, ka32_ref, ka16_ref, va32_ref, va16_ref, vat_ref,
                 qb_ref, kb32_ref, kb16_ref, vb32_ref, vb16_ref):
    xb = x_ref[...].astype(BF16)

    def group(g):
        return jnp.dot(xb, w_ref[:, g * GROUP:(g + 1) * GROUP], preferred_element_type=F32)

    cos = cos_ref[...]
    s1 = s1_ref[...]
    s2 = s2_ref[...]

    def rope(h):
        parts = []
        for j in range(GROUP // LANES):
            hj = h[:, j * LANES:(j + 1) * LANES]
            nxt = pltpu.roll(hj, LANES - ROPE_DIM // 2, 1)
            prv = pltpu.roll(hj, ROPE_DIM // 2, 1)
            parts.append(hj * cos + nxt * s1 + prv * s2)
        return jnp.concatenate(parts, axis=1)

    qa = rope(group(0)) * (D_HEAD_A ** -0.5 * LOG2_E)
    qa_ref[...] = qa.astype(BF16)
    ka = rope(group(1))
    ka32_ref[...] = ka
    ka16_ref[...] = ka.astype(BF16)
    va = group(2)
    va32_ref[...] = va
    va16_ref[...] = va.astype(BF16)
    vt = va.T
    rows = vt.shape[1]
    tail = jnp.concatenate([jnp.ones((1, rows), F32), jnp.zeros((VT_ROWS - PAIR - 1, rows), F32)], axis=0)
    parts = []
    for h in range(H_A):
        parts += [vt[h * PAIR:(h + 1) * PAIR], tail]
    vat_ref[...] = jnp.concatenate(parts, axis=0).astype(BF16)
    qb_ref[...] = (group(3) * (D_HEAD_B ** -0.5 * LOG2_E)).astype(BF16)
    kb = group(4)
    kb32_ref[...] = kb
    kb16_ref[...] = kb.astype(BF16)
    vb = group(5)
    vb32_ref[...] = vb
    vb16_ref[...] = vb.astype(BF16)


def _rope_tables(pos):
    half = ROPE_DIM // 2
    inv = ROPE_THETA ** (-jnp.arange(half, dtype=F32) * 2.0 / ROPE_DIM)
    ang = pos.astype(F32)[:, None] * inv[None, :]
    widen = lambda a: jnp.tile(a, (1, LANES // half))
    in_head = (jnp.arange(LANES, dtype=I32) % D_HEAD_A)[None, :]
    cos = jnp.where(in_head < ROPE_DIM, widen(jnp.cos(ang)), 1.0)
    sin = widen(jnp.sin(ang))
    s1 = jnp.where(in_head < half, -sin, 0.0)
    s2 = jnp.where((in_head >= half) & (in_head < ROPE_DIM), sin, 0.0)
    return cos, s1, s2


def _project(x, w16, pos):
    t = x.shape[0]
    cos, s1, s2 = _rope_tables(pos)
    row = lambda i: (i, 0)
    blk = lambda cols: pl.BlockSpec((PROJ_ROWS, cols), row)
    f32o = jax.ShapeDtypeStruct((t, GROUP), F32)
    b16o = jax.ShapeDtypeStruct((t, GROUP), BF16)
    return pl.pallas_call(
        _proj_kernel,
        grid=(t // PROJ_ROWS,),
        in_specs=[blk(D_MODEL), pl.BlockSpec(w16.shape, lambda i: (0, 0)),
                  blk(LANES), blk(LANES), blk(LANES)],
        out_specs=[blk(GROUP), blk(GROUP), blk(GROUP), blk(GROUP), blk(GROUP),
                   pl.BlockSpec((H_A * VT_ROWS, PROJ_ROWS), lambda i: (0, i)),
                   blk(GROUP), blk(GROUP), blk(GROUP), blk(GROUP), blk(GROUP)],
        out_shape=[b16o, f32o, b16o, f32o, b16o, jax.ShapeDtypeStruct((H_A * VT_ROWS, t), BF16),
                   b16o, f32o, b16o, f32o, b16o],
        compiler_params=_cparams(("parallel",)),
        name="proj",
    )(x, w16, cos, s1, s2)


def _subln(o, g, lam_init, axis):
    o = o * lax.rsqrt(jnp.mean(jnp.square(o), axis=axis, keepdims=True) + SUBLN_EPS)
    return o * g * (1.0 - lam_init)


def _half_only(qp, upper):
    return jnp.where(_half_mask(qp.shape, upper), qp, jnp.zeros_like(qp))


def _attn_a_prompt_kernel(qi_ref, kj_ref, lam_ref, q_ref, k_ref, vt_ref, dmask_ref, g_ref, o_ref,
                          m_ref, acc_ref, *, lam_init):
    s = pl.program_id(0)
    qi = qi_ref[s]
    kj = kj_ref[s]

    @pl.when(kj == 0)
    def _():
        m_ref[...] = jnp.full(m_ref.shape, -jnp.inf, F32)
        acc_ref[...] = jnp.zeros(acc_ref.shape, F32)

    def step(diagonal):
        def scores(c):
            h = c // 2
            kp = k_ref[:, h * PAIR:(h + 1) * PAIR]
            qz = _half_only(q_ref[:, h * PAIR:(h + 1) * PAIR], c % 2 == 1)
            st = lax.dot_general(kp, qz, (((1,), (1,)), ((), ())), preferred_element_type=F32)
            return st + dmask_ref[...] if diagonal else st

        pending = [scores(c) for c in range(ATT_AHEAD)]
        for c in range(N_COMP):
            st = pending.pop(0)
            if c + ATT_AHEAD < N_COMP:
                pending.append(scores(c + ATT_AHEAD))
            m_old = m_ref[c:c + 1, :]
            m_new = jnp.maximum(m_old, jnp.max(st, axis=0, keepdims=True))
            alpha = jnp.exp2(m_old - m_new)
            p = jnp.exp2(st - m_new).astype(BF16)
            h = c // 2
            pv = jnp.dot(vt_ref[h * VT_ROWS:(h + 1) * VT_ROWS, :], p, preferred_element_type=F32)
            acc_ref[c] = alpha * acc_ref[c] + pv
            m_ref[c:c + 1, :] = m_new

    last_kj = qi // (ATT_TK // ATT_TQ)

    @pl.when(kj < last_kj)
    def _():
        step(False)

    @pl.when(kj == last_kj)
    def _():
        step(True)
        lam = lam_ref[0]
        for h in range(H_A):
            a0 = acc_ref[2 * h]
            a1 = acc_ref[2 * h + 1]
            o0 = a0[:PAIR] / a0[PAIR:PAIR + 1]
            o1 = a1[:PAIR] / a1[PAIR:PAIR + 1]
            ot = _subln(o0 - lam * o1, g_ref[...], lam_init, 0)
            o_ref[:, h * PAIR:(h + 1) * PAIR] = ot.T.astype(BF16)


def _attn_a_prompt(qa16, ka16, vat16, lam, subln_g, s_len, lam_init):
    ratio = ATT_TK // ATT_TQ
    assert ATT_TK == ratio * ATT_TQ and s_len % ATT_TK == 0
    kchunk = jnp.arange(ATT_TK, dtype=I32)[None, :, None] // CHUNK
    qchunk = (jnp.arange(ratio, dtype=I32)[:, None, None] * ATT_TQ
              + jnp.arange(ATT_TQ, dtype=I32)[None, None, :]) // CHUNK
    dmask = jnp.where(kchunk <= qchunk, 0.0, NEG_INF).astype(F32)
    nq = s_len // ATT_TQ
    qi_tab, kj_tab = [], []
    for i in range(nq):
        for j in range(i // ratio + 1):
            qi_tab.append(i)
            kj_tab.append(j)
    qi_tab = jnp.asarray(qi_tab, I32)
    kj_tab = jnp.asarray(kj_tab, I32)
    grid_spec = pltpu.PrefetchScalarGridSpec(
        num_scalar_prefetch=2,
        grid=(int(qi_tab.shape[0]),),
        in_specs=[
            pl.BlockSpec(memory_space=pltpu.SMEM),
            pl.BlockSpec((ATT_TQ, GROUP), lambda s, qi, kj: (qi[s], 0)),
            pl.BlockSpec((ATT_TK, GROUP), lambda s, qi, kj: (kj[s], 0)),
            pl.BlockSpec((H_A * VT_ROWS, ATT_TK), lambda s, qi, kj: (0, kj[s])),
            pl.BlockSpec((None, ATT_TK, ATT_TQ), lambda s, qi, kj: (qi[s] % ratio, 0, 0)),
            pl.BlockSpec((PAIR, 1), lambda s, qi, kj: (0, 0)),
        ],
        out_specs=pl.BlockSpec((ATT_TQ, GROUP), lambda s, qi, kj: (qi[s], 0)),
        scratch_shapes=[pltpu.VMEM((N_COMP, ATT_TQ), F32),
                        pltpu.VMEM((N_COMP, VT_ROWS, ATT_TQ), F32)],
    )
    return pl.pallas_call(
        functools.partial(_attn_a_prompt_kernel, lam_init=lam_init),
        grid_spec=grid_spec,
        out_shape=jax.ShapeDtypeStruct((s_len, GROUP), BF16),
        compiler_params=_cparams(("arbitrary",)),
        name="attn_a_prompt",
    )(qi_tab, kj_tab, lam, qa16, ka16, vat16, dmask, subln_g.reshape(PAIR, 1))


def _attn_a_sample_kernel(lam_ref, q_ref, kn_ref, vn_ref, kc_ref, vc_ref, g_ref, o_ref, *, lam_init):
    lam = lam_ref[0]
    past = vc_ref.shape[0] // H_A
    per_head = {}

    def head_operands(h):
        if h not in per_head:
            rows = slice(h * PAIR, (h + 1) * PAIR)
            per_head[h] = (kc_ref[rows, :].astype(BF16), kn_ref[:, rows],
                           vc_ref[pl.ds(h, past, stride=H_A), :].astype(BF16), vn_ref[:, rows])
        return per_head[h]

    def scores(c):
        kt_cache, k_new, _, _ = head_operands(c // 2)
        qz = _half_only(q_ref[:, c // 2 * PAIR:(c // 2 + 1) * PAIR], c % 2 == 1)
        return [jnp.dot(qz, kt_cache, preferred_element_type=F32),
                lax.dot_general(qz, k_new, (((1,), (1,)), ((), ())), preferred_element_type=F32)]

    pending = [scores(c) for c in range(ATT_AHEAD)]
    outs = []
    for c in range(N_COMP):
        scs = pending.pop(0)
        if c + ATT_AHEAD < N_COMP:
            pending.append(scores(c + ATT_AHEAD))
        m = functools.reduce(jnp.maximum, [jnp.max(sc, axis=1, keepdims=True) for sc in scs])
        l = None
        o = None
        for sc, v in zip(scs, head_operands(c // 2)[2:]):
            p = jnp.exp2(sc - m)
            ls = jnp.sum(p, axis=1, keepdims=True)
            os_ = jnp.dot(p.astype(BF16), v, preferred_element_type=F32)
            l = ls if l is None else l + ls
            o = os_ if o is None else o + os_
        outs.append(o / l)
    for h in range(H_A):
        o = _subln(outs[2 * h] - lam * outs[2 * h + 1], g_ref[...], lam_init, 1)
        o_ref[:, h * PAIR:(h + 1) * PAIR] = o.astype(BF16)


def _attn_a_sample(qa16, ka16, va16, cache_k, cache_v, lam, subln_g, row0, lam_init):
    nb = cache_k.shape[0]
    sd = CHUNK
    new = lambda b: (row0 // sd + b, 0)
    return pl.pallas_call(
        functools.partial(_attn_a_sample_kernel, lam_init=lam_init),
        grid=(nb,),
        in_specs=[
            pl.BlockSpec(memory_space=pltpu.SMEM),
            pl.BlockSpec((sd, GROUP), new), pl.BlockSpec((sd, GROUP), new), pl.BlockSpec((sd, GROUP), new),
            pl.BlockSpec((None,) + cache_k.shape[1:], lambda b: (b, 0, 0)),
            pl.BlockSpec((None,) + cache_v.shape[1:], lambda b: (b, 0, 0)),
            pl.BlockSpec((1, PAIR), lambda b: (0, 0)),
        ],
        out_specs=pl.BlockSpec((sd, GROUP), lambda b: (b, 0)),
        out_shape=jax.ShapeDtypeStruct((nb * sd, GROUP), BF16),
        compiler_params=_cparams(("parallel",)),
        name="attn_a_sample",
    )(lam, qa16, ka16, va16, cache_k, cache_v, subln_g.reshape(1, PAIR))


def _band_bias(rel_bias, n_q_chunks, first_pos):
    nq = n_q_chunks * CHUNK
    nk = nq + BAND_PAST
    heads = rel_bias.shape[0]
    n_lo = BAND_PAST - REL_CLIP
    period = max(nq + nk, n_lo + (2 * REL_CLIP + 1) + nq - 1)
    n_hi = period - (nq - 1) - n_lo - (2 * REL_CLIP + 1)
    assert n_lo >= 0
    lo = jnp.broadcast_to(rel_bias[:, :1], (heads, n_lo))
    hi = jnp.broadcast_to(rel_bias[:, -1:], (heads, n_hi))
    wrap = jnp.broadcast_to(rel_bias[:, :1], (heads, nq - 1))
    e = jnp.concatenate([lo, rel_bias, hi, wrap], axis=1).astype(F32)
    flat = jnp.broadcast_to(e[:, None, :], (heads, nq, period)).reshape(heads, nq * period)
    bias = flat[:, :nq * (period - 1)].reshape(heads, nq, period - 1)[:, :, :nk]
    qpos = jnp.arange(nq, dtype=I32)[:, None]
    kpos = jnp.arange(nk, dtype=I32)[None, :] - BAND_PAST
    dch = qpos // CHUNK - jnp.floor_divide(kpos, CHUNK)
    ok = (dch >= 0) & (dch <= BAND_CHUNKS) & (kpos + first_pos >= 0)
    return jnp.where(ok[None], bias * LOG2_E, NEG_INF)


def _band_heads(q_ref, ks, vs, bias_ref, o_ref):
    starts = [sum(k.shape[0] for k in ks[:n]) for n in range(len(ks))]

    def scores(head):
        cols = slice(head // 2 * PAIR, (head // 2 + 1) * PAIR)
        qz = _half_only(q_ref[:, cols], head % 2 == 1)
        out = []
        for k, col0 in zip(ks, starts):
            sc = lax.dot_general(qz, k[:, cols], (((1,), (1,)), ((), ())), preferred_element_type=F32)
            out.append(sc + bias_ref[head, :, col0:col0 + k.shape[0]])
        return out

    pending = [scores(h) for h in range(BAND_AHEAD)]
    res = []
    for head in range(H_B):
        scs = pending.pop(0)
        if head + BAND_AHEAD < H_B:
            pending.append(scores(head + BAND_AHEAD))
        cols = slice(head // 2 * PAIR, (head // 2 + 1) * PAIR)
        m = functools.reduce(jnp.maximum, [jnp.max(sc, axis=1, keepdims=True) for sc in scs])
        acc = None
        for sc, v in zip(scs, vs):
            vp = v[:, cols]
            v_and_ones = jnp.where(_half_mask(vp.shape, head % 2 == 1), vp, jnp.ones_like(vp))
            part = jnp.dot(jnp.exp2(sc - m).astype(BF16), v_and_ones, preferred_element_type=F32)
            acc = part if acc is None else acc + part
        res.append(acc)
    for j in range(H_B // 2):
        lower = _half_mask(res[2 * j].shape, False)
        num = jnp.where(lower, res[2 * j], res[2 * j + 1])
        den = jnp.where(lower, pltpu.roll(res[2 * j], D_HEAD_B, 1), pltpu.roll(res[2 * j + 1], D_HEAD_B, 1))
        o_ref[:, j * PAIR:(j + 1) * PAIR] = (num / den).astype(BF16)


def _attn_b_prompt_kernel(q_ref, k0_ref, k1_ref, k2_ref, v0_ref, v1_ref, v2_ref, bias_ref, o_ref):
    ks = [jnp.concatenate([k0_ref[...], k1_ref[...], k2_ref[...]], axis=0)]
    vs = [jnp.concatenate([v0_ref[...], v1_ref[...], v2_ref[...]], axis=0)]
    _band_heads(q_ref, ks, vs, bias_ref, o_ref)


def _attn_b_prompt(qb16, kb16, vb16, bias, s_len):
    assert BAND_PAST == 2 * BAND_TQ
    n_lead = bias.shape[0] - 1
    cur = lambda i: (i, 0)
    p1 = lambda i: (jnp.maximum(i - 1, 0), 0)
    p2 = lambda i: (jnp.maximum(i - 2, 0), 0)
    blk = lambda f: pl.BlockSpec((BAND_TQ, GROUP), f)
    return pl.pallas_call(
        _attn_b_prompt_kernel,
        grid=(s_len // BAND_TQ,),
        in_specs=[blk(cur), blk(p2), blk(p1), blk(cur), blk(p2), blk(p1), blk(cur),
                  pl.BlockSpec((None,) + bias.shape[1:], lambda i: (jnp.minimum(i, n_lead), 0, 0, 0))],
        out_specs=blk(cur),
        out_shape=jax.ShapeDtypeStruct((s_len, GROUP), BF16),
        compiler_params=_cparams(("parallel",)),
        name="attn_b_prompt",
    )(qb16, kb16, kb16, kb16, vb16, vb16, vb16, bias)


def _attn_b_sample_kernel(q_ref, kn_ref, vn_ref, kc_ref, vc_ref, bias_ref, o_ref):
    ks = [kc_ref[...].astype(BF16), kn_ref[...]]
    vs = [vc_ref[...].astype(BF16), vn_ref[...]]
    _band_heads(q_ref, ks, vs, bias_ref, o_ref)


def _attn_b_sample(qb16, kb16, vb16, cache_k, cache_v, bias, row0):
    nb, lb, _ = cache_k.shape
    sd = CHUNK
    new = lambda b: (row0 // sd + b, 0)
    return pl.pallas_call(
        _attn_b_sample_kernel,
        grid=(nb,),
        in_specs=[
            pl.BlockSpec((sd, GROUP), new), pl.BlockSpec((sd, GROUP), new), pl.BlockSpec((sd, GROUP), new),
            pl.BlockSpec((None, lb, GROUP), lambda b: (b, 0, 0)),
            pl.BlockSpec((None, lb, GROUP), lambda b: (b, 0, 0)),
            pl.BlockSpec(bias.shape, lambda b: (0, 0, 0)),
        ],
        out_specs=pl.BlockSpec((sd, GROUP), lambda b: (b, 0)),
        out_shape=jax.ShapeDtypeStruct((nb * sd, GROUP), BF16),
        compiler_params=_cparams(("parallel",)),
        name="attn_b_sample",
    )(qb16, kb16, vb16, cache_k, cache_v, bias)


def _layer_norm(z, g, b):
    mu = jnp.mean(z, axis=-1, keepdims=True)
    var = jnp.mean(jnp.square(z - mu), axis=-1, keepdims=True)
    return (z - mu) * lax.rsqrt(var + LN_EPS) * g + b


def _to_token_tiles(ref, x):
    rows = x.shape[0]
    for j in range(TT):
        ref[pl.ds(j, rows, stride=TT), :] = x[:, j * LANES:(j + 1) * LANES]


def _from_token_tiles(ref, rows):
    return jnp.concatenate([ref[pl.ds(j, rows, stride=TT), :] for j in range(TT)], axis=1)


def _merge_kernel(oap_ref, obp_ref, oas_ref, obs_ref, xp_ref, xs_ref, wo_ref, g_ref, b_ref, wr_ref,
                  br_ref, x1_ref, ti_ref, gate_ref, *, n_prompt_blocks):
    is_prompt = pl.program_id(0) < n_prompt_blocks
    oa = jnp.where(is_prompt, oap_ref[...], oas_ref[...])
    ob = jnp.where(is_prompt, obp_ref[...], obs_ref[...])
    x = jnp.where(is_prompt, xp_ref[...], xs_ref[...])
    mix = jnp.dot(oa, wo_ref[:GROUP, :], preferred_element_type=F32)
    mix = mix + jnp.dot(ob, wo_ref[GROUP:, :], preferred_element_type=F32)
    x1 = _layer_norm(DEEPNORM_ALPHA * x + mix, g_ref[...], b_ref[...])
    _to_token_tiles(x1_ref, x1)
    logits = jnp.dot(x1.astype(BF16), wr_ref[...], preferred_element_type=F32) + br_ref[...]
    lane = lax.broadcasted_iota(I32, logits.shape, 1)
    logits = jnp.where(lane < N_EXPERTS, logits, -jnp.inf)
    vals, idxs = [], []
    for _ in range(TOP_K):
        mx = jnp.max(logits, axis=1, keepdims=True)
        ix = jnp.min(jnp.where(logits == mx, lane, LANES), axis=1, keepdims=True)
        vals.append(mx)
        idxs.append(ix)
        logits = jnp.where(lane == ix, -jnp.inf, logits)
    col = lax.broadcasted_iota(I32, ti_ref.shape, 1)
    top_v = jnp.broadcast_to(vals[-1], ti_ref.shape)
    top_i = jnp.broadcast_to(idxs[-1], ti_ref.shape)
    for k in range(TOP_K - 1):
        top_v = jnp.where(col == k, vals[k], top_v)
        top_i = jnp.where(col == k, idxs[k], top_i)
    ti_ref[...] = top_i
    e = jnp.exp(top_v - vals[0])
    gate_ref[...] = e / jnp.sum(e, axis=1, keepdims=True)


def _merge(oa_p, ob_p, oa_s, ob_s, x_p, x_s, wo16, ln_g, ln_b, wr16, br):
    t = x_p.shape[0] + x_s.shape[0]
    npb = x_p.shape[0] // MERGE_ROWS
    row = lambda i: (i, 0)
    const = lambda i: (0, 0)
    blk = lambda cols: pl.BlockSpec((MERGE_ROWS, cols), row)
    prompt = lambda cols: pl.BlockSpec((MERGE_ROWS, cols), lambda i: (jnp.minimum(i, npb - 1), 0))
    sample = lambda cols: pl.BlockSpec((MERGE_ROWS, cols), lambda i: (jnp.maximum(i - npb, 0), 0))
    return pl.pallas_call(
        functools.partial(_merge_kernel, n_prompt_blocks=npb),
        grid=(t // MERGE_ROWS,),
        in_specs=[prompt(GROUP), prompt(GROUP), sample(GROUP), sample(GROUP),
                  prompt(D_MODEL), sample(D_MODEL),
                  pl.BlockSpec(wo16.shape, const), pl.BlockSpec((1, D_MODEL), const),
                  pl.BlockSpec((1, D_MODEL), const), pl.BlockSpec(wr16.shape, const),
                  pl.BlockSpec((1, LANES), const)],
        out_specs=[pl.BlockSpec((MERGE_ROWS * TT, LANES), row), blk(TOP_K), blk(TOP_K)],
        out_shape=[jax.ShapeDtypeStruct((t * TT, LANES), F32), jax.ShapeDtypeStruct((t, TOP_K), I32),
                   jax.ShapeDtypeStruct((t, TOP_K), F32)],
        compiler_params=_cparams(("parallel",)),
        name="merge",
    )(oa_p, ob_p, oa_s, ob_s, x_p, x_s, wo16, ln_g.reshape(1, D_MODEL), ln_b.reshape(1, D_MODEL),
      wr16, br)


def _moe_kernel(blk_e_ref, nused_ref, fresh_ref, wslot_ref, next_e_ref, idx_hbm, x_hbm, wgu_hbm, wd_hbm,
                perm_ref, bgu_ref, bd_ref, out_hbm, idx_smem, xbuf, ybuf, wgu32, wd32, wgu16_ref,
                wd16_ref, isem, gsem, ssem, wsem, *, n_tok):
    i = pl.program_id(0)
    nused = nused_ref[0]
    n_assign = n_tok * TOP_K
    last = nused - 1
    par = lax.rem(i, 2)

    def idx_copy(block, slot):
        return pltpu.make_async_copy(idx_hbm.at[block], idx_smem.at[slot], isem.at[slot])

    def start_gather(islot, xslot):
        for r in range(MOE_ROWS):
            src = pl.multiple_of(idx_smem[islot, 0, r], TT)
            pltpu.make_async_copy(x_hbm.at[pl.ds(src, TT)], xbuf.at[xslot, pl.ds(r * TT, TT)],
                                  gsem.at[xslot]).start(priority=r % DMA_QUEUES)

    def wait_gather(xslot):
        pltpu.make_async_copy(x_hbm.at[pl.ds(0, MOE_ROWS * TT)], xbuf.at[xslot], gsem.at[xslot]).wait()

    def start_scatter(islot, yslot):
        for r in range(MOE_ROWS):
            dst = pl.multiple_of(idx_smem[islot, 1, r], TT)
            pltpu.make_async_copy(ybuf.at[yslot, pl.ds(r * TT, TT)], out_hbm.at[pl.ds(dst, TT)],
                                  ssem.at[yslot]).start(priority=r % DMA_QUEUES)

    def weight_copies(expert, wslot):
        return (pltpu.make_async_copy(wgu_hbm.at[expert], wgu32.at[wslot], wsem.at[wslot, 0]),
                pltpu.make_async_copy(wd_hbm.at[expert], wd32.at[wslot], wsem.at[wslot, 1]))

    def slot_flush(yslot, row0):
        return pltpu.make_async_copy(ybuf.at[yslot], out_hbm.at[pl.ds(row0 * TT, MOE_ROWS * TT)],
                                     ssem.at[yslot])

    @pl.when(i == 0)
    def _():
        ybuf[...] = jnp.zeros(ybuf.shape, F32)
        for p in range(2):
            slot_flush(p, n_assign + p * MOE_ROWS).start()
        idx_copy(0, 0).start()
        for c in weight_copies(blk_e_ref[0], 0):
            c.start()
        idx_copy(0, 0).wait()
        start_gather(0, 0)
        idx_copy(jnp.minimum(1, last), 1).start()

    @pl.when(i < nused)
    def _():
        s_cur = lax.rem(i, 3)
        s_nxt = lax.rem(i + 1, 3)
        s_nn = lax.rem(i + 2, 3)
        idx_copy(0, s_nxt).wait()
        start_gather(s_nxt, 1 - par)
        idx_copy(jnp.minimum(i + 2, last), s_nn).start()

        wslot = wslot_ref[i]

        @pl.when(fresh_ref[i] == 1)
        def _():
            for c in weight_copies(0, wslot):
                c.wait()
            for c in weight_copies(next_e_ref[i], 1 - wslot):
                c.start()
            for g in range(GU_GROUPS):
                cols = slice(g * GU_GROUP, (g + 1) * GU_GROUP)
                sorted_cols = jnp.dot(wgu32[wslot, :, cols].astype(BF16), perm_ref[...],
                                      preferred_element_type=F32)
                wgu16_ref[:, cols] = sorted_cols.astype(BF16)
            wd16_ref[...] = wd32[wslot].astype(BF16)

        wait_gather(par)
        x = _from_token_tiles(xbuf.at[par], MOE_ROWS).astype(BF16)
        h = jnp.dot(x, wgu16_ref[...], preferred_element_type=F32) + bgu_ref[...]
        half = GU_GROUP // 2
        hg = jnp.concatenate([h[:, g * GU_GROUP:g * GU_GROUP + half] for g in range(GU_GROUPS)], axis=1)
        hu = jnp.concatenate([h[:, g * GU_GROUP + half:(g + 1) * GU_GROUP] for g in range(GU_GROUPS)],
                             axis=1)
        gate = jnp.minimum(hg, SWIGLU_LIMIT)
        up = jnp.clip(hu, -SWIGLU_LIMIT, SWIGLU_LIMIT)
        glu = gate * (1.0 / (1.0 + jnp.exp(-(gate * SWIGLU_ALPHA))))
        act = ((up + 1.0) * glu).astype(BF16)
        y = jnp.dot(act, wd16_ref[...], preferred_element_type=F32) + bd_ref[...]

        slot_flush(par, 0).wait()
        _to_token_tiles(ybuf.at[par], y)
        start_scatter(s_cur, par)

        @pl.when(i == last)
        def _():
            slot_flush(par, 0).wait()
            slot_flush(1 - par, 0).wait()
            wait_gather(1 - par)
            idx_copy(0, s_nn).wait()
            for c in weight_copies(0, 1 - wslot):
                c.wait()


def _gate_up_sorter():
    src = jnp.arange(GU_GROUP, dtype=I32)[:, None]
    dst = jnp.arange(GU_GROUP, dtype=I32)[None, :]
    half = GU_GROUP // 2
    return (src == jnp.where(dst < half, 2 * dst, 2 * (dst - half) + 1)).astype(BF16)


def _sort_gate_up(b):
    lead = b.shape[:-1]
    return b.reshape(lead + (GU_GROUPS, GU_GROUP // 2, 2)).swapaxes(-1, -2).reshape(lead + (-1,))


def _moe(x1, idx, blk_e, nused, fresh, wslot, next_e, w_gate_up, w_down, bgu, bd):
    t = x1.shape[0] // TT
    nblk = idx.shape[0]
    wmap = lambda i, be, nu, fr, ws, ne: (be[i], 0, 0)
    const = lambda i, be, nu, fr, ws, ne: (0, 0)
    grid_spec = pltpu.PrefetchScalarGridSpec(
        num_scalar_prefetch=5,
        grid=(nblk,),
        in_specs=[
            pl.BlockSpec(memory_space=pl.ANY),
            pl.BlockSpec(memory_space=pl.ANY),
            pl.BlockSpec(memory_space=pl.ANY),
            pl.BlockSpec(memory_space=pl.ANY),
            pl.BlockSpec((GU_GROUP, GU_GROUP), const),
            pl.BlockSpec((None, 1, 2 * D_MODEL), wmap),
            pl.BlockSpec((None, 1, D_MODEL), wmap),
        ],
        out_specs=pl.BlockSpec(memory_space=pl.ANY),
        scratch_shapes=[
            pltpu.SMEM((3, 2, MOE_ROWS), I32),
            pltpu.VMEM((2, MOE_ROWS * TT, LANES), F32),
            pltpu.VMEM((2, MOE_ROWS * TT, LANES), F32),
            pltpu.VMEM((2, D_MODEL, 2 * D_MODEL), F32),
            pltpu.VMEM((2, D_MODEL, D_MODEL), F32),
            pltpu.VMEM((D_MODEL, 2 * D_MODEL), BF16),
            pltpu.VMEM((D_MODEL, D_MODEL), BF16),
            pltpu.SemaphoreType.DMA((3,)),
            pltpu.SemaphoreType.DMA((2,)),
            pltpu.SemaphoreType.DMA((2,)),
            pltpu.SemaphoreType.DMA((2, 2)),
        ],
    )
    return pl.pallas_call(
        functools.partial(_moe_kernel, n_tok=t),
        grid_spec=grid_spec,
        out_shape=jax.ShapeDtypeStruct(((t * TOP_K + 2 * MOE_ROWS) * TT, LANES), F32),
        compiler_params=_cparams(("arbitrary",)),
        name="moe",
    )(blk_e, nused, fresh, wslot, next_e, idx, x1, w_gate_up, w_down, _gate_up_sorter(), bgu, bd)


def _route(top_i):
    t = top_i.shape[0]
    n = t * TOP_K
    nblk = n // MOE_ROWS + N_EXPERTS
    flat_e = top_i.reshape(-1)
    order = jnp.argsort(flat_e, stable=True).astype(I32)
    experts = jnp.arange(N_EXPERTS, dtype=I32)
    counts = jnp.sum((flat_e[:, None] == experts[None, :]).astype(I32), axis=0)
    padded = ((counts + MOE_ROWS - 1) // MOE_ROWS) * MOE_ROWS
    pad_end = jnp.cumsum(padded)
    pad_start = pad_end - padded
    start = jnp.cumsum(counts) - counts
    nused = (pad_end[-1] // MOE_ROWS).astype(I32).reshape(1)
    blk_first = jnp.arange(nblk, dtype=I32) * MOE_ROWS
    blk_e = jnp.minimum(jnp.sum((blk_first[:, None] >= pad_end[None, :]).astype(I32), axis=1),
                        N_EXPERTS - 1)
    row = jnp.arange(MOE_ROWS, dtype=I32)[None, :]
    j = blk_first[:, None] + row - pad_start[blk_e][:, None]
    valid = j < counts[blk_e][:, None]
    a = order[jnp.clip(start[blk_e][:, None] + j, 0, n - 1)]
    tok = lax.shift_right_logical(a, TOP_SHIFT)
    src = jnp.where(valid, tok, 0)
    dump = n + (jnp.arange(nblk, dtype=I32)[:, None] % 2) * MOE_ROWS + row
    dst = jnp.where(valid, (a & (TOP_K - 1)) * t + tok, dump)
    blk_e = blk_e.astype(I32)
    fresh = jnp.concatenate([jnp.ones((1,), I32), (blk_e[1:] != blk_e[:-1]).astype(I32)])
    wslot = (jnp.cumsum(fresh) - 1) % 2
    blk = jnp.arange(nblk, dtype=I32)
    nxt = lax.cummin(jnp.where(fresh == 1, blk, nblk)[::-1])[::-1]
    nxt = jnp.concatenate([nxt[1:], jnp.full((1,), nblk, I32)])
    next_e = jnp.where(nxt < nblk, blk_e[jnp.minimum(nxt, nblk - 1)], blk_e)
    return (jnp.stack([src * TT, dst * TT], axis=1).astype(I32), blk_e, nused, fresh,
            wslot.astype(I32), next_e.astype(I32))


def _final_kernel(x1_ref, gate_ref, y0_ref, y1_ref, y2_ref, y3_ref, g_ref, b_ref, o_ref):
    gates = gate_ref[...]
    y = gates[:, 0:1] * _from_token_tiles(y0_ref, FINAL_ROWS)
    for k, ref in enumerate((y1_ref, y2_ref, y3_ref), start=1):
        y = y + gates[:, k:k + 1] * _from_token_tiles(ref, FINAL_ROWS)
    x1 = _from_token_tiles(x1_ref, FINAL_ROWS)
    o_ref[...] = _layer_norm(DEEPNORM_ALPHA * x1 + y, g_ref[...], b_ref[...])


def _final(x1, gates, planes, ln_g, ln_b, row0, rows):
    nb_all = gates.shape[0] // FINAL_ROWS
    b0 = row0 // FINAL_ROWS
    row = lambda i: (b0 + i, 0)
    const = lambda i: (0, 0)
    tiles = lambda f: pl.BlockSpec((FINAL_ROWS * TT, LANES), f)
    plane = lambda k: tiles(lambda i: (k * nb_all + b0 + i, 0))
    return pl.pallas_call(
        _final_kernel,
        grid=(rows // FINAL_ROWS,),
        in_specs=[tiles(row), pl.BlockSpec((FINAL_ROWS, TOP_K), row),
                  plane(0), plane(1), plane(2), plane(3),
                  pl.BlockSpec((1, D_MODEL), const), pl.BlockSpec((1, D_MODEL), const)],
        out_specs=pl.BlockSpec((FINAL_ROWS, D_MODEL), lambda i: (i, 0)),
        out_shape=jax.ShapeDtypeStruct((rows, D_MODEL), F32),
        compiler_params=_cparams(("parallel",)),
        name="final",
    )(x1, gates, planes, planes, planes, planes, ln_g.reshape(1, D_MODEL), ln_b.reshape(1, D_MODEL))


def kernel(x_prompt, x_sample, cache_a_k, cache_a_v, cache_b_k, cache_b_v, w_in, lambda_qk, subln_g,
           rel_bias, w_out, ln1_g, ln1_b, w_router, b_router, w_gate_up, b_gate_up, w_down, b_down,
           ln2_g, ln2_b):
    b, s, d = x_prompt.shape
    bd, sd, _ = x_sample.shape
    depth, _, past, _, _ = cache_a_k.shape
    lb = cache_b_k.shape[2]
    assert depth == 1 and b == 1 and d == D_MODEL
    assert sd == CHUNK and past % CHUNK == 0 and lb == BAND_PAST and s >= BAND_PAST
    assert s % ATT_TQ == 0 and s % BAND_TQ == 0
    ts = bd * sd
    t = s + ts
    lam_init = _lambda_init(0)

    x_p = x_prompt.reshape(s, d)
    x_s = x_sample.reshape(ts, d)
    pos_p = jnp.arange(s, dtype=I32)
    pos_s = jnp.tile(past + jnp.arange(sd, dtype=I32), bd)
    lq = lambda_qk[0].astype(F32)
    lam = (jnp.exp(jnp.sum(lq[0] * lq[1])) - jnp.exp(jnp.sum(lq[2] * lq[3])) + lam_init).reshape(1)

    w_in16 = w_in[0].astype(BF16)
    (qa_p, ka32_p, ka_p, va32_p, _, vat_p, qb_p, kb32_p, kb_p, vb32_p, vb_p) = _project(x_p, w_in16, pos_p)
    (qa_s, ka32_s, ka_s, va32_s, va_s, _, qb_s, kb32_s, kb_s, vb32_s, vb_s) = _project(x_s, w_in16, pos_s)

    oa_p = _attn_a_prompt(qa_p, ka_p, vat_p, lam, subln_g[0], s, lam_init)
    kt_a =jnp.transpose(cache_a_k[0], (0, 2, 3, 1)).reshape(bd, GROUP, past)
    oa_s = _attn_a_sample(qa_s, ka_s, va_s, kt_a, cache_a_v[0].reshape(bd, past * H_A, PAIR),
                          lam, subln_g[0], 0, lam_init)

    n_lead = BAND_PAST // BAND_TQ
    bias_p = jnp.stack([_band_bias(rel_bias[0], BAND_TQ // CHUNK, i * BAND_TQ) for i in range(n_lead + 1)])
    ob_p = _attn_b_prompt(qb_p, kb_p, vb_p, bias_p, s)
    ob_s = _attn_b_sample(qb_s, kb_s, vb_s, cache_b_k[0].reshape(bd, lb, GROUP),
                          cache_b_v[0].reshape(bd, lb, GROUP), _band_bias(rel_bias[0], 1, past), 0)

    wr16 = jnp.pad(w_router[0], ((0, 0), (0, LANES - N_EXPERTS))).astype(BF16)
    br = jnp.pad(b_router[0], (0, LANES - N_EXPERTS)).reshape(1, LANES)
    x1, top_i, gates = _merge(oa_p, ob_p, oa_s, ob_s, x_p, x_s, w_out[0].astype(BF16), ln1_g[0],
                              ln1_b[0], wr16, br)

    idx, blk_e, nused, fresh, wslot, next_e = _route(top_i)
    planes = _moe(x1, idx, blk_e, nused, fresh, wslot, next_e, w_gate_up[0], w_down[0],
                  _sort_gate_up(b_gate_up[0]).reshape(N_EXPERTS, 1, 2 * D_MODEL),
                  b_down[0].reshape(N_EXPERTS, 1, D_MODEL))
    y_p = _final(x1, gates, planes, ln2_g[0], ln2_b[0], 0, s)
    y_s = _final(x1, gates, planes, ln2_g[0], ln2_b[0], s, ts)

    heads_a = lambda a, n: a.reshape(1, n, -1, N_COMP, D_HEAD_A)
    vals_a = lambda a, n: a.reshape(1, n, -1, H_A, 2 * D_HEAD_A)
    heads_b = lambda a, n: a.reshape(1, n, -1, H_B, D_HEAD_B)
    keep_s = min(BAND_PAST, lb + sd)
    kb_new = jnp.concatenate([cache_b_k[0], heads_b(kb32_s, bd)[0]], axis=1)[:, lb + sd - keep_s:]
    vb_new = jnp.concatenate([cache_b_v[0], heads_b(vb32_s, bd)[0]], axis=1)[:, lb + sd - keep_s:]
    return (y_p.reshape(b, s, d), y_s.reshape(bd, sd, d),
            heads_a(ka32_p, b), vals_a(va32_p, b),
            heads_b(kb32_p[s - BAND_PAST:], b), heads_b(vb32_p[s - BAND_PAST:], b),
            heads_a(ka32_s, bd), vals_a(va32_s, bd),
            kb_new[None], vb_new[None])
```

```python
import functools
import math

import jax
import jax.numpy as jnp
from jax import lax
from jax.experimental import pallas as pl
from jax.experimental.pallas import tpu as pltpu

F32 = jnp.float32
BF16 = jnp.bfloat16
I32 = jnp.int32

D_MODEL = 1024
CHUNK = 64
D_HEAD_A = 64
H_A = 4
N_COMP = 2 * H_A
ROPE_DIM = D_HEAD_A // 4
ROPE_THETA = 500000.0
D_HEAD_B = 64
H_B = 8
BAND_CHUNKS = 8
BAND_PAST = BAND_CHUNKS * CHUNK
REL_CLIP = 128
GROUP = 512
N_EXPERTS = 32
TOP_K = 4
TOP_SHIFT = 2
SWIGLU_LIMIT = 7.0
SWIGLU_ALPHA = 1.702
LN_EPS = 1e-5
SUBLN_EPS = 1e-5
DEPTH = 1
DEEPNORM_ALPHA = (2.0 * DEPTH) ** 0.25
NEG_INF = -1e30
LANES = 128
PAIR = 2 * D_HEAD_A
TT = D_MODEL // LANES
VT_ROWS = PAIR + 16
LOG2_E = math.log2(math.e)

VMEM_LIMIT = 56 * 1024 * 1024

PROJ_ROWS = 256
ATT_TQ = 512
ATT_TK = 1024
ATT_AHEAD = 2
BAND_TQ = 256
BAND_AHEAD = 2
MERGE_ROWS = 512
MOE_ROWS = 256
DMA_QUEUES = 2
GU_GROUP = 256
GU_GROUPS = 2 * D_MODEL // GU_GROUP
FINAL_ROWS = 256


def _lambda_init(layer_idx):
    return 0.8 - 0.6 * math.exp(-0.3 * layer_idx)


def _cparams(sem):
    return pltpu.CompilerParams(dimension_semantics=sem, vmem_limit_bytes=VMEM_LIMIT)


def _half_mask(shape, upper):
    lane = lax.broadcasted_iota(I32, shape, len(shape) - 1)
    return (lane >= D_HEAD_A) if upper else (lane < D_HEAD_A)


def _proj_kernel(x_ref, w_ref, cos_ref, s1_ref, s2_ref,
                 qa_ref, ka32_ref, ka16_ref, va32_ref, va16_ref, vat_ref,
                 qb_ref, kb32_ref, kb16_ref, vb32_ref, vb16_ref):
    xb = x_ref[...].astype(BF16)

    def group(g):
        return jnp.dot(xb, w_ref[:, g * GROUP:(g + 1) * GROUP], preferred_element_type=F32)

    cos = cos_ref[...]
    s1 = s1_ref[...]
    s2 = s2_ref[...]

    def rope(h):
        parts = []
        for j in range(GROUP // LANES):
            hj = h[:, j * LANES:(j + 1) * LANES]
            nxt = pltpu.roll(hj, LANES - ROPE_DIM // 2, 1)
            prv = pltpu.roll(hj, ROPE_DIM // 2, 1)
            parts.append(hj * cos + nxt * s1 + prv * s2)
        return jnp.concatenate(parts, axis=1)

    qa = rope(group(0)) * (D_HEAD_A ** -0.5 * LOG2_E)
    qa_ref[...] = qa.astype(BF16)
    ka = rope(group(1))
    ka32_ref[...] = ka
    ka16_ref[...] = ka.astype(BF16)
    va = group(2)
    for h in range(H_A):
        va32_ref[pl.ds(h, va.shape[0], stride=H_A), :] = va[:, h * PAIR:(h + 1) * PAIR]
    va16_ref[...] = va.astype(BF16)
    vt = va.T
    rows = vt.shape[1]
    tail = jnp.concatenate([jnp.ones((1, rows), F32), jnp.zeros((VT_ROWS - PAIR - 1, rows), F32)], axis=0)
    parts = []
    for h in range(H_A):
        parts += [vt[h * PAIR:(h + 1) * PAIR], tail]
    vat_ref[...] = jnp.concatenate(parts, axis=0).astype(BF16)
    qb_ref[...] = (group(3) * (D_HEAD_B ** -0.5 * LOG2_E)).astype(BF16)
    kb = group(4)
    kb32_ref[...] = kb
    kb16_ref[...] = kb.astype(BF16)
    vb = group(5)
    vb32_ref[...] = vb
    vb16_ref[...] = vb.astype(BF16)


def _rope_tables(pos):
    half = ROPE_DIM // 2
    inv = ROPE_THETA ** (-jnp.arange(half, dtype=F32) * 2.0 / ROPE_DIM)
    ang = pos.astype(F32)[:, None] * inv[None, :]
    widen = lambda a: jnp.tile(a, (1, LANES // half))
    in_head = (jnp.arange(LANES, dtype=I32) % D_HEAD_A)[None, :]
    cos = jnp.where(in_head < ROPE_DIM, widen(jnp.cos(ang)), 1.0)
    sin = widen(jnp.sin(ang))
    s1 = jnp.where(in_head < half, -sin, 0.0)
    s2 = jnp.where((in_head >= half) & (in_head < ROPE_DIM), sin, 0.0)
    return cos, s1, s2


def _project(x, w16, pos):
    t = x.shape[0]
    cos, s1, s2 = _rope_tables(pos)
    row = lambda i: (i, 0)
    blk = lambda cols: pl.BlockSpec((PROJ_ROWS, cols), row)
    f32o = jax.ShapeDtypeStruct((t, GROUP), F32)
    b16o = jax.ShapeDtypeStruct((t, GROUP), BF16)
    return pl.pallas_call(
        _proj_kernel,
        grid=(t // PROJ_ROWS,),
        in_specs=[blk(D_MODEL), pl.BlockSpec(w16.shape, lambda i: (0, 0)),
                  blk(LANES), blk(LANES), blk(LANES)],
        out_specs=[blk(GROUP), blk(GROUP), blk(GROUP), pl.BlockSpec((PROJ_ROWS * H_A, PAIR), row), blk(GROUP),
                   pl.BlockSpec((H_A * VT_ROWS, PROJ_ROWS), lambda i: (0, i)),
                   blk(GROUP), blk(GROUP), blk(GROUP), blk(GROUP), blk(GROUP)],
        out_shape=[b16o, f32o, b16o, jax.ShapeDtypeStruct((t * H_A, PAIR), F32), b16o,
                   jax.ShapeDtypeStruct((H_A * VT_ROWS, t), BF16),
                   b16o, f32o, b16o, f32o, b16o],
        compiler_params=_cparams(("parallel",)),
        name="proj",
    )(x, w16, cos, s1, s2)


def _subln(o, g, lam_init, axis):
    o = o * lax.rsqrt(jnp.mean(jnp.square(o), axis=axis, keepdims=True) + SUBLN_EPS)
    return o * g * (1.0 - lam_init)


def _half_only(qp, upper):
    return jnp.where(_half_mask(qp.shape, upper), qp, jnp.zeros_like(qp))


def _attn_a_prompt_kernel(qi_ref, kj_ref, lam_ref, q_ref, k_ref, vt_ref, dmask_ref, g_ref, o_ref,
                          m_ref, acc_ref, *, lam_init):
    s = pl.program_id(0)
    qi = qi_ref[s]
    kj = kj_ref[s]

    @pl.when(kj == 0)
    def _():
        m_ref[...] = jnp.full(m_ref.shape, -jnp.inf, F32)
        acc_ref[...] = jnp.zeros(acc_ref.shape, F32)

    def step(diagonal):
        def scores(c):
            h = c // 2
            kp = k_ref[:, h * PAIR:(h + 1) * PAIR]
            qz = _half_only(q_ref[:, h * PAIR:(h + 1) * PAIR], c % 2 == 1)
            st = lax.dot_general(kp, qz, (((1,), (1,)), ((), ())), preferred_element_type=F32)
            return st + dmask_ref[...] if diagonal else st

        pending = [scores(c) for c in range(ATT_AHEAD)]
        for c in range(N_COMP):
            st = pending.pop(0)
            if c + ATT_AHEAD < N_COMP:
                pending.append(scores(c + ATT_AHEAD))
            m_old = m_ref[c:c + 1, :]
            m_new = jnp.maximum(m_old, jnp.max(st, axis=0, keepdims=True))
            alpha = jnp.exp2(m_old - m_new)
            p = jnp.exp2(st - m_new).astype(BF16)
            h = c // 2
            pv = jnp.dot(vt_ref[h * VT_ROWS:(h + 1) * VT_ROWS, :], p, preferred_element_type=F32)
            acc_ref[c] = alpha * acc_ref[c] + pv
            m_ref[c:c + 1, :] = m_new

    last_kj = qi // (ATT_TK // ATT_TQ)

    @pl.when(kj < last_kj)
    def _():
        step(False)

    @pl.when(kj == last_kj)
    def _():
        step(True)
        lam = lam_ref[0]
        for h in range(H_A):
            a0 = acc_ref[2 * h]
            a1 = acc_ref[2 * h + 1]
            o0 = a0[:PAIR] / a0[PAIR:PAIR + 1]
            o1 = a1[:PAIR] / a1[PAIR:PAIR + 1]
            ot = _subln(o0 - lam * o1, g_ref[...], lam_init, 0)
            o_ref[:, h * PAIR:(h + 1) * PAIR] = ot.T.astype(BF16)


def _attn_a_prompt(qa16, ka16, vat16, lam, subln_g, s_len, lam_init):
    ratio = ATT_TK // ATT_TQ
    assert ATT_TK == ratio * ATT_TQ and s_len % ATT_TK == 0
    kchunk = jnp.arange(ATT_TK, dtype=I32)[None, :, None] // CHUNK
    qchunk = (jnp.arange(ratio, dtype=I32)[:, None, None] * ATT_TQ
              + jnp.arange(ATT_TQ, dtype=I32)[None, None, :]) // CHUNK
    dmask = jnp.where(kchunk <= qchunk, 0.0, NEG_INF).astype(F32)
    nq = s_len // ATT_TQ
    qi_tab, kj_tab = [], []
    for i in range(nq):
        for j in range(i // ratio + 1):
            qi_tab.append(i)
            kj_tab.append(j)
    qi_tab = jnp.asarray(qi_tab, I32)
    kj_tab = jnp.asarray(kj_tab, I32)
    grid_spec = pltpu.PrefetchScalarGridSpec(
        num_scalar_prefetch=2,
        grid=(int(qi_tab.shape[0]),),
        in_specs=[
            pl.BlockSpec(memory_space=pltpu.SMEM),
            pl.BlockSpec((ATT_TQ, GROUP), lambda s, qi, kj: (qi[s], 0)),
            pl.BlockSpec((ATT_TK, GROUP), lambda s, qi, kj: (kj[s], 0)),
            pl.BlockSpec((H_A * VT_ROWS, ATT_TK), lambda s, qi, kj: (0, kj[s])),
            pl.BlockSpec((None, ATT_TK, ATT_TQ), lambda s, qi, kj: (qi[s] % ratio, 0, 0)),
            pl.BlockSpec((PAIR, 1), lambda s, qi, kj: (0, 0)),
        ],
        out_specs=pl.BlockSpec((ATT_TQ, GROUP), lambda s, qi, kj: (qi[s], 0)),
        scratch_shapes=[pltpu.VMEM((N_COMP, ATT_TQ), F32),
                        pltpu.VMEM((N_COMP, VT_ROWS, ATT_TQ), F32)],
    )
    return pl.pallas_call(
        functools.partial(_attn_a_prompt_kernel, lam_init=lam_init),
        grid_spec=grid_spec,
        out_shape=jax.ShapeDtypeStruct((s_len, GROUP), BF16),
        compiler_params=_cparams(("arbitrary",)),
        name="attn_a_prompt",
    )(qi_tab, kj_tab, lam, qa16, ka16, vat16, dmask, subln_g.reshape(PAIR, 1))


def _attn_a_sample_kernel(lam_ref, q_ref, kn_ref, vn_ref, kc_ref, vc_ref, g_ref, o_ref, *, lam_init):
    lam = lam_ref[0]
    past = vc_ref.shape[0] // H_A
    per_head = {}

    def head_operands(h):
        if h not in per_head:
            rows = slice(h * PAIR, (h + 1) * PAIR)
            per_head[h] = (kc_ref[rows, :].astype(BF16), kn_ref[:, rows],
                           vc_ref[pl.ds(h, past, stride=H_A), :].astype(BF16), vn_ref[:, rows])
        return per_head[h]

    def scores(c):
        kt_cache, k_new, _, _ = head_operands(c // 2)
        qz = _half_only(q_ref[:, c // 2 * PAIR:(c // 2 + 1) * PAIR], c % 2 == 1)
        return [jnp.dot(qz, kt_cache, preferred_element_type=F32),
                lax.dot_general(qz, k_new, (((1,), (1,)), ((), ())), preferred_element_type=F32)]

    pending = [scores(c) for c in range(ATT_AHEAD)]
    outs = []
    for c in range(N_COMP):
        scs = pending.pop(0)
        if c + ATT_AHEAD < N_COMP:
            pending.append(scores(c + ATT_AHEAD))
        m = functools.reduce(jnp.maximum, [jnp.max(sc, axis=1, keepdims=True) for sc in scs])
        l = None
        o = None
        for sc, v in zip(scs, head_operands(c // 2)[2:]):
            p = jnp.exp2(sc - m)
            ls = jnp.sum(p, axis=1, keepdims=True)
            os_ = jnp.dot(p.astype(BF16), v, preferred_element_type=F32)
            l = ls if l is None else l + ls
            o = os_ if o is None else o + os_
        outs.append(o / l)
    for h in range(H_A):
        o = _subln(outs[2 * h] - lam * outs[2 * h + 1], g_ref[...], lam_init, 1)
        o_ref[:, h * PAIR:(h + 1) * PAIR] = o.astype(BF16)


def _attn_a_sample(qa16, ka16, va16, cache_k, cache_v, lam, subln_g, row0, lam_init):
    nb = cache_k.shape[0]
    sd = CHUNK
    new = lambda b: (row0 // sd + b, 0)
    return pl.pallas_call(
        functools.partial(_attn_a_sample_kernel, lam_init=lam_init),
        grid=(nb,),
        in_specs=[
            pl.BlockSpec(memory_space=pltpu.SMEM),
            pl.BlockSpec((sd, GROUP), new), pl.BlockSpec((sd, GROUP), new), pl.BlockSpec((sd, GROUP), new),
            pl.BlockSpec((None,) + cache_k.shape[1:], lambda b: (b, 0, 0)),
            pl.BlockSpec((None,) + cache_v.shape[1:], lambda b: (b, 0, 0)),
            pl.BlockSpec((1, PAIR), lambda b: (0, 0)),
        ],
        out_specs=pl.BlockSpec((sd, GROUP), lambda b: (b, 0)),
        out_shape=jax.ShapeDtypeStruct((nb * sd, GROUP), BF16),
        compiler_params=_cparams(("parallel",)),
        name="attn_a_sample",
    )(lam, qa16, ka16, va16, cache_k, cache_v, subln_g.reshape(1, PAIR))


def _band_bias(rel_bias, n_q_chunks, first_pos):
    nq = n_q_chunks * CHUNK
    nk = nq + BAND_PAST
    heads = rel_bias.shape[0]
    n_lo = BAND_PAST - REL_CLIP
    period = max(nq + nk, n_lo + (2 * REL_CLIP + 1) + nq - 1)
    n_hi = period - (nq - 1) - n_lo - (2 * REL_CLIP + 1)
    assert n_lo >= 0
    lo = jnp.broadcast_to(rel_bias[:, :1], (heads, n_lo))
    hi = jnp.broadcast_to(rel_bias[:, -1:], (heads, n_hi))
    wrap = jnp.broadcast_to(rel_bias[:, :1], (heads, nq - 1))
    e = jnp.concatenate([lo, rel_bias, hi, wrap], axis=1).astype(F32)
    flat = jnp.broadcast_to(e[:, None, :], (heads, nq, period)).reshape(heads, nq * period)
    bias = flat[:, :nq * (period - 1)].reshape(heads, nq, period - 1)[:, :, :nk]
    qpos = jnp.arange(nq, dtype=I32)[:, None]
    kpos = jnp.arange(nk, dtype=I32)[None, :] - BAND_PAST
    dch = qpos // CHUNK - jnp.floor_divide(kpos, CHUNK)
    ok = (dch >= 0) & (dch <= BAND_CHUNKS) & (kpos + first_pos >= 0)
    return jnp.where(ok[None], bias * LOG2_E, NEG_INF)


def _band_heads(q_ref, ks, vs, bias_ref, o_ref):
    starts = [sum(k.shape[0] for k in ks[:n]) for n in range(len(ks))]

    def scores(head):
        cols = slice(head // 2 * PAIR, (head // 2 + 1) * PAIR)
        qz = _half_only(q_ref[:, cols], head % 2 == 1)
        out = []
        for k, col0 in zip(ks, starts):
            sc = lax.dot_general(qz, k[:, cols], (((1,), (1,)), ((), ())), preferred_element_type=F32)
            out.append(sc + bias_ref[head, :, col0:col0 + k.shape[0]])
        return out

    pending = [scores(h) for h in range(BAND_AHEAD)]
    res = []
    for head in range(H_B):
        scs = pending.pop(0)
        if head + BAND_AHEAD < H_B:
            pending.append(scores(head + BAND_AHEAD))
        cols = slice(head // 2 * PAIR, (head // 2 + 1) * PAIR)
        m = functools.reduce(jnp.maximum, [jnp.max(sc, axis=1, keepdims=True) for sc in scs])
        acc = None
        for sc, v in zip(scs, vs):
            vp = v[:, cols]
            v_and_ones = jnp.where(_half_mask(vp.shape, head % 2 == 1), vp, jnp.ones_like(vp))
            part = jnp.dot(jnp.exp2(sc - m).astype(BF16), v_and_ones, preferred_element_type=F32)
            acc = part if acc is None else acc + part
        res.append(acc)
    for j in range(H_B // 2):
        lower = _half_mask(res[2 * j].shape, False)
        num = jnp.where(lower, res[2 * j], res[2 * j + 1])
        den = jnp.where(lower, pltpu.roll(res[2 * j], D_HEAD_B, 1), pltpu.roll(res[2 * j + 1], D_HEAD_B, 1))
        o_ref[:, j * PAIR:(j + 1) * PAIR] = (num / den).astype(BF16)


def _attn_b_prompt_kernel(q_ref, k0_ref, k1_ref, k2_ref, v0_ref, v1_ref, v2_ref, bias_ref, o_ref):
    ks = [jnp.concatenate([k0_ref[...], k1_ref[...], k2_ref[...]], axis=0)]
    vs = [jnp.concatenate([v0_ref[...], v1_ref[...], v2_ref[...]], axis=0)]
    _band_heads(q_ref, ks, vs, bias_ref, o_ref)


def _attn_b_prompt(qb16, kb16, vb16, bias, s_len):
    assert BAND_PAST == 2 * BAND_TQ
    n_lead = bias.shape[0] - 1
    cur = lambda i: (i, 0)
    p1 = lambda i: (jnp.maximum(i - 1, 0), 0)
    p2 = lambda i: (jnp.maximum(i - 2, 0), 0)
    blk = lambda f: pl.BlockSpec((BAND_TQ, GROUP), f)
    return pl.pallas_call(
        _attn_b_prompt_kernel,
        grid=(s_len // BAND_TQ,),
        in_specs=[blk(cur), blk(p2), blk(p1), blk(cur), blk(p2), blk(p1), blk(cur),
                  pl.BlockSpec((None,) + bias.shape[1:], lambda i: (jnp.minimum(i, n_lead), 0, 0, 0))],
        out_specs=blk(cur),
        out_shape=jax.ShapeDtypeStruct((s_len, GROUP), BF16),
        compiler_params=_cparams(("parallel",)),
        name="attn_b_prompt",
    )(qb16, kb16, kb16, kb16, vb16, vb16, vb16, bias)


def _attn_b_sample_kernel(q_ref, kn_ref, vn_ref, kc_ref, vc_ref, bias_ref, o_ref):
    ks = [kc_ref[...].T.astype(BF16), kn_ref[...]]
    vs = [vc_ref[...].T.astype(BF16), vn_ref[...]]
    _band_heads(q_ref, ks, vs, bias_ref, o_ref)


def _attn_b_sample(qb16, kb16, vb16, cache_k, cache_v, bias, row0):
    nb, _, lb = cache_k.shape
    sd = CHUNK
    new = lambda b: (row0 // sd + b, 0)
    return pl.pallas_call(
        _attn_b_sample_kernel,
        grid=(nb,),
        in_specs=[
            pl.BlockSpec((sd, GROUP), new), pl.BlockSpec((sd, GROUP), new), pl.BlockSpec((sd, GROUP), new),
            pl.BlockSpec((None, GROUP, lb), lambda b: (b, 0, 0)),
            pl.BlockSpec((None, GROUP, lb), lambda b: (b, 0, 0)),
            pl.BlockSpec(bias.shape, lambda b: (0, 0, 0)),
        ],
        out_specs=pl.BlockSpec((sd, GROUP), lambda b: (b, 0)),
        out_shape=jax.ShapeDtypeStruct((nb * sd, GROUP), BF16),
        compiler_params=_cparams(("parallel",)),
        name="attn_b_sample",
    )(qb16, kb16, vb16, cache_k, cache_v, bias)


def _layer_norm(z, g, b):
    mu = jnp.mean(z, axis=-1, keepdims=True)
    var = jnp.mean(jnp.square(z - mu), axis=-1, keepdims=True)
    return (z - mu) * lax.rsqrt(var + LN_EPS) * g + b


def _to_token_tiles(ref, x):
    rows = x.shape[0]
    for j in range(TT):
        ref[pl.ds(j, rows, stride=TT), :] = x[:, j * LANES:(j + 1) * LANES]


def _from_token_tiles(ref, rows):
    return jnp.concatenate([ref[pl.ds(j, rows, stride=TT), :] for j in range(TT)], axis=1)


def _merge_kernel(oap_ref, obp_ref, oas_ref, obs_ref, xp_ref, xs_ref, wo_ref, g_ref, b_ref, wr_ref,
                  br_ref, x1_ref, ti_ref, gate_ref, *, n_prompt_blocks):
    is_prompt = pl.program_id(0) < n_prompt_blocks
    oa = jnp.where(is_prompt, oap_ref[...], oas_ref[...])
    ob = jnp.where(is_prompt, obp_ref[...], obs_ref[...])
    x = jnp.where(is_prompt, xp_ref[...], xs_ref[...])
    mix = jnp.dot(oa, wo_ref[:GROUP, :], preferred_element_type=F32)
    mix = mix + jnp.dot(ob, wo_ref[GROUP:, :], preferred_element_type=F32)
    x1 = _layer_norm(DEEPNORM_ALPHA * x + mix, g_ref[...], b_ref[...])
    _to_token_tiles(x1_ref, x1)
    logits = jnp.dot(x1.astype(BF16), wr_ref[...], preferred_element_type=F32) + br_ref[...]
    lane = lax.broadcasted_iota(I32, logits.shape, 1)
    logits = jnp.where(lane < N_EXPERTS, logits, -jnp.inf)
    vals, idxs = [], []
    for _ in range(TOP_K):
        mx = jnp.max(logits, axis=1, keepdims=True)
        ix = jnp.min(jnp.where(logits == mx, lane, LANES), axis=1, keepdims=True)
        vals.append(mx)
        idxs.append(ix)
        logits = jnp.where(lane == ix, -jnp.inf, logits)
    col = lax.broadcasted_iota(I32, ti_ref.shape, 1)
    top_v = jnp.broadcast_to(vals[-1], ti_ref.shape)
    top_i = jnp.broadcast_to(idxs[-1], ti_ref.shape)
    for k in range(TOP_K - 1):
        top_v = jnp.where(col == k, vals[k], top_v)
        top_i = jnp.where(col == k, idxs[k], top_i)
    ti_ref[...] = top_i
    e = jnp.exp(top_v - vals[0])
    gate_ref[...] = e / jnp.sum(e, axis=1, keepdims=True)


def _merge(oa_p, ob_p, oa_s, ob_s, x_p, x_s, wo16, ln_g, ln_b, wr16, br):
    t = x_p.shape[0] + x_s.shape[0]
    npb = x_p.shape[0] // MERGE_ROWS
    row = lambda i: (i, 0)
    const = lambda i: (0, 0)
    blk = lambda cols: pl.BlockSpec((MERGE_ROWS, cols), row)
    prompt = lambda cols: pl.BlockSpec((MERGE_ROWS, cols), lambda i: (jnp.minimum(i, npb - 1), 0))
    sample = lambda cols: pl.BlockSpec((MERGE_ROWS, cols), lambda i: (jnp.maximum(i - npb, 0), 0))
    return pl.pallas_call(
        functools.partial(_merge_kernel, n_prompt_blocks=npb),
        grid=(t // MERGE_ROWS,),
        in_specs=[prompt(GROUP), prompt(GROUP), sample(GROUP), sample(GROUP),
                  prompt(D_MODEL), sample(D_MODEL),
                  pl.BlockSpec(wo16.shape, const), pl.BlockSpec((1, D_MODEL), const),
                  pl.BlockSpec((1, D_MODEL), const), pl.BlockSpec(wr16.shape, const),
                  pl.BlockSpec((1, LANES), const)],
        out_specs=[pl.BlockSpec((MERGE_ROWS * TT, LANES), row), blk(TOP_K), blk(TOP_K)],
        out_shape=[jax.ShapeDtypeStruct((t * TT, LANES), F32), jax.ShapeDtypeStruct((t, TOP_K), I32),
                   jax.ShapeDtypeStruct((t, TOP_K), F32)],
        compiler_params=_cparams(("parallel",)),
        name="merge",
    )(oa_p, ob_p, oa_s, ob_s, x_p, x_s, wo16, ln_g.reshape(1, D_MODEL), ln_b.reshape(1, D_MODEL),
      wr16, br)


def _moe_kernel(blk_e_ref, nused_ref, fresh_ref, wslot_ref, next_e_ref, idx_hbm, x_hbm, wgu_hbm, wd_hbm,
                perm_ref, bgu_ref, bd_ref, out_hbm, idx_smem, xbuf, ybuf, wgu32, wd32, wgu16_ref,
                wd16_ref, isem, gsem, ssem, wsem, *, n_tok):
    i = pl.program_id(0)
    nused = nused_ref[0]
    n_assign = n_tok * TOP_K
    last = nused - 1
    par = lax.rem(i, 2)

    def idx_copy(block, slot):
        return pltpu.make_async_copy(idx_hbm.at[block], idx_smem.at[slot], isem.at[slot])

    def start_gather(islot, xslot):
        for r in range(MOE_ROWS):
            src = pl.multiple_of(idx_smem[islot, 0, r], TT)
            pltpu.make_async_copy(x_hbm.at[pl.ds(src, TT)], xbuf.at[xslot, pl.ds(r * TT, TT)],
                                  gsem.at[xslot]).start(priority=r % DMA_QUEUES)

    def wait_gather(xslot):
        pltpu.make_async_copy(x_hbm.at[pl.ds(0, MOE_ROWS * TT)], xbuf.at[xslot], gsem.at[xslot]).wait()

    def start_scatter(islot, yslot):
        for r in range(MOE_ROWS):
            dst = pl.multiple_of(idx_smem[islot, 1, r], TT)
            pltpu.make_async_copy(ybuf.at[yslot, pl.ds(r * TT, TT)], out_hbm.at[pl.ds(dst, TT)],
                                  ssem.at[yslot]).start(priority=r % DMA_QUEUES)

    def weight_copies(expert, wslot):
        return (pltpu.make_async_copy(wgu_hbm.at[expert], wgu32.at[wslot], wsem.at[wslot, 0]),
                pltpu.make_async_copy(wd_hbm.at[expert], wd32.at[wslot], wsem.at[wslot, 1]))

    def slot_flush(yslot, row0):
        return pltpu.make_async_copy(ybuf.at[yslot], out_hbm.at[pl.ds(row0 * TT, MOE_ROWS * TT)],
                                     ssem.at[yslot])

    @pl.when(i == 0)
    def _():
        ybuf[...] = jnp.zeros(ybuf.shape, F32)
        for p in range(2):
            slot_flush(p, n_assign + p * MOE_ROWS).start()
        idx_copy(0, 0).start()
        for c in weight_copies(blk_e_ref[0], 0):
            c.start()
        idx_copy(0, 0).wait()
        start_gather(0, 0)
        idx_copy(jnp.minimum(1, last), 1).start()

    @pl.when(i < nused)
    def _():
        s_cur = lax.rem(i, 3)
        s_nxt = lax.rem(i + 1, 3)
        s_nn = lax.rem(i + 2, 3)
        idx_copy(0, s_nxt).wait()
        start_gather(s_nxt, 1 - par)
        idx_copy(jnp.minimum(i + 2, last), s_nn).start()

        wslot = wslot_ref[i]

        @pl.when(fresh_ref[i] == 1)
        def _():
            for c in weight_copies(0, wslot):
                c.wait()
            for c in weight_copies(next_e_ref[i], 1 - wslot):
                c.start()
            for g in range(GU_GROUPS):
                cols = slice(g * GU_GROUP, (g + 1) * GU_GROUP)
                sorted_cols = jnp.dot(wgu32[wslot, :, cols].astype(BF16), perm_ref[...],
                                      preferred_element_type=F32)
                wgu16_ref[:, cols] = sorted_cols.astype(BF16)
            wd16_ref[...] = wd32[wslot].astype(BF16)

        wait_gather(par)
        x = _from_token_tiles(xbuf.at[par], MOE_ROWS).astype(BF16)
        h = jnp.dot(x, wgu16_ref[...], preferred_element_type=F32) + bgu_ref[...]
        half = GU_GROUP // 2
        hg = jnp.concatenate([h[:, g * GU_GROUP:g * GU_GROUP + half] for g in range(GU_GROUPS)], axis=1)
        hu = jnp.concatenate([h[:, g * GU_GROUP + half:(g + 1) * GU_GROUP] for g in range(GU_GROUPS)],
                             axis=1)
        gate = jnp.minimum(hg, SWIGLU_LIMIT)
        up = jnp.clip(hu, -SWIGLU_LIMIT, SWIGLU_LIMIT)
        glu = gate * (1.0 / (1.0 + jnp.exp(-(gate * SWIGLU_ALPHA))))
        act = ((up + 1.0) * glu).astype(BF16)
        y = jnp.dot(act, wd16_ref[...], preferred_element_type=F32) + bd_ref[...]

        slot_flush(par, 0).wait()
        _to_token_tiles(ybuf.at[par], y)
        start_scatter(s_cur, par)

        @pl.when(i == last)
        def _():
            slot_flush(par, 0).wait()
            slot_flush(1 - par, 0).wait()
            wait_gather(1 - par)
            idx_copy(0, s_nn).wait()
            for c in weight_copies(0, 1 - wslot):
                c.wait()


def _gate_up_sorter():
    src = jnp.arange(GU_GROUP, dtype=I32)[:, None]
    dst = jnp.arange(GU_GROUP, dtype=I32)[None, :]
    half = GU_GROUP // 2
    return (src == jnp.where(dst < half, 2 * dst, 2 * (dst - half) + 1)).astype(BF16)


def _sort_gate_up(b):
    lead = b.shape[:-1]
    return b.reshape(lead + (GU_GROUPS, GU_GROUP // 2, 2)).swapaxes(-1, -2).reshape(lead + (-1,))


def _moe(x1, idx, blk_e, nused, fresh, wslot, next_e, w_gate_up, w_down, bgu, bd):
    t = x1.shape[0] // TT
    nblk = idx.shape[0]
    wmap = lambda i, be, nu, fr, ws, ne: (be[i], 0, 0)
    const = lambda i, be, nu, fr, ws, ne: (0, 0)
    grid_spec = pltpu.PrefetchScalarGridSpec(
        num_scalar_prefetch=5,
        grid=(nblk,),
        in_specs=[
            pl.BlockSpec(memory_space=pl.ANY),
            pl.BlockSpec(memory_space=pl.ANY),
            pl.BlockSpec(memory_space=pl.ANY),
            pl.BlockSpec(memory_space=pl.ANY),
            pl.BlockSpec((GU_GROUP, GU_GROUP), const),
            pl.BlockSpec((None, 1, 2 * D_MODEL), wmap),
            pl.BlockSpec((None, 1, D_MODEL), wmap),
        ],
        out_specs=pl.BlockSpec(memory_space=pl.ANY),
        scratch_shapes=[
            pltpu.SMEM((3, 2, MOE_ROWS), I32),
            pltpu.VMEM((2, MOE_ROWS * TT, LANES), F32),
            pltpu.VMEM((2, MOE_ROWS * TT, LANES), F32),
            pltpu.VMEM((2, D_MODEL, 2 * D_MODEL), F32),
            pltpu.VMEM((2, D_MODEL, D_MODEL), F32),
            pltpu.VMEM((D_MODEL, 2 * D_MODEL), BF16),
            pltpu.VMEM((D_MODEL, D_MODEL), BF16),
            pltpu.SemaphoreType.DMA((3,)),
            pltpu.SemaphoreType.DMA((2,)),
            pltpu.SemaphoreType.DMA((2,)),
            pltpu.SemaphoreType.DMA((2, 2)),
        ],
    )
    return pl.pallas_call(
        functools.partial(_moe_kernel, n_tok=t),
        grid_spec=grid_spec,
        out_shape=jax.ShapeDtypeStruct(((t * TOP_K + 2 * MOE_ROWS) * TT, LANES), F32),
        compiler_params=_cparams(("arbitrary",)),
        name="moe",
    )(blk_e, nused, fresh, wslot, next_e, idx, x1, w_gate_up, w_down, _gate_up_sorter(), bgu, bd)


def _route(top_i):
    t = top_i.shape[0]
    n = t * TOP_K
    nblk = n // MOE_ROWS + N_EXPERTS
    flat_e = top_i.reshape(-1)
    order = jnp.argsort(flat_e, stable=True).astype(I32)
    experts = jnp.arange(N_EXPERTS, dtype=I32)
    counts = jnp.sum((flat_e[:, None] == experts[None, :]).astype(I32), axis=0)
    padded = ((counts + MOE_ROWS - 1) // MOE_ROWS) * MOE_ROWS
    pad_end = jnp.cumsum(padded)
    pad_start = pad_end - padded
    start = jnp.cumsum(counts) - counts
    nused = (pad_end[-1] // MOE_ROWS).astype(I32).reshape(1)
    blk_first = jnp.arange(nblk, dtype=I32) * MOE_ROWS
    blk_e = jnp.minimum(jnp.sum((blk_first[:, None] >= pad_end[None, :]).astype(I32), axis=1),
                        N_EXPERTS - 1)
    row = jnp.arange(MOE_ROWS, dtype=I32)[None, :]
    j = blk_first[:, None] + row - pad_start[blk_e][:, None]
    valid = j < counts[blk_e][:, None]
    a = order[jnp.clip(start[blk_e][:, None] + j, 0, n - 1)]
    tok = lax.shift_right_logical(a, TOP_SHIFT)
    src = jnp.where(valid, tok, 0)
    dump = n + (jnp.arange(nblk, dtype=I32)[:, None] % 2) * MOE_ROWS + row
    dst = jnp.where(valid, (a & (TOP_K - 1)) * t + tok, dump)
    blk_e = blk_e.astype(I32)
    fresh = jnp.concatenate([jnp.ones((1,), I32), (blk_e[1:] != blk_e[:-1]).astype(I32)])
    wslot = (jnp.cumsum(fresh) - 1) % 2
    blk = jnp.arange(nblk, dtype=I32)
    nxt = lax.cummin(jnp.where(fresh == 1, blk, nblk)[::-1])[::-1]
    nxt = jnp.concatenate([nxt[1:], jnp.full((1,), nblk, I32)])
    next_e = jnp.where(nxt < nblk, blk_e[jnp.minimum(nxt, nblk - 1)], blk_e)
    return (jnp.stack([src * TT, dst * TT], axis=1).astype(I32), blk_e, nused, fresh,
            wslot.astype(I32), next_e.astype(I32))


def _final_kernel(x1_ref, gate_ref, y0_ref, y1_ref, y2_ref, y3_ref, g_ref, b_ref, o_ref):
    gates = gate_ref[...]
    y = gates[:, 0:1] * _from_token_tiles(y0_ref, FINAL_ROWS)
    for k, ref in enumerate((y1_ref, y2_ref, y3_ref), start=1):
        y = y + gates[:, k:k + 1] * _from_token_tiles(ref, FINAL_ROWS)
    x1 = _from_token_tiles(x1_ref, FINAL_ROWS)
    o_ref[...] = _layer_norm(DEEPNORM_ALPHA * x1 + y, g_ref[...], b_ref[...])


def _final(x1, gates, planes, ln_g, ln_b, row0, rows):
    nb_all = gates.shape[0] // FINAL_ROWS
    b0 = row0 // FINAL_ROWS
    row = lambda i: (b0 + i, 0)
    const = lambda i: (0, 0)
    tiles = lambda f: pl.BlockSpec((FINAL_ROWS * TT, LANES), f)
    plane = lambda k: tiles(lambda i: (k * nb_all + b0 + i, 0))
    return pl.pallas_call(
        _final_kernel,
        grid=(rows // FINAL_ROWS,),
        in_specs=[tiles(row), pl.BlockSpec((FINAL_ROWS, TOP_K), row),
                  plane(0), plane(1), plane(2), plane(3),
                  pl.BlockSpec((1, D_MODEL), const), pl.BlockSpec((1, D_MODEL), const)],
        out_specs=pl.BlockSpec((FINAL_ROWS, D_MODEL), lambda i: (i, 0)),
        out_shape=jax.ShapeDtypeStruct((rows, D_MODEL), F32),
        compiler_params=_cparams(("parallel",)),
        name="final",
    )(x1, gates, planes, planes, planes, planes, ln_g.reshape(1, D_MODEL), ln_b.reshape(1, D_MODEL))


def kernel(x_prompt, x_sample, cache_a_k, cache_a_v, cache_b_k, cache_b_v, w_in, lambda_qk, subln_g,
           rel_bias, w_out, ln1_g, ln1_b, w_router, b_router, w_gate_up, b_gate_up, w_down, b_down,
           ln2_g, ln2_b):
    b, s, d = x_prompt.shape
    bd, sd, _ = x_sample.shape
    depth, _, past, _, _ = cache_a_k.shape
    lb = cache_b_k.shape[2]
    assert depth == 1 and b == 1 and d == D_MODEL
    assert sd == CHUNK and past % CHUNK == 0 and lb == BAND_PAST and s >= BAND_PAST
    assert s % ATT_TQ == 0 and s % BAND_TQ == 0
    ts = bd * sd
    t = s + ts
    lam_init = _lambda_init(0)

    x_p = x_prompt.reshape(s, d)
    x_s = x_sample.reshape(ts, d)
    pos_p = jnp.arange(s, dtype=I32)
    pos_s = jnp.tile(past + jnp.arange(sd, dtype=I32), bd)
    lq = lambda_qk[0].astype(F32)
    lam = (jnp.exp(jnp.sum(lq[0] * lq[1])) - jnp.exp(jnp.sum(lq[2] * lq[3])) + lam_init).reshape(1)

    w_in16 = w_in[0].astype(BF16)
    (qa_p, ka32_p, ka_p, va32_p, _, vat_p, qb_p, kb32_p, kb_p, vb32_p, vb_p) = _project(x_p, w_in16, pos_p)
    (qa_s, ka32_s, ka_s, va32_s, va_s, _, qb_s, kb32_s, kb_s, vb32_s, vb_s) = _project(x_s, w_in16, pos_s)

    oa_p = _attn_a_prompt(qa_p, ka_p, vat_p, lam, subln_g[0], s, lam_init)
    kt_a = jnp.transpose(cache_a_k[0], (0, 2, 3, 1)).reshape(bd, GROUP, past)
    oa_s = _attn_a_sample(qa_s, ka_s, va_s, kt_a, cache_a_v[0].reshape(bd, past * H_A, PAIR),
                          lam, subln_g[0], 0, lam_init)

    n_lead = BAND_PAST // BAND_TQ
    bias_p = jnp.stack([_band_bias(rel_bias[0], BAND_TQ // CHUNK, i * BAND_TQ) for i in range(n_lead + 1)])
    ob_p = _attn_b_prompt(qb_p, kb_p, vb_p, bias_p, s)
    transposed = lambda c: jnp.transpose(c, (0, 2, 3, 1)).reshape(bd, GROUP, -1)
    ob_s = _attn_b_sample(qb_s, kb_s, vb_s, transposed(cache_b_k[0]), transposed(cache_b_v[0]),
                          _band_bias(rel_bias[0], 1, past), 0)

    wr16 = jnp.pad(w_router[0], ((0, 0), (0, LANES - N_EXPERTS))).astype(BF16)
    br = jnp.pad(b_router[0], (0, LANES - N_EXPERTS)).reshape(1, LANES)
    x1, top_i, gates = _merge(oa_p, ob_p, oa_s, ob_s, x_p, x_s, w_out[0].astype(BF16), ln1_g[0],
                              ln1_b[0], wr16, br)

    idx, blk_e, nused, fresh, wslot, next_e = _route(top_i)
    planes = _moe(x1, idx, blk_e, nused, fresh, wslot, next_e, w_gate_up[0], w_down[0],
                  _sort_gate_up(b_gate_up[0]).reshape(N_EXPERTS, 1, 2 * D_MODEL),
                  b_down[0].reshape(N_EXPERTS, 1, D_MODEL))
    y_p = _final(x1, gates, planes, ln2_g[0], ln2_b[0], 0, s)
    y_s = _final(x1, gates, planes, ln2_g[0], ln2_b[0], s, ts)

    heads_a = lambda a, n: a.reshape(1, n, -1, N_COMP, D_HEAD_A)
    vals_a = lambda a, n: a.reshape(1, n, -1, H_A, 2 * D_HEAD_A)
    heads_b = lambda a, n: a.reshape(1, n, -1, H_B, D_HEAD_B)
    keep_s = min(BAND_PAST, lb + sd)
    kb_new = jnp.concatenate([cache_b_k[0], heads_b(kb32_s, bd)[0]], axis=1)[:, lb + sd - keep_s:]
    vb_new = jnp.concatenate([cache_b_v[0], heads_b(vb32_s, bd)[0]], axis=1)[:, lb + sd - keep_s:]
    return (y_p.reshape(b, s, d), y_s.reshape(bd, sd, d),
            heads_a(ka32_p, b), vals_a(va32_p, b),
            heads_b(kb32_p[s - BAND_PAST:], b), heads_b(vb32_p[s - BAND_PAST:], b),
            heads_a(ka32_s, bd), vals_a(va32_s, bd),
            kb_new[None], vb_new[None])
```

```python
import functools
import math

import jax
import jax.numpy as jnp
from jax import lax
from jax.experimental import pallas as pl
from jax.experimental.pallas import tpu as pltpu

F32 = jnp.float32
BF16 = jnp.bfloat16
I32 = jnp.int32

D_MODEL = 1024
CHUNK = 64
D_HEAD_A = 64
H_A = 4
N_COMP = 2 * H_A
ROPE_DIM = D_HEAD_A // 4
ROPE_THETA = 500000.0
D_HEAD_B = 64
H_B = 8
BAND_CHUNKS = 8
BAND_PAST = BAND_CHUNKS * CHUNK
REL_CLIP = 128
GROUP = 512
N_EXPERTS = 32
TOP_K = 4
TOP_SHIFT = 2
SWIGLU_LIMIT = 7.0
SWIGLU_ALPHA = 1.702
LN_EPS = 1e-5
SUBLN_EPS = 1e-5
DEPTH = 1
DEEPNORM_ALPHA = (2.0 * DEPTH) ** 0.25
NEG_INF = -1e30
LANES = 128
PAIR = 2 * D_HEAD_A
TT = D_MODEL // LANES
VT_ROWS = PAIR + 16
LOG2_E = math.log2(math.e)

VMEM_LIMIT = 56 * 1024 * 1024

PROJ_ROWS = 256
ATT_TQ = 512
ATT_TK = 1024
ATT_AHEAD = 2
BAND_TQ = 256
BAND_AHEAD = 2
MERGE_ROWS = 512
MOE_ROWS = 256
DMA_QUEUES = 2
GU_GROUP = 256
GU_GROUPS = 2 * D_MODEL // GU_GROUP
FINAL_ROWS = 256


def _lambda_init(layer_idx):
    return 0.8 - 0.6 * math.exp(-0.3 * layer_idx)


def _cparams(sem):
    return pltpu.CompilerParams(dimension_semantics=sem, vmem_limit_bytes=VMEM_LIMIT)


def _half_mask(shape, upper):
    lane = lax.broadcasted_iota(I32, shape, len(shape) - 1)
    return (lane >= D_HEAD_A) if upper else (lane < D_HEAD_A)


def _proj_kernel(x_ref, w_ref, cos_ref, s1_ref, s2_ref,
                 qa_ref, ka32_ref, ka16_ref, va32_ref, va16_ref, vat_ref,
                 qb_ref, kb32_ref, kb16_ref, vb32_ref, vb16_ref):
    xb = x_ref[...].astype(BF16)

    def group(g):
        return jnp.dot(xb, w_ref[:, g * GROUP:(g + 1) * GROUP], preferred_element_type=F32)

    cos = cos_ref[...]
    s1 = s1_ref[...]
    s2 = s2_ref[...]

    def rope(h):
        parts = []
        for j in range(GROUP // LANES):
            hj = h[:, j * LANES:(j + 1) * LANES]
            nxt = pltpu.roll(hj, LANES - ROPE_DIM // 2, 1)
            prv = pltpu.roll(hj, ROPE_DIM // 2, 1)
            parts.append(hj * cos + nxt * s1 + prv * s2)
        return jnp.concatenate(parts, axis=1)

    qa = rope(group(0)) * (D_HEAD_A ** -0.5 * LOG2_E)
    qa_ref[...] = qa.astype(BF16)
    ka = rope(group(1))
    ka32_ref[...] = ka
    ka16_ref[...] = ka.astype(BF16)
    va = group(2)
    for h in range(H_A):
        va32_ref[pl.ds(h, va.shape[0], stride=H_A), :] = va[:, h * PAIR:(h + 1) * PAIR]
    va16_ref[...] = va.astype(BF16)
    vt = va.T
    rows = vt.shape[1]
    tail = jnp.concatenate([jnp.ones((1, rows), F32), jnp.zeros((VT_ROWS - PAIR - 1, rows), F32)], axis=0)
    parts = []
    for h in range(H_A):
        parts += [vt[h * PAIR:(h + 1) * PAIR], tail]
    vat_ref[...] = jnp.concatenate(parts, axis=0).astype(BF16)
    qb_ref[...] = (group(3) * (D_HEAD_B ** -0.5 * LOG2_E)).astype(BF16)
    kb = group(4)
    kb32_ref[...] = kb
    kb16_ref[...] = kb.astype(BF16)
    vb = group(5)
    vb32_ref[...] = vb
    vb16_ref[...] = vb.astype(BF16)


def _rope_tables(pos):
    half = ROPE_DIM // 2
    inv = ROPE_THETA ** (-jnp.arange(half, dtype=F32) * 2.0 / ROPE_DIM)
    ang = pos.astype(F32)[:, None] * inv[None, :]
    widen = lambda a: jnp.tile(a, (1, LANES // half))
    in_head = (jnp.arange(LANES, dtype=I32) % D_HEAD_A)[None, :]
    cos = jnp.where(in_head < ROPE_DIM, widen(jnp.cos(ang)), 1.0)
    sin = widen(jnp.sin(ang))
    s1 = jnp.where(in_head < half, -sin, 0.0)
    s2 = jnp.where((in_head >= half) & (in_head < ROPE_DIM), sin, 0.0)
    return cos, s1, s2


def _project(x, w16, pos):
    t = x.shape[0]
    cos, s1, s2 = _rope_tables(pos)
    row = lambda i: (i, 0)
    blk = lambda cols: pl.BlockSpec((PROJ_ROWS, cols), row)
    f32o = jax.ShapeDtypeStruct((t, GROUP), F32)
    b16o = jax.ShapeDtypeStruct((t, GROUP), BF16)
    return pl.pallas_call(
        _proj_kernel,
        grid=(t // PROJ_ROWS,),
        in_specs=[blk(D_MODEL), pl.BlockSpec(w16.shape, lambda i: (0, 0)),
                  blk(LANES), blk(LANES), blk(LANES)],
        out_specs=[blk(GROUP), blk(GROUP), blk(GROUP), pl.BlockSpec((PROJ_ROWS * H_A, PAIR), row), blk(GROUP),
                   pl.BlockSpec((H_A * VT_ROWS, PROJ_ROWS), lambda i: (0, i)),
                   blk(GROUP), blk(GROUP), blk(GROUP), blk(GROUP), blk(GROUP)],
        out_shape=[b16o, f32o, b16o, jax.ShapeDtypeStruct((t * H_A, PAIR), F32), b16o,
                   jax.ShapeDtypeStruct((H_A * VT_ROWS, t), BF16),
                   b16o, f32o, b16o, f32o, b16o],
        compiler_params=_cparams(("parallel",)),
        name="proj",
    )(x, w16, cos, s1, s2)


def _subln(o, g, lam_init, axis):
    o = o * lax.rsqrt(jnp.mean(jnp.square(o), axis=axis, keepdims=True) + SUBLN_EPS)
    return o * g * (1.0 - lam_init)


def _half_only(qp, upper):
    return jnp.where(_half_mask(qp.shape, upper), qp, jnp.zeros_like(qp))


def _attn_a_prompt_kernel(qi_ref, kj_ref, lam_ref, q_ref, k_ref, vt_ref, dmask_ref, g_ref, o_ref,
                          m_ref, acc_ref, *, lam_init):
    s = pl.program_id(0)
    qi = qi_ref[s]
    kj = kj_ref[s]

    @pl.when(kj == 0)
    def _():
        m_ref[...] = jnp.full(m_ref.shape, -jnp.inf, F32)
        acc_ref[...] = jnp.zeros(acc_ref.shape, F32)

    def step(diagonal, n_keys=ATT_TK):
        def scores(c):
            h = c // 2
            kp = k_ref[:n_keys, h * PAIR:(h + 1) * PAIR]
            qz = _half_only(q_ref[:, h * PAIR:(h + 1) * PAIR], c % 2 == 1)
            st = lax.dot_general(kp, qz, (((1,), (1,)), ((), ())), preferred_element_type=F32)
            return st + dmask_ref[:n_keys, :] if diagonal else st

        pending = [scores(c) for c in range(ATT_AHEAD)]
        for c in range(N_COMP):
            st = pending.pop(0)
            if c + ATT_AHEAD < N_COMP:
                pending.append(scores(c + ATT_AHEAD))
            m_old = m_ref[c:c + 1, :]
            m_new = jnp.maximum(m_old, jnp.max(st, axis=0, keepdims=True))
            alpha = jnp.exp2(m_old - m_new)
            p = jnp.exp2(st - m_new).astype(BF16)
            h = c // 2
            pv = jnp.dot(vt_ref[h * VT_ROWS:(h + 1) * VT_ROWS, :n_keys], p, preferred_element_type=F32)
            acc_ref[c] = alpha * acc_ref[c] + pv
            m_ref[c:c + 1, :] = m_new

    ratio = ATT_TK // ATT_TQ
    last_kj = qi // ratio

    @pl.when(kj < last_kj)
    def _():
        step(False)

    for part in range(ratio):
        @pl.when((kj == last_kj) & (qi % ratio == part))
        def _():
            step(True, (part + 1) * ATT_TQ)

    @pl.when(kj == last_kj)
    def _():
        lam = lam_ref[0]
        for h in range(H_A):
            a0 = acc_ref[2 * h]
            a1 = acc_ref[2 * h + 1]
            o0 = a0[:PAIR] / a0[PAIR:PAIR + 1]
            o1 = a1[:PAIR] / a1[PAIR:PAIR + 1]
            ot = _subln(o0 - lam * o1, g_ref[...], lam_init, 0)
            o_ref[:, h * PAIR:(h + 1) * PAIR] = ot.T.astype(BF16)


def _attn_a_prompt(qa16, ka16, vat16, lam, subln_g, s_len, lam_init):
    ratio = ATT_TK // ATT_TQ
    assert ATT_TK == ratio * ATT_TQ and s_len % ATT_TK == 0
    kchunk = jnp.arange(ATT_TK, dtype=I32)[None, :, None] // CHUNK
    qchunk = (jnp.arange(ratio, dtype=I32)[:, None, None] * ATT_TQ
              + jnp.arange(ATT_TQ, dtype=I32)[None, None, :]) // CHUNK
    dmask = jnp.where(kchunk <= qchunk, 0.0, NEG_INF).astype(F32)
    nq = s_len // ATT_TQ
    qi_tab, kj_tab = [], []
    for i in range(nq):
        for j in range(i // ratio + 1):
            qi_tab.append(i)
            kj_tab.append(j)
    qi_tab = jnp.asarray(qi_tab, I32)
    kj_tab = jnp.asarray(kj_tab, I32)
    grid_spec = pltpu.PrefetchScalarGridSpec(
        num_scalar_prefetch=2,
        grid=(int(qi_tab.shape[0]),),
        in_specs=[
            pl.BlockSpec(memory_space=pltpu.SMEM),
            pl.BlockSpec((ATT_TQ, GROUP), lambda s, qi, kj: (qi[s], 0)),
            pl.BlockSpec((ATT_TK, GROUP), lambda s, qi, kj: (kj[s], 0)),
            pl.BlockSpec((H_A * VT_ROWS, ATT_TK), lambda s, qi, kj: (0, kj[s])),
            pl.BlockSpec((None, ATT_TK, ATT_TQ), lambda s, qi, kj: (qi[s] % ratio, 0, 0)),
            pl.BlockSpec((PAIR, 1), lambda s, qi, kj: (0, 0)),
        ],
        out_specs=pl.BlockSpec((ATT_TQ, GROUP), lambda s, qi, kj: (qi[s], 0)),
        scratch_shapes=[pltpu.VMEM((N_COMP, ATT_TQ), F32),
                        pltpu.VMEM((N_COMP, VT_ROWS, ATT_TQ), F32)],
    )
    return pl.pallas_call(
        functools.partial(_attn_a_prompt_kernel, lam_init=lam_init),
        grid_spec=grid_spec,
        out_shape=jax.ShapeDtypeStruct((s_len, GROUP), BF16),
        compiler_params=_cparams(("arbitrary",)),
        name="attn_a_prompt",
    )(qi_tab, kj_tab, lam, qa16, ka16, vat16, dmask, subln_g.reshape(PAIR, 1))


def _attn_a_sample_kernel(lam_ref, q_ref, kn_ref, vn_ref, kc_ref, vc_ref, g_ref, o_ref, *, lam_init):
    lam = lam_ref[0]
    past = vc_ref.shape[0] // H_A
    per_head = {}

    def head_operands(h):
        if h not in per_head:
            rows = slice(h * PAIR, (h + 1) * PAIR)
            per_head[h] = (kc_ref[rows, :].astype(BF16), kn_ref[:, rows],
                           vc_ref[pl.ds(h, past, stride=H_A), :].astype(BF16), vn_ref[:, rows])
        return per_head[h]

    def scores(c):
        kt_cache, k_new, _, _ = head_operands(c // 2)
        qz = _half_only(q_ref[:, c // 2 * PAIR:(c // 2 + 1) * PAIR], c % 2 == 1)
        return [jnp.dot(qz, kt_cache, preferred_element_type=F32),
                lax.dot_general(qz, k_new, (((1,), (1,)), ((), ())), preferred_element_type=F32)]

    pending = [scores(c) for c in range(ATT_AHEAD)]
    outs = []
    for c in range(N_COMP):
        scs = pending.pop(0)
        if c + ATT_AHEAD < N_COMP:
            pending.append(scores(c + ATT_AHEAD))
        m = functools.reduce(jnp.maximum, [jnp.max(sc, axis=1, keepdims=True) for sc in scs])
        l = None
        o = None
        for sc, v in zip(scs, head_operands(c // 2)[2:]):
            p = jnp.exp2(sc - m)
            ls = jnp.sum(p, axis=1, keepdims=True)
            os_ = jnp.dot(p.astype(BF16), v, preferred_element_type=F32)
            l = ls if l is None else l + ls
            o = os_ if o is None else o + os_
        outs.append(o / l)
    for h in range(H_A):
        o = _subln(outs[2 * h] - lam * outs[2 * h + 1], g_ref[...], lam_init, 1)
        o_ref[:, h * PAIR:(h + 1) * PAIR] = o.astype(BF16)


def _attn_a_sample(qa16, ka16, va16, cache_k, cache_v, lam, subln_g, row0, lam_init):
    nb = cache_k.shape[0]
    sd = CHUNK
    new = lambda b: (row0 // sd + b, 0)
    return pl.pallas_call(
        functools.partial(_attn_a_sample_kernel, lam_init=lam_init),
        grid=(nb,),
        in_specs=[
            pl.BlockSpec(memory_space=pltpu.SMEM),
            pl.BlockSpec((sd, GROUP), new), pl.BlockSpec((sd, GROUP), new), pl.BlockSpec((sd, GROUP), new),
            pl.BlockSpec((None,) + cache_k.shape[1:], lambda b: (b, 0, 0)),
            pl.BlockSpec((None,) + cache_v.shape[1:], lambda b: (b, 0, 0)),
            pl.BlockSpec((1, PAIR), lambda b: (0, 0)),
        ],
        out_specs=pl.BlockSpec((sd, GROUP), lambda b: (b, 0)),
        out_shape=jax.ShapeDtypeStruct((nb * sd, GROUP), BF16),
        compiler_params=_cparams(("parallel",)),
        name="attn_a_sample",
    )(lam, qa16, ka16, va16, cache_k, cache_v, subln_g.reshape(1, PAIR))


def _band_bias(rel_bias, n_q_chunks, first_pos):
    nq = n_q_chunks * CHUNK
    nk = nq + BAND_PAST
    heads = rel_bias.shape[0]
    n_lo = BAND_PAST - REL_CLIP
    period = max(nq + nk, n_lo + (2 * REL_CLIP + 1) + nq - 1)
    n_hi = period - (nq - 1) - n_lo - (2 * REL_CLIP + 1)
    assert n_lo >= 0
    lo = jnp.broadcast_to(rel_bias[:, :1], (heads, n_lo))
    hi = jnp.broadcast_to(rel_bias[:, -1:], (heads, n_hi))
    wrap = jnp.broadcast_to(rel_bias[:, :1], (heads, nq - 1))
    e = jnp.concatenate([lo, rel_bias, hi, wrap], axis=1).astype(F32)
    flat = jnp.broadcast_to(e[:, None, :], (heads, nq, period)).reshape(heads, nq * period)
    bias = flat[:, :nq * (period - 1)].reshape(heads, nq, period - 1)[:, :, :nk]
    qpos = jnp.arange(nq, dtype=I32)[:, None]
    kpos = jnp.arange(nk, dtype=I32)[None, :] - BAND_PAST
    dch = qpos // CHUNK - jnp.floor_divide(kpos, CHUNK)
    ok = (dch >= 0) & (dch <= BAND_CHUNKS) & (kpos + first_pos >= 0)
    return jnp.where(ok[None], bias * LOG2_E, NEG_INF)


def _band_heads(q_ref, ks, vs, bias_ref, o_ref):
    starts = [sum(k.shape[0] for k in ks[:n]) for n in range(len(ks))]

    def scores(head):
        cols = slice(head // 2 * PAIR, (head // 2 + 1) * PAIR)
        qz = _half_only(q_ref[:, cols], head % 2 == 1)
        out = []
        for k, col0 in zip(ks, starts):
            sc = lax.dot_general(qz, k[:, cols], (((1,), (1,)), ((), ())), preferred_element_type=F32)
            out.append(sc + bias_ref[head, :, col0:col0 + k.shape[0]])
        return out

    pending = [scores(h) for h in range(BAND_AHEAD)]
    res = []
    for head in range(H_B):
        scs = pending.pop(0)
        if head + BAND_AHEAD < H_B:
            pending.append(scores(head + BAND_AHEAD))
        cols = slice(head // 2 * PAIR, (head // 2 + 1) * PAIR)
        m = functools.reduce(jnp.maximum, [jnp.max(sc, axis=1, keepdims=True) for sc in scs])
        acc = None
        for sc, v in zip(scs, vs):
            vp = v[:, cols]
            v_and_ones = jnp.where(_half_mask(vp.shape, head % 2 == 1), vp, jnp.ones_like(vp))
            part = jnp.dot(jnp.exp2(sc - m).astype(BF16), v_and_ones, preferred_element_type=F32)
            acc = part if acc is None else acc + part
        res.append(acc)
    for j in range(H_B // 2):
        lower = _half_mask(res[2 * j].shape, False)
        num = jnp.where(lower, res[2 * j], res[2 * j + 1])
        den = jnp.where(lower, pltpu.roll(res[2 * j], D_HEAD_B, 1), pltpu.roll(res[2 * j + 1], D_HEAD_B, 1))
        o_ref[:, j * PAIR:(j + 1) * PAIR] = (num / den).astype(BF16)


def _attn_b_prompt_kernel(q_ref, k0_ref, k1_ref, k2_ref, v0_ref, v1_ref, v2_ref, bias_ref, o_ref):
    ks = [jnp.concatenate([k0_ref[...], k1_ref[...], k2_ref[...]], axis=0)]
    vs = [jnp.concatenate([v0_ref[...], v1_ref[...], v2_ref[...]], axis=0)]
    _band_heads(q_ref, ks, vs, bias_ref, o_ref)


def _attn_b_prompt(qb16, kb16, vb16, bias, s_len):
    assert BAND_PAST == 2 * BAND_TQ
    n_lead = bias.shape[0] - 1
    cur = lambda i: (i, 0)
    p1 = lambda i: (jnp.maximum(i - 1, 0), 0)
    p2 = lambda i: (jnp.maximum(i - 2, 0), 0)
    blk = lambda f: pl.BlockSpec((BAND_TQ, GROUP), f)
    return pl.pallas_call(
        _attn_b_prompt_kernel,
        grid=(s_len // BAND_TQ,),
        in_specs=[blk(cur), blk(p2), blk(p1), blk(cur), blk(p2), blk(p1), blk(cur),
                  pl.BlockSpec((None,) + bias.shape[1:], lambda i: (jnp.minimum(i, n_lead), 0, 0, 0))],
        out_specs=blk(cur),
        out_shape=jax.ShapeDtypeStruct((s_len, GROUP), BF16),
        compiler_params=_cparams(("parallel",)),
        name="attn_b_prompt",
    )(qb16, kb16, kb16, kb16, vb16, vb16, vb16, bias)


def _attn_b_sample_kernel(q_ref, kn_ref, vn_ref, kc_ref, vc_ref, bias_ref, o_ref):
    ks = [kc_ref[...].T.astype(BF16), kn_ref[...]]
    vs = [vc_ref[...].T.astype(BF16), vn_ref[...]]
    _band_heads(q_ref, ks, vs, bias_ref, o_ref)


def _attn_b_sample(qb16, kb16, vb16, cache_k, cache_v, bias, row0):
    nb, _, lb = cache_k.shape
    sd = CHUNK
    new = lambda b: (row0 // sd + b, 0)
    return pl.pallas_call(
        _attn_b_sample_kernel,
        grid=(nb,),
        in_specs=[
            pl.BlockSpec((sd, GROUP), new), pl.BlockSpec((sd, GROUP), new), pl.BlockSpec((sd, GROUP), new),
            pl.BlockSpec((None, GROUP, lb), lambda b: (b, 0, 0)),
            pl.BlockSpec((None, GROUP, lb), lambda b: (b, 0, 0)),
            pl.BlockSpec(bias.shape, lambda b: (0, 0, 0)),
        ],
        out_specs=pl.BlockSpec((sd, GROUP), lambda b: (b, 0)),
        out_shape=jax.ShapeDtypeStruct((nb * sd, GROUP), BF16),
        compiler_params=_cparams(("parallel",)),
        name="attn_b_sample",
    )(qb16, kb16, vb16, cache_k, cache_v, bias)


def _layer_norm(z, g, b):
    mu = jnp.mean(z, axis=-1, keepdims=True)
    var = jnp.mean(jnp.square(z - mu), axis=-1, keepdims=True)
    return (z - mu) * lax.rsqrt(var + LN_EPS) * g + b


def _to_token_tiles(ref, x):
    rows = x.shape[0]
    for j in range(TT):
        ref[pl.ds(j, rows, stride=TT), :] = x[:, j * LANES:(j + 1) * LANES]


def _from_token_tiles(ref, rows):
    return jnp.concatenate([ref[pl.ds(j, rows, stride=TT), :] for j in range(TT)], axis=1)


def _merge_kernel(oap_ref, obp_ref, oas_ref, obs_ref, xp_ref, xs_ref, wo_ref, g_ref, b_ref, wr_ref,
                  br_ref, x1_ref, ti_ref, gate_ref, *, n_prompt_blocks):
    is_prompt = pl.program_id(0) < n_prompt_blocks
    oa = jnp.where(is_prompt, oap_ref[...], oas_ref[...])
    ob = jnp.where(is_prompt, obp_ref[...], obs_ref[...])
    x = jnp.where(is_prompt, xp_ref[...], xs_ref[...])
    mix = jnp.dot(oa, wo_ref[:GROUP, :], preferred_element_type=F32)
    mix = mix + jnp.dot(ob, wo_ref[GROUP:, :], preferred_element_type=F32)
    x1 = _layer_norm(DEEPNORM_ALPHA * x + mix, g_ref[...], b_ref[...])
    _to_token_tiles(x1_ref, x1)
    logits = jnp.dot(x1.astype(BF16), wr_ref[...], preferred_element_type=F32) + br_ref[...]
    lane = lax.broadcasted_iota(I32, logits.shape, 1)
    logits = jnp.where(lane < N_EXPERTS, logits, -jnp.inf)
    vals, idxs = [], []
    for _ in range(TOP_K):
        mx = jnp.max(logits, axis=1, keepdims=True)
        ix = jnp.min(jnp.where(logits == mx, lane, LANES), axis=1, keepdims=True)
        vals.append(mx)
        idxs.append(ix)
        logits = jnp.where(lane == ix, -jnp.inf, logits)
    col = lax.broadcasted_iota(I32, ti_ref.shape, 1)
    top_v = jnp.broadcast_to(vals[-1], ti_ref.shape)
    top_i = jnp.broadcast_to(idxs[-1], ti_ref.shape)
    for k in range(TOP_K - 1):
        top_v = jnp.where(col == k, vals[k], top_v)
        top_i = jnp.where(col == k, idxs[k], top_i)
    ti_ref[...] = top_i
    e = jnp.exp(top_v - vals[0])
    gate_ref[...] = e / jnp.sum(e, axis=1, keepdims=True)


def _merge(oa_p, ob_p, oa_s, ob_s, x_p, x_s, wo16, ln_g, ln_b, wr16, br):
    t = x_p.shape[0] + x_s.shape[0]
    npb = x_p.shape[0] // MERGE_ROWS
    row = lambda i: (i, 0)
    const = lambda i: (0, 0)
    blk = lambda cols: pl.BlockSpec((MERGE_ROWS, cols), row)
    prompt = lambda cols: pl.BlockSpec((MERGE_ROWS, cols), lambda i: (jnp.minimum(i, npb - 1), 0))
    sample = lambda cols: pl.BlockSpec((MERGE_ROWS, cols), lambda i: (jnp.maximum(i - npb, 0), 0))
    return pl.pallas_call(
        functools.partial(_merge_kernel, n_prompt_blocks=npb),
        grid=(t // MERGE_ROWS,),
        in_specs=[prompt(GROUP), prompt(GROUP), sample(GROUP), sample(GROUP),
                  prompt(D_MODEL), sample(D_MODEL),
                  pl.BlockSpec(wo16.shape, const), pl.BlockSpec((1, D_MODEL), const),
                  pl.BlockSpec((1, D_MODEL), const), pl.BlockSpec(wr16.shape, const),
                  pl.BlockSpec((1, LANES), const)],
        out_specs=[pl.BlockSpec((MERGE_ROWS * TT, LANES), row), blk(TOP_K), blk(TOP_K)],
        out_shape=[jax.ShapeDtypeStruct((t * TT, LANES), F32), jax.ShapeDtypeStruct((t, TOP_K), I32),
                   jax.ShapeDtypeStruct((t, TOP_K), F32)],
        compiler_params=_cparams(("parallel",)),
        name="merge",
    )(oa_p, ob_p, oa_s, ob_s, x_p, x_s, wo16, ln_g.reshape(1, D_MODEL), ln_b.reshape(1, D_MODEL),
      wr16, br)


def _moe_kernel(blk_e_ref, nused_ref, fresh_ref, wslot_ref, next_e_ref, idx_hbm, x_hbm, wgu_hbm, wd_hbm,
                perm_ref, bgu_ref, bd_ref, out_hbm, idx_smem, xbuf, ybuf, wgu32, wd32, wgu16_ref,
                wd16_ref, isem, gsem, ssem, wsem, *, n_tok):
    i = pl.program_id(0)
    nused = nused_ref[0]
    n_assign = n_tok * TOP_K
    last = nused - 1
    par = lax.rem(i, 2)

    def idx_copy(block, slot):
        return pltpu.make_async_copy(idx_hbm.at[block], idx_smem.at[slot], isem.at[slot])

    def start_gather(islot, xslot):
        for r in range(MOE_ROWS):
            src = pl.multiple_of(idx_smem[islot, 0, r], TT)
            pltpu.make_async_copy(x_hbm.at[pl.ds(src, TT)], xbuf.at[xslot, pl.ds(r * TT, TT)],
                                  gsem.at[xslot]).start(priority=r % DMA_QUEUES)

    def wait_gather(xslot):
        pltpu.make_async_copy(x_hbm.at[pl.ds(0, MOE_ROWS * TT)], xbuf.at[xslot], gsem.at[xslot]).wait()

    def start_scatter(islot, yslot):
        for r in range(MOE_ROWS):
            dst = pl.multiple_of(idx_smem[islot, 1, r], TT)
            pltpu.make_async_copy(ybuf.at[yslot, pl.ds(r * TT, TT)], out_hbm.at[pl.ds(dst, TT)],
                                  ssem.at[yslot]).start(priority=r % DMA_QUEUES)

    def weight_copies(expert, wslot):
        return (pltpu.make_async_copy(wgu_hbm.at[expert], wgu32.at[wslot], wsem.at[wslot, 0]),
                pltpu.make_async_copy(wd_hbm.at[expert], wd32.at[wslot], wsem.at[wslot, 1]))

    def slot_flush(yslot, row0):
        return pltpu.make_async_copy(ybuf.at[yslot], out_hbm.at[pl.ds(row0 * TT, MOE_ROWS * TT)],
                                     ssem.at[yslot])

    @pl.when(i == 0)
    def _():
        ybuf[...] = jnp.zeros(ybuf.shape, F32)
        for p in range(2):
            slot_flush(p, n_assign + p * MOE_ROWS).start()
        idx_copy(0, 0).start()
        for c in weight_copies(blk_e_ref[0], 0):
            c.start()
        idx_copy(0, 0).wait()
        start_gather(0, 0)
        idx_copy(jnp.minimum(1, last), 1).start()

    @pl.when(i < nused)
    def _():
        s_cur = lax.rem(i, 3)
        s_nxt = lax.rem(i + 1, 3)
        s_nn = lax.rem(i + 2, 3)
        idx_copy(0, s_nxt).wait()
        start_gather(s_nxt, 1 - par)
        idx_copy(jnp.minimum(i + 2, last), s_nn).start()

        wslot = wslot_ref[i]

        @pl.when(fresh_ref[i] == 1)
        def _():
            for c in weight_copies(0, wslot):
                c.wait()
            for c in weight_copies(next_e_ref[i], 1 - wslot):
                c.start()
            for g in range(GU_GROUPS):
                cols = slice(g * GU_GROUP, (g + 1) * GU_GROUP)
                sorted_cols = jnp.dot(wgu32[wslot, :, cols].astype(BF16), perm_ref[...],
                                      preferred_element_type=F32)
                wgu16_ref[:, cols] = sorted_cols.astype(BF16)
            wd16_ref[...] = wd32[wslot].astype(BF16)

        wait_gather(par)
        x = _from_token_tiles(xbuf.at[par], MOE_ROWS).astype(BF16)
        h = jnp.dot(x, wgu16_ref[...], preferred_element_type=F32) + bgu_ref[...]
        half = GU_GROUP // 2
        hg = jnp.concatenate([h[:, g * GU_GROUP:g * GU_GROUP + half] for g in range(GU_GROUPS)], axis=1)
        hu = jnp.concatenate([h[:, g * GU_GROUP + half:(g + 1) * GU_GROUP] for g in range(GU_GROUPS)],
                             axis=1)
        gate = jnp.minimum(hg, SWIGLU_LIMIT)
        up = jnp.clip(hu, -SWIGLU_LIMIT, SWIGLU_LIMIT)
        glu = gate * (1.0 / (1.0 + jnp.exp(-(gate * SWIGLU_ALPHA))))
        act = ((up + 1.0) * glu).astype(BF16)
        y = jnp.dot(act, wd16_ref[...], preferred_element_type=F32) + bd_ref[...]

        slot_flush(par, 0).wait()
        _to_token_tiles(ybuf.at[par], y)
        start_scatter(s_cur, par)

        @pl.when(i == last)
        def _():
            slot_flush(par, 0).wait()
            slot_flush(1 - par, 0).wait()
            wait_gather(1 - par)
            idx_copy(0, s_nn).wait()
            for c in weight_copies(0, 1 - wslot):
                c.wait()


def _gate_up_sorter():
    src = jnp.arange(GU_GROUP, dtype=I32)[:, None]
    dst = jnp.arange(GU_GROUP, dtype=I32)[None, :]
    half = GU_GROUP // 2
    return (src == jnp.where(dst < half, 2 * dst, 2 * (dst - half) + 1)).astype(BF16)


def _sort_gate_up(b):
    lead = b.shape[:-1]
    return b.reshape(lead + (GU_GROUPS, GU_GROUP // 2, 2)).swapaxes(-1, -2).reshape(lead + (-1,))


def _moe(x1, idx, blk_e, nused, fresh, wslot, next_e, w_gate_up, w_down, bgu, bd):
    t = x1.shape[0] // TT
    nblk = idx.shape[0]
    wmap = lambda i, be, nu, fr, ws, ne: (be[i], 0, 0)
    const = lambda i, be, nu, fr, ws, ne: (0, 0)
    grid_spec = pltpu.PrefetchScalarGridSpec(
        num_scalar_prefetch=5,
        grid=(nblk,),
        in_specs=[
            pl.BlockSpec(memory_space=pl.ANY),
            pl.BlockSpec(memory_space=pl.ANY),
            pl.BlockSpec(memory_space=pl.ANY),
            pl.BlockSpec(memory_space=pl.ANY),
            pl.BlockSpec((GU_GROUP, GU_GROUP), const),
            pl.BlockSpec((None, 1, 2 * D_MODEL), wmap),
            pl.BlockSpec((None, 1, D_MODEL), wmap),
        ],
        out_specs=pl.BlockSpec(memory_space=pl.ANY),
        scratch_shapes=[
            pltpu.SMEM((3, 2, MOE_ROWS), I32),
            pltpu.VMEM((2, MOE_ROWS * TT, LANES), F32),
            pltpu.VMEM((2, MOE_ROWS * TT, LANES), F32),
            pltpu.VMEM((2, D_MODEL, 2 * D_MODEL), F32),
            pltpu.VMEM((2, D_MODEL, D_MODEL), F32),
            pltpu.VMEM((D_MODEL, 2 * D_MODEL), BF16),
            pltpu.VMEM((D_MODEL, D_MODEL), BF16),
            pltpu.SemaphoreType.DMA((3,)),
            pltpu.SemaphoreType.DMA((2,)),
            pltpu.SemaphoreType.DMA((2,)),
            pltpu.SemaphoreType.DMA((2, 2)),
        ],
    )
    return pl.pallas_call(
        functools.partial(_moe_kernel, n_tok=t),
        grid_spec=grid_spec,
        out_shape=jax.ShapeDtypeStruct(((t * TOP_K + 2 * MOE_ROWS) * TT, LANES), F32),
        compiler_params=_cparams(("arbitrary",)),
        name="moe",
    )(blk_e, nused, fresh, wslot, next_e, idx, x1, w_gate_up, w_down, _gate_up_sorter(), bgu, bd)


def _route(top_i):
    t = top_i.shape[0]
    n = t * TOP_K
    nblk = n // MOE_ROWS + N_EXPERTS
    flat_e = top_i.reshape(-1)
    order = jnp.argsort(flat_e, stable=True).astype(I32)
    experts = jnp.arange(N_EXPERTS, dtype=I32)
    counts = jnp.sum((flat_e[:, None] == experts[None, :]).astype(I32), axis=0)
    padded = ((counts + MOE_ROWS - 1) // MOE_ROWS) * MOE_ROWS
    pad_end = jnp.cumsum(padded)
    pad_start = pad_end - padded
    start = jnp.cumsum(counts) - counts
    nused = (pad_end[-1] // MOE_ROWS).astype(I32).reshape(1)
    blk_first = jnp.arange(nblk, dtype=I32) * MOE_ROWS
    blk_e = jnp.minimum(jnp.sum((blk_first[:, None] >= pad_end[None, :]).astype(I32), axis=1),
                        N_EXPERTS - 1)
    row = jnp.arange(MOE_ROWS, dtype=I32)[None, :]
    j = blk_first[:, None] + row - pad_start[blk_e][:, None]
    valid = j < counts[blk_e][:, None]
    a = order[jnp.clip(start[blk_e][:, None] + j, 0, n - 1)]
    tok = lax.shift_right_logical(a, TOP_SHIFT)
    src = jnp.where(valid, tok, 0)
    dump = n + (jnp.arange(nblk, dtype=I32)[:, None] % 2) * MOE_ROWS + row
    dst = jnp.where(valid, (a & (TOP_K - 1)) * t + tok, dump)
    blk_e = blk_e.astype(I32)
    fresh = jnp.concatenate([jnp.ones((1,), I32), (blk_e[1:] != blk_e[:-1]).astype(I32)])
    wslot = (jnp.cumsum(fresh) - 1) % 2
    blk = jnp.arange(nblk, dtype=I32)
    nxt = lax.cummin(jnp.where(fresh == 1, blk, nblk)[::-1])[::-1]
    nxt = jnp.concatenate([nxt[1:], jnp.full((1,), nblk, I32)])
    next_e = jnp.where(nxt < nblk, blk_e[jnp.minimum(nxt, nblk - 1)], blk_e)
    return (jnp.stack([src * TT, dst * TT], axis=1).astype(I32), blk_e, nused, fresh,
            wslot.astype(I32), next_e.astype(I32))


def _final_kernel(x1_ref, gate_ref, y0_ref, y1_ref, y2_ref, y3_ref, g_ref, b_ref, o_ref):
    gates = gate_ref[...]
    y = gates[:, 0:1] * _from_token_tiles(y0_ref, FINAL_ROWS)
    for k, ref in enumerate((y1_ref, y2_ref, y3_ref), start=1):
        y = y + gates[:, k:k + 1] * _from_token_tiles(ref, FINAL_ROWS)
    x1 = _from_token_tiles(x1_ref, FINAL_ROWS)
    o_ref[...] = _layer_norm(DEEPNORM_ALPHA * x1 + y, g_ref[...], b_ref[...])


def _final(x1, gates, planes, ln_g, ln_b, row0, rows):
    nb_all = gates.shape[0] // FINAL_ROWS
    b0 = row0 // FINAL_ROWS
    row = lambda i: (b0 + i, 0)
    const = lambda i: (0, 0)
    tiles = lambda f: pl.BlockSpec((FINAL_ROWS * TT, LANES), f)
    plane = lambda k: tiles(lambda i: (k * nb_all + b0 + i, 0))
    return pl.pallas_call(
        _final_kernel,
        grid=(rows // FINAL_ROWS,),
        in_specs=[tiles(row), pl.BlockSpec((FINAL_ROWS, TOP_K), row),
                  plane(0), plane(1), plane(2), plane(3),
                  pl.BlockSpec((1, D_MODEL), const), pl.BlockSpec((1, D_MODEL), const)],
        out_specs=pl.BlockSpec((FINAL_ROWS, D_MODEL), lambda i: (i, 0)),
        out_shape=jax.ShapeDtypeStruct((rows, D_MODEL), F32),
        compiler_params=_cparams(("parallel",)),
        name="final",
    )(x1, gates, planes, planes, planes, planes, ln_g.reshape(1, D_MODEL), ln_b.reshape(1, D_MODEL))


def kernel(x_prompt, x_sample, cache_a_k, cache_a_v, cache_b_k, cache_b_v, w_in, lambda_qk, subln_g,
           rel_bias, w_out, ln1_g, ln1_b, w_router, b_router, w_gate_up, b_gate_up, w_down, b_down,
           ln2_g, ln2_b):
    b, s, d = x_prompt.shape
    bd, sd, _ = x_sample.shape
    depth, _, past, _, _ = cache_a_k.shape
    lb = cache_b_k.shape[2]
    assert depth == 1 and b == 1 and d == D_MODEL
    assert sd == CHUNK and past % CHUNK == 0 and lb == BAND_PAST and s >= BAND_PAST
    assert s % ATT_TQ == 0 and s % BAND_TQ == 0
    ts = bd * sd
    t = s + ts
    lam_init = _lambda_init(0)

    x_p = x_prompt.reshape(s, d)
    x_s = x_sample.reshape(ts, d)
    pos_p = jnp.arange(s, dtype=I32)
    pos_s = jnp.tile(past + jnp.arange(sd, dtype=I32), bd)
    lq = lambda_qk[0].astype(F32)
    lam = (jnp.exp(jnp.sum(lq[0] * lq[1])) - jnp.exp(jnp.sum(lq[2] * lq[3])) + lam_init).reshape(1)

    w_in16 = w_in[0].astype(BF16)
    (qa_p, ka32_p, ka_p, va32_p, _, vat_p, qb_p, kb32_p, kb_p, vb32_p, vb_p) = _project(x_p, w_in16, pos_p)
    (qa_s, ka32_s, ka_s, va32_s, va_s, _, qb_s, kb32_s, kb_s, vb32_s, vb_s) = _project(x_s, w_in16, pos_s)

    oa_p = _attn_a_prompt(qa_p, ka_p, vat_p, lam, subln_g[0], s, lam_init)
    kt_a = jnp.transpose(cache_a_k[0], (0, 2, 3, 1)).reshape(bd, GROUP, past)
    oa_s = _attn_a_sample(qa_s, ka_s, va_s, kt_a, cache_a_v[0].reshape(bd, past * H_A, PAIR),
                          lam, subln_g[0], 0, lam_init)

    n_lead = BAND_PAST // BAND_TQ
    bias_p = jnp.stack([_band_bias(rel_bias[0], BAND_TQ // CHUNK, i * BAND_TQ) for i in range(n_lead + 1)])
    ob_p = _attn_b_prompt(qb_p, kb_p, vb_p, bias_p, s)
    transposed = lambda c: jnp.transpose(c, (0, 2, 3, 1)).reshape(bd, GROUP, -1)
    ob_s = _attn_b_sample(qb_s, kb_s, vb_s, transposed(cache_b_k[0]), transposed(cache_b_v[0]),
                          _band_bias(rel_bias[0], 1, past), 0)

    wr16 = jnp.pad(w_router[0], ((0, 0), (0, LANES - N_EXPERTS))).astype(BF16)
    br = jnp.pad(b_router[0], (0, LANES - N_EXPERTS)).reshape(1, LANES)
    x1, top_i, gates = _merge(oa_p, ob_p, oa_s, ob_s, x_p, x_s, w_out[0].astype(BF16), ln1_g[0],
                              ln1_b[0], wr16, br)

    idx, blk_e, nused, fresh, wslot, next_e = _route(top_i)
    planes = _moe(x1, idx, blk_e, nused, fresh, wslot, next_e, w_gate_up[0], w_down[0],
                  _sort_gate_up(b_gate_up[0]).reshape(N_EXPERTS, 1, 2 * D_MODEL),
                  b_down[0].reshape(N_EXPERTS, 1, D_MODEL))
    y_p = _final(x1, gates, planes, ln2_g[0], ln2_b[0], 0, s)
    y_s = _final(x1, gates, planes, ln2_g[0], ln2_b[0], s, ts)

    heads_a = lambda a, n: a.reshape(1, n, -1, N_COMP, D_HEAD_A)
    vals_a = lambda a, n: a.reshape(1, n, -1, H_A, 2 * D_HEAD_A)
    heads_b = lambda a, n: a.reshape(1, n, -1, H_B, D_HEAD_B)
    keep_s = min(BAND_PAST, lb + sd)
    kb_new = jnp.concatenate([cache_b_k[0], heads_b(kb32_s, bd)[0]], axis=1)[:, lb + sd - keep_s:]
    vb_new = jnp.concatenate([cache_b_v[0], heads_b(vb32_s, bd)[0]], axis=1)[:, lb + sd - keep_s:]
    return (y_p.reshape(b, s, d), y_s.reshape(bd, sd, d),
            heads_a(ka32_p, b), vals_a(va32_p, b),
            heads_b(kb32_p[s - BAND_PAST:], b), heads_b(vb32_p[s - BAND_PAST:], b),
            heads_a(ka32_s, bd), vals_a(va32_s, bd),
            kb_new[None], vb_new[None])
```
